```python
import jax, jax.numpy as jnp
from jax import lax
import numpy as np

D_MODEL = 2048
BATCH = 8
SEQ = 4096
DEPTH = 1
DEC_BATCH = 16
DEC_SEQ = 2048
PAST_LEN = 128

MIX_WIDTH = D_MODEL
F_WIDTH = MIX_WIDTH // 2
F_GROUPS = 8
F_CH = F_WIDTH // F_GROUPS
A_WIDTH = MIX_WIDTH - F_WIDTH
HEAD_DIM = 64
N_Q_HEADS = A_WIDTH // HEAD_DIM
N_KV_HEADS = 4
GQA_GROUP = N_Q_HEADS // N_KV_HEADS
KV_WIDTH = N_KV_HEADS * HEAD_DIM
IN_WIDTH = F_WIDTH + A_WIDTH + 2 * KV_WIDTH
WINDOW = 128
BLOCK = 128
ROPE_THETA = 10000.0
D_FF = 4 * D_MODEL
EPS = 1e-6
NEG_INF = -1e30

kernel_name = "hymba_fnet_swa_encoder"


def rmsnorm(x, g):
    xf = x.astype(jnp.float32)
    y = xf * lax.rsqrt(jnp.mean(xf * xf, axis=-1, keepdims=True) + EPS)
    return (y * g.astype(jnp.float32)).astype(x.dtype)


def rope_tables(seq_len):
    inv_freq = 1.0 / (ROPE_THETA ** (jnp.arange(0, HEAD_DIM, 2, dtype=jnp.float32) / HEAD_DIM))
    pos = jnp.arange(seq_len, dtype=jnp.float32)
    ang = pos[:, None] * inv_freq[None, :]
    ang = jnp.concatenate([ang, ang], axis=-1)
    return jnp.cos(ang), jnp.sin(ang)


def apply_rope(x, cos, sin):
    half = HEAD_DIM // 2
    x1, x2 = x[..., :half], x[..., half:]
    rot = jnp.concatenate([-x2, x1], axis=-1)
    c = cos[None, :, None, :].astype(x.dtype)
    s = sin[None, :, None, :].astype(x.dtype)
    return x * c + rot * s


def fourier_mix(zf, w_f):
    spec = jnp.fft.fft2(zf.astype(jnp.float32), axes=(1, 3), norm='ortho')
    re = spec.real.astype(zf.dtype)
    return jnp.einsum('bsgc,gcd->bsgd', re, w_f)


def window_attention(q, k, v, sink):
    B, S = q.shape[0], q.shape[1]
    nb = S // BLOCK
    qb = q.reshape(B, nb, BLOCK, N_KV_HEADS, GQA_GROUP, HEAD_DIM)
    pad = ((0, 0), (BLOCK, BLOCK), (0, 0), (0, 0))
    kp = jnp.pad(k, pad).reshape(B, nb + 2, BLOCK, N_KV_HEADS, HEAD_DIM)
    vp = jnp.pad(v, pad).reshape(B, nb + 2, BLOCK, N_KV_HEADS, HEAD_DIM)
    kb = jnp.concatenate([kp[:, :-2], kp[:, 1:-1], kp[:, 2:]], axis=2)
    vb = jnp.concatenate([vp[:, :-2], vp[:, 1:-1], vp[:, 2:]], axis=2)
    scale = HEAD_DIM ** -0.5
    s = jnp.einsum('bnqhgd,bnkhd->bnhgqk', qb, kb).astype(jnp.float32) * scale
    blk = jnp.arange(nb)[:, None] * BLOCK
    qpos = blk + jnp.arange(BLOCK)[None, :]
    kpos = blk - BLOCK + jnp.arange(3 * BLOCK)[None, :]
    valid = (jnp.abs(qpos[:, :, None] - kpos[:, None, :]) <= WINDOW) \
        & (kpos >= 0)[:, None, :] & (kpos < S)[:, None, :]
    s = jnp.where(valid[None, :, None, None, :, :], s, NEG_INF)
    sink_l = sink.astype(jnp.float32).reshape(N_KV_HEADS, GQA_GROUP)[None, None, :, :, None, None]
    m = jnp.maximum(jnp.max(s, axis=-1, keepdims=True), sink_l)
    p = jnp.exp(s - m)
    p = p / (jnp.sum(p, axis=-1, keepdims=True) + jnp.exp(sink_l - m))
    o = jnp.einsum('bnhgqk,bnkhd->bnqhgd', p.astype(v.dtype), vb)
    return o.reshape(B, S, N_Q_HEADS, HEAD_DIM)


def hybrid_layer(x, ln_mix_g, w_in, w_fourier, attn_sink, out_norm_fourier_g,
                 out_norm_attn_g, w_out, ln_mlp_g, w_up, w_down):
    B, S, _ = x.shape
    h = rmsnorm(x, ln_mix_g)
    z = h @ w_in
    o1 = F_WIDTH
    o2 = o1 + A_WIDTH
    o3 = o2 + KV_WIDTH
    zf = z[..., :o1].reshape(B, S, F_GROUPS, F_CH)
    q = z[..., o1:o2].reshape(B, S, N_Q_HEADS, HEAD_DIM)
    k = z[..., o2:o3].reshape(B, S, N_KV_HEADS, HEAD_DIM)
    v = z[..., o3:].reshape(B, S, N_KV_HEADS, HEAD_DIM)
    of = fourier_mix(zf, w_fourier).reshape(B, S, F_WIDTH)
    cos, sin = rope_tables(S)
    q = apply_rope(q, cos, sin)
    k = apply_rope(k, cos, sin)
    oa = window_attention(q, k, v, attn_sink).reshape(B, S, A_WIDTH)
    mixed = jnp.concatenate([rmsnorm(of, out_norm_fourier_g), rmsnorm(oa, out_norm_attn_g)], axis=-1)
    x = x + mixed @ w_out
    h = rmsnorm(x, ln_mlp_g)
    u = jax.nn.relu(h @ w_up)
    x = x + (u * u) @ w_down
    return x


def setup_inputs(seed: int = 0) -> dict:
    key = jax.random.key(seed)
    ks = jax.random.split(key, 16)
    f32 = jnp.float32
    nrm = lambda k, shape, s: jax.random.normal(k, shape, f32) * s
    return {
        'x_prompt': nrm(ks[0], (BATCH, SEQ, D_MODEL), 1.0),
        'x_sample': nrm(ks[1], (DEC_BATCH, DEC_SEQ, D_MODEL), 1.0),
        'ln_mix_g': 1.0 + nrm(ks[2], (DEPTH, D_MODEL), 0.02),
        'w_in': nrm(ks[3], (DEPTH, D_MODEL, IN_WIDTH), D_MODEL ** -0.5),
        'w_fourier': nrm(ks[4], (DEPTH, F_GROUPS, F_CH, F_CH), F_CH ** -0.5),
        'attn_sink': nrm(ks[5], (DEPTH, N_Q_HEADS), 0.5),
        'out_norm_fourier_g': 1.0 + nrm(ks[6], (DEPTH, F_WIDTH), 0.02),
        'out_norm_attn_g': 1.0 + nrm(ks[7], (DEPTH, A_WIDTH), 0.02),
        'w_out': nrm(ks[8], (DEPTH, MIX_WIDTH, D_MODEL), MIX_WIDTH ** -0.5),
        'ln_mlp_g': 1.0 + nrm(ks[9], (DEPTH, D_MODEL), 0.02),
        'w_up': nrm(ks[10], (DEPTH, D_MODEL, D_FF), D_MODEL ** -0.5),
        'w_down': nrm(ks[11], (DEPTH, D_FF, D_MODEL), D_FF ** -0.5),
        'ln_final_g': 1.0 + nrm(ks[12], (D_MODEL,), 0.02),
    }


def trunk(x, ln_mix_g, w_in, w_fourier, attn_sink, out_norm_fourier_g,
          out_norm_attn_g, w_out, ln_mlp_g, w_up, w_down, ln_final_g):
    for l in range(DEPTH):
        x = hybrid_layer(x, ln_mix_g[l], w_in[l], w_fourier[l], attn_sink[l],
                         out_norm_fourier_g[l], out_norm_attn_g[l], w_out[l],
                         ln_mlp_g[l], w_up[l], w_down[l])
    return rmsnorm(x, ln_final_g)


def reference(x_prompt, x_sample, ln_mix_g, w_in, w_fourier, attn_sink,
              out_norm_fourier_g, out_norm_attn_g, w_out, ln_mlp_g, w_up,
              w_down, ln_final_g):
    y_prompt = trunk(x_prompt, ln_mix_g, w_in, w_fourier, attn_sink, out_norm_fourier_g,
                     out_norm_attn_g, w_out, ln_mlp_g, w_up, w_down, ln_final_g)
    y_sample = trunk(x_sample, ln_mix_g, w_in, w_fourier, attn_sink, out_norm_fourier_g,
                     out_norm_attn_g, w_out, ln_mlp_g, w_up, w_down, ln_final_g)
    return (y_prompt, y_sample)
```

```python
import functools
import math

import jax
import jax.numpy as jnp
from jax import lax
from jax.experimental import pallas as pl
from jax.experimental.pallas import tpu as pltpu

D_MODEL = 2048
F_WIDTH = 1024
F_GROUPS = 8
F_CH = 128
A_WIDTH = 1024
HEAD_DIM = 64
HALF_DIM = HEAD_DIM // 2
N_Q_HEADS = 16
N_KV_HEADS = 4
GQA_GROUP = 4
KV_WIDTH = 256
WINDOW = 128
BLOCK = 128
ROPE_THETA = 10000.0
D_FF = 4 * D_MODEL
EPS = 1e-6
NEG_INF = -1e30

LANES = 128
VMEM_LIMIT = 56 * 1024 * 1024

BF16 = jnp.bfloat16
F32 = jnp.float32

_NT = (((1,), (1,)), ((), ()))


def _params(*sem):
    return pltpu.CompilerParams(dimension_semantics=sem, vmem_limit_bytes=VMEM_LIMIT)


def _resident(shape):
    nd = len(shape)
    return pl.BlockSpec(shape, lambda *_: (0,) * nd, pipeline_mode=pl.Buffered(1))


def _fourier_weights_kernel(cs_ref, w_ref, o_ref):
    w = w_ref[...]
    c = jnp.dot(cs_ref[0], w, preferred_element_type=F32, precision=lax.Precision.HIGHEST)
    s = jnp.dot(cs_ref[1], w, preferred_element_type=F32, precision=lax.Precision.HIGHEST)
    o_ref[:, :F_CH] = c.astype(BF16)
    o_ref[:, F_CH:] = s.astype(BF16)


def _fourier_weights(w_fourier):
    n = jnp.arange(F_CH, dtype=jnp.int32)
    ang = ((n[:, None] * n[None, :]) % F_CH).astype(F32) * (2.0 * math.pi / F_CH)
    cs = jnp.stack([jnp.cos(ang), jnp.sin(ang)]) * (F_CH ** -0.5)
    return pl.pallas_call(
        _fourier_weights_kernel,
        grid=(F_GROUPS,),
        in_specs=[pl.BlockSpec((2, F_CH, F_CH), lambda g: (0, 0, 0)),
                  pl.BlockSpec((None, F_CH, F_CH), lambda g: (g, 0, 0))],
        out_specs=pl.BlockSpec((None, F_CH, 2 * F_CH), lambda g: (g, 0, 0)),
        out_shape=jax.ShapeDtypeStruct((F_GROUPS, F_CH, 2 * F_CH), BF16),
        name="fourier_weights",
    )(cs, w_fourier)


def _in_proj_kernel(x_ref, g_ref, wf_ref, wqt_ref, wk_ref, wvt_ref, mg_ref,
                    cost_ref, sint_ref, cos_ref, sin_ref,
                    ab_ref, qt_ref, k_ref, vt_ref):
    x = x_ref[...]
    r = lax.rsqrt(jnp.mean(x * x, axis=-1, keepdims=True) + EPS)
    h = ((x * r) * g_ref[...]).astype(BF16)

    zf = jnp.dot(h, wf_ref[...], preferred_element_type=F32).astype(BF16)
    for g in range(F_GROUPS):
        lo = g * F_CH
        ab = jnp.dot(zf[:, lo:lo + F_CH], mg_ref[g], preferred_element_type=F32)
        ab_ref[0, :, lo:lo + F_CH] = ab[:, :F_CH].astype(BF16)
        ab_ref[1, :, lo:lo + F_CH] = ab[:, F_CH:].astype(BF16)

    qt = lax.dot_general(wqt_ref[...], h, _NT, preferred_element_type=F32)
    cost = cost_ref[...]
    sint = sint_ref[...]
    scale = HEAD_DIM ** -0.5
    for hd in range(N_Q_HEADS):
        lo = hd * HEAD_DIM
        blk = qt[lo:lo + HEAD_DIM]
        rot = jnp.concatenate([blk[HALF_DIM:], blk[:HALF_DIM]], axis=0)
        qt_ref[lo:lo + HEAD_DIM, :] = ((blk * cost + rot * sint) * scale).astype(BF16)

    kk = jnp.dot(h, wk_ref[...], preferred_element_type=F32)
    cos = cos_ref[...]
    sin = sin_ref[...]
    first_half = (lax.broadcasted_iota(jnp.int32, cos.shape, 1) % HEAD_DIM) < HALF_DIM
    for c in range(KV_WIDTH // LANES):
        blk = kk[:, c * LANES:(c + 1) * LANES]
        rot = jnp.where(first_half,
                        pltpu.roll(blk, LANES - HALF_DIM, axis=1),
                        pltpu.roll(blk, HALF_DIM, axis=1))
        k_ref[:, c * LANES:(c + 1) * LANES] = (blk * cos + rot * sin).astype(BF16)

    vt = lax.dot_general(wvt_ref[...], h, _NT, preferred_element_type=F32)
    vt_ref[...] = vt.astype(BF16)


def _in_proj(x, g, wf, wqt, wk, wvt, mg, cost, sint, cos, sin, tm):
    B, S, _ = x.shape
    return pl.pallas_call(
        _in_proj_kernel,
        grid=(B, S // tm),
        in_specs=[
            pl.BlockSpec((None, tm, D_MODEL), lambda b, s: (b, s, 0)),
            _resident((1, D_MODEL)),
            _resident((D_MODEL, F_WIDTH)),
            _resident((A_WIDTH, D_MODEL)),
            _resident((D_MODEL, KV_WIDTH)),
            _resident((KV_WIDTH, D_MODEL)),
            _resident((F_GROUPS, F_CH, 2 * F_CH)),
            pl.BlockSpec((HEAD_DIM, tm), lambda b, s: (0, s)),
            pl.BlockSpec((HEAD_DIM, tm), lambda b, s: (0, s)),
            pl.BlockSpec((tm, LANES), lambda b, s: (s, 0)),
            pl.BlockSpec((tm, LANES), lambda b, s: (s, 0)),
        ],
        out_specs=[
            pl.BlockSpec((None, 2, tm, F_WIDTH), lambda b, s: (b, 0, s, 0)),
            pl.BlockSpec((None, A_WIDTH, tm), lambda b, s: (b, 0, s)),
            pl.BlockSpec((None, tm, KV_WIDTH), lambda b, s: (b, s, 0)),
            pl.BlockSpec((None, KV_WIDTH, tm), lambda b, s: (b, 0, s)),
        ],
        out_shape=[
            jax.ShapeDtypeStruct((B, 2, S, F_WIDTH), BF16),
            jax.ShapeDtypeStruct((B, A_WIDTH, S), BF16),
            jax.ShapeDtypeStruct((B, S, KV_WIDTH), BF16),
            jax.ShapeDtypeStruct((B, KV_WIDTH, S), BF16),
        ],
        compiler_params=_params("parallel", "parallel"),
        name="in_proj",
    )(x, g, wf, wqt, wk, wvt, mg, cost, sint, cos, sin)


def _seq_dft_kernel(d_ref, ab_ref, g_ref, o_ref, acc_ref, *, n_split):
    j = pl.program_id(2)
    tn = F_WIDTH // n_split
    res = jnp.dot(d_ref[...], ab_ref[...], preferred_element_type=F32)
    for c in range(n_split):
        @pl.when(j == c)
        def _(c=c):
            acc_ref[:, c * tn:(c + 1) * tn] = res

    @pl.when(j == n_split - 1)
    def _():
        y = acc_ref[...]
        r = lax.rsqrt(jnp.mean(y * y, axis=-1, keepdims=True) + EPS)
        o_ref[...] = ((y * r) * g_ref[...]).astype(BF16)


def _seq_dft(dmat, ab, g, tm, n_split):
    B, S2, _ = ab.shape
    S = S2 // 2
    tn = F_WIDTH // n_split
    return pl.pallas_call(
        functools.partial(_seq_dft_kernel, n_split=n_split),
        grid=(S // tm, B, n_split),
        in_specs=[
            pl.BlockSpec((tm, S2), lambda i, b, j: (i, 0)),
            pl.BlockSpec((None, S2, tn), lambda i, b, j: (b, 0, j)),
            _resident((1, F_WIDTH)),
        ],
        out_specs=pl.BlockSpec((None, tm, F_WIDTH), lambda i, b, j: (b, i, 0)),
        out_shape=jax.ShapeDtypeStruct((B, S, F_WIDTH), BF16),
        scratch_shapes=[pltpu.VMEM((tm, F_WIDTH), F32)],
        compiler_params=_params("parallel", "parallel", "arbitrary"),
        name="seq_dft",
    )(dmat, ab, g)


def _window_attn_kernel(sink_ref, qt_ref, k_ref, kp_ref, kn_ref, vt_ref, vtp_ref, vtn_ref, g_ref,
                        o_ref, kfull, vtfull, ot_all, *, tq, n_blocks):
    s_idx = pl.program_id(1)
    r_blocks = tq // BLOCK
    span = 3 * BLOCK

    kfull[0:BLOCK, :] = kp_ref[...]
    kfull[BLOCK:BLOCK + tq, :] = k_ref[...]
    kfull[BLOCK + tq:, :] = kn_ref[...]
    vtfull[:, 0:BLOCK] = vtp_ref[...]
    vtfull[:, BLOCK:BLOCK + tq] = vt_ref[...]
    vtfull[:, BLOCK + tq:] = vtn_ref[...]

    key_i = lax.broadcasted_iota(jnp.int32, (span, BLOCK), 0)
    qry_i = lax.broadcasted_iota(jnp.int32, (span, BLOCK), 1)
    diff = key_i - qry_i
    band = (diff >= BLOCK - WINDOW) & (diff <= BLOCK + WINDOW)

    for n in range(r_blocks):
        nglob = s_idx * r_blocks + n
        lo = jnp.where(nglob == 0, BLOCK, 0)
        hi = jnp.where(nglob == n_blocks - 1, 2 * BLOCK, span)
        valid = band & (key_i >= lo) & (key_i < hi)
        kwin = kfull[n * BLOCK:n * BLOCK + span, :]
        for h in range(N_KV_HEADS):
            r0 = h * HEAD_DIM
            q4 = jnp.concatenate(
                [qt_ref[(h * GQA_GROUP + g) * HEAD_DIM:(h * GQA_GROUP + g + 1) * HEAD_DIM,
                        n * BLOCK:(n + 1) * BLOCK] for g in range(GQA_GROUP)], axis=1)
            parts = [q4]
            if r0 > 0:
                parts.insert(0, jnp.zeros((r0, GQA_GROUP * BLOCK), BF16))
            if r0 + HEAD_DIM < KV_WIDTH:
                parts.append(jnp.zeros((KV_WIDTH - r0 - HEAD_DIM, GQA_GROUP * BLOCK), BF16))
            qpad = jnp.concatenate(parts, axis=0)
            st_all = jnp.dot(kwin, qpad, preferred_element_type=F32)
            pn = []
            for g in range(GQA_GROUP):
                st = jnp.where(valid, st_all[:, g * BLOCK:(g + 1) * BLOCK], NEG_INF)
                sink = jnp.full((1, BLOCK), sink_ref[h * GQA_GROUP + g], F32)
                m = jnp.maximum(jnp.max(st, axis=0, keepdims=True), sink)
                p = jnp.exp(st - m)
                den = jnp.sum(p, axis=0, keepdims=True) + jnp.exp(sink - m)
                pn.append((p * (1.0 / den)).astype(BF16))
            vwin = vtfull[r0:r0 + HEAD_DIM, n * BLOCK:n * BLOCK + span]
            ot = jnp.dot(vwin, jnp.concatenate(pn, axis=1), preferred_element_type=F32)
            for g in range(GQA_GROUP):
                q0 = (h * GQA_GROUP + g) * HEAD_DIM
                ot_all[q0:q0 + HEAD_DIM, n * BLOCK:(n + 1) * BLOCK] = ot[:, g * BLOCK:(g + 1) * BLOCK]

    y = ot_all[...]
    r = lax.rsqrt(jnp.mean(y * y, axis=0, keepdims=True) + EPS)
    o_ref[...] = ((y * r).T * g_ref[...]).astype(BF16)


def _window_attn(sink, qt, k, vt, g, tq):
    B, _, S = qt.shape
    n_blocks = S // BLOCK
    r_blocks = tq // BLOCK
    prev_blk = lambda s: jnp.maximum(s * r_blocks - 1, 0)
    next_blk = lambda s: jnp.minimum((s + 1) * r_blocks, n_blocks - 1)
    return pl.pallas_call(
        functools.partial(_window_attn_kernel, tq=tq, n_blocks=n_blocks),
        grid=(B, S // tq),
        in_specs=[
            pl.BlockSpec(memory_space=pltpu.SMEM),
            pl.BlockSpec((None, A_WIDTH, tq), lambda b, s: (b, 0, s)),
            pl.BlockSpec((None, tq, KV_WIDTH), lambda b, s: (b, s, 0)),
            pl.BlockSpec((None, BLOCK, KV_WIDTH), lambda b, s: (b, prev_blk(s), 0)),
            pl.BlockSpec((None, BLOCK, KV_WIDTH), lambda b, s: (b, next_blk(s), 0)),
            pl.BlockSpec((None, KV_WIDTH, tq), lambda b, s: (b, 0, s)),
            pl.BlockSpec((None, KV_WIDTH, BLOCK), lambda b, s: (b, 0, prev_blk(s))),
            pl.BlockSpec((None, KV_WIDTH, BLOCK), lambda b, s: (b, 0, next_blk(s))),
            pl.BlockSpec((1, A_WIDTH), lambda b, s: (0, 0)),
        ],
        out_specs=pl.BlockSpec((None, tq, A_WIDTH), lambda b, s: (b, s, 0)),
        out_shape=jax.ShapeDtypeStruct((B, S, A_WIDTH), BF16),
        scratch_shapes=[
            pltpu.VMEM((tq + 2 * BLOCK, KV_WIDTH), BF16),
            pltpu.VMEM((KV_WIDTH, tq + 2 * BLOCK), BF16),
            pltpu.VMEM((A_WIDTH, tq), F32),
        ],
        compiler_params=_params("parallel", "parallel"),
        name="window_attn",
    )(sink, qt, k, k, k, vt, vt, vt, g)


def _out_proj_kernel(of_ref, oa_ref, x_ref, wof_ref, woa_ref, g_ref, x1_ref, h2_ref):
    y = jnp.dot(of_ref[...], wof_ref[...], preferred_element_type=F32)
    y = y + jnp.dot(oa_ref[...], woa_ref[...], preferred_element_type=F32)
    x1 = x_ref[...] + y
    x1_ref[...] = x1
    r = lax.rsqrt(jnp.mean(x1 * x1, axis=-1, keepdims=True) + EPS)
    h2_ref[...] = ((x1 * r) * g_ref[...]).astype(BF16)


def _out_proj(of_n, oa_n, x, wof, woa, g, tm):
    T = x.shape[0]
    return pl.pallas_call(
        _out_proj_kernel,
        grid=(T // tm,),
        in_specs=[
            pl.BlockSpec((tm, F_WIDTH), lambda i: (i, 0)),
            pl.BlockSpec((tm, A_WIDTH), lambda i: (i, 0)),
            pl.BlockSpec((tm, D_MODEL), lambda i: (i, 0)),
            _resident((F_WIDTH, D_MODEL)),
            _resident((A_WIDTH, D_MODEL)),
            _resident((1, D_MODEL)),
        ],
        out_specs=[
            pl.BlockSpec((tm, D_MODEL), lambda i: (i, 0)),
            pl.BlockSpec((tm, D_MODEL), lambda i: (i, 0)),
        ],
        out_shape=[
            jax.ShapeDtypeStruct((T, D_MODEL), F32),
            jax.ShapeDtypeStruct((T, D_MODEL), BF16),
        ],
        compiler_params=_params("parallel"),
        name="out_proj",
    )(of_n, oa_n, x, wof, woa, g)


def _mlp_kernel(h2_ref, x1_ref, wu_ref, wd_ref, g_ref, o_ref, acc_ref):
    j = pl.program_id(1)

    @pl.when(j == 0)
    def _():
        acc_ref[...] = x1_ref[...]

    u = jnp.maximum(jnp.dot(h2_ref[...], wu_ref[...], preferred_element_type=F32), 0.0)
    acc_ref[...] += jnp.dot((u * u).astype(BF16), wd_ref[...], preferred_element_type=F32)

    @pl.when(j == pl.num_programs(1) - 1)
    def _():
        y = acc_ref[...]
        r = lax.rsqrt(jnp.mean(y * y, axis=-1, keepdims=True) + EPS)
        o_ref[...] = (y * r) * g_ref[...]


def _mlp(h2, x1, wu, wd, g, tm, tf):
    T = h2.shape[0]
    return pl.pallas_call(
        _mlp_kernel,
        grid=(T // tm, D_FF // tf),
        in_specs=[
            pl.BlockSpec((tm, D_MODEL), lambda i, j: (i, 0)),
            pl.BlockSpec((tm, D_MODEL), lambda i, j: (i, 0)),
            pl.BlockSpec((D_MODEL, tf), lambda i, j: (0, j)),
            pl.BlockSpec((tf, D_MODEL), lambda i, j: (j, 0)),
            _resident((1, D_MODEL)),
        ],
        out_specs=pl.BlockSpec((tm, D_MODEL), lambda i, j: (i, 0)),
        out_shape=jax.ShapeDtypeStruct((T, D_MODEL), F32),
        scratch_shapes=[pltpu.VMEM((tm, D_MODEL), F32)],
        compiler_params=_params("parallel", "arbitrary"),
        name="mlp",
    )(h2, x1, wu, wd, g)


def _rope_tables(S):
    inv_freq = 1.0 / (ROPE_THETA ** (jnp.arange(0, HEAD_DIM, 2, dtype=F32) / HEAD_DIM))
    pos = jnp.arange(S, dtype=F32)
    ang = pos[:, None] * inv_freq[None, :]
    ang = jnp.concatenate([ang, ang], axis=-1)
    cos, sin = jnp.cos(ang), jnp.sin(ang)
    sign = jnp.where(jnp.arange(HEAD_DIM) < HALF_DIM, -1.0, 1.0).astype(F32)
    sin = sin * sign[None, :]
    reps = LANES // HEAD_DIM
    return cos.T, sin.T, jnp.tile(cos, (1, reps)), jnp.tile(sin, (1, reps))


def _seq_dft_matrix(S):
    hi_n = 64
    lo_n = S // hi_n
    k = jnp.arange(S, dtype=jnp.int32)
    w = 2.0 * math.pi / S
    a_hi = ((jnp.arange(hi_n, dtype=jnp.int32)[:, None] * lo_n * k[None, :]) % S).astype(F32) * w
    a_lo = ((jnp.arange(lo_n, dtype=jnp.int32)[:, None] * k[None, :]) % S).astype(F32) * w
    ch, sh = jnp.cos(a_hi)[:, None, :], jnp.sin(a_hi)[:, None, :]
    cl, sl = jnp.cos(a_lo)[None, :, :], jnp.sin(a_lo)[None, :, :]
    scale = S ** -0.5
    cos = ((ch * cl - sh * sl) * scale).reshape(S, S)
    nsin = ((sh * cl + ch * sl) * -scale).reshape(S, S)
    return jnp.concatenate([cos, nsin], axis=1).astype(BF16)


def _trunk(x, w, tm_in, tq, tm_dft, n_split, tm_out, tm_mlp, tf):
    B, S, _ = x.shape
    cost, sint, cos, sin = _rope_tables(S)
    ab, qt, k, vt = _in_proj(x, w["g_mix"], w["wf"], w["wqt"], w["wk"], w["wvt"], w["mg"],
                             cost, sint, cos, sin, tm_in)
    of_n = _seq_dft(_seq_dft_matrix(S), ab.reshape(B, 2 * S, F_WIDTH), w["g_of"], tm_dft, n_split)
    oa_n = _window_attn(w["sink"], qt, k, vt, w["g_oa"], tq)
    T = B * S
    x2 = x.reshape(T, D_MODEL)
    x1, h2 = _out_proj(of_n.reshape(T, F_WIDTH), oa_n.reshape(T, A_WIDTH), x2,
                       w["wof"], w["woa"], w["g_mlp"], tm_out)
    y = _mlp(h2, x1, w["wu"], w["wd"], w["g_final"], tm_mlp, tf)
    return y.reshape(B, S, D_MODEL)


def _prepare_weights(ln_mix_g, w_in, w_fourier, attn_sink, out_norm_fourier_g, out_norm_attn_g,
                     w_out, ln_mlp_g, w_up, w_down, ln_final_g):
    o1, o2, o3 = F_WIDTH, F_WIDTH + A_WIDTH, F_WIDTH + A_WIDTH + KV_WIDTH
    w_in_b = w_in.astype(BF16)
    w_out_b = w_out.astype(BF16)
    return {
        "g_mix": ln_mix_g.reshape(1, D_MODEL),
        "wf": w_in_b[:, :o1],
        "wqt": w_in_b[:, o1:o2].T,
        "wk": w_in_b[:, o2:o3],
        "wvt": w_in_b[:, o3:].T,
        "mg": _fourier_weights(w_fourier),
        "sink": attn_sink.astype(F32),
        "g_of": out_norm_fourier_g.reshape(1, F_WIDTH),
        "g_oa": out_norm_attn_g.reshape(1, A_WIDTH),
        "wof": w_out_b[:F_WIDTH],
        "woa": w_out_b[F_WIDTH:],
        "g_mlp": ln_mlp_g.reshape(1, D_MODEL),
        "wu": w_up.astype(BF16),
        "wd": w_down.astype(BF16),
        "g_final": ln_final_g.reshape(1, D_MODEL),
    }


def kernel(x_prompt, x_sample, ln_mix_g, w_in, w_fourier, attn_sink, out_norm_fourier_g,
           out_norm_attn_g, w_out, ln_mlp_g, w_up, w_down, ln_final_g):
    assert ln_mix_g.shape[0] == 1, "single-layer block"
    w = _prepare_weights(ln_mix_g[0], w_in[0], w_fourier[0], attn_sink[0], out_norm_fourier_g[0],
                         out_norm_attn_g[0], w_out[0], ln_mlp_g[0], w_up[0], w_down[0], ln_final_g)
    tiles = dict(tm_in=512, tq=512, tm_dft=512, n_split=2, tm_out=512, tm_mlp=512, tf=1024)
    return (_trunk(x_prompt, w, **tiles), _trunk(x_sample, w, **tiles))
```

```python
import functools
import math

import jax
import jax.numpy as jnp
from jax import lax
from jax.experimental import pallas as pl
from jax.experimental.pallas import tpu as pltpu

D_MODEL = 2048
F_WIDTH = 1024
F_GROUPS = 8
F_CH = 128
A_WIDTH = 1024
HEAD_DIM = 64
HALF_DIM = HEAD_DIM // 2
N_Q_HEADS = 16
N_KV_HEADS = 4
GQA_GROUP = 4
KV_WIDTH = 256
WINDOW = 128
BLOCK = 128
ROPE_THETA = 10000.0
D_FF = 4 * D_MODEL
EPS = 1e-6
NEG_INF = -1e30

LANES = 128
VMEM_LIMIT = 56 * 1024 * 1024

BF16 = jnp.bfloat16
F32 = jnp.float32

_NT = (((1,), (1,)), ((), ()))


def _params(*sem):
    return pltpu.CompilerParams(dimension_semantics=sem, vmem_limit_bytes=VMEM_LIMIT)


def _resident(shape):
    nd = len(shape)
    return pl.BlockSpec(shape, lambda *_: (0,) * nd, pipeline_mode=pl.Buffered(1))


def _fourier_weights_kernel(cs_ref, w_ref, o_ref):
    w = w_ref[...]
    c = jnp.dot(cs_ref[0], w, preferred_element_type=F32, precision=lax.Precision.HIGHEST)
    s = jnp.dot(cs_ref[1], w, preferred_element_type=F32, precision=lax.Precision.HIGHEST)
    o_ref[:, :F_CH] = c.astype(BF16)
    o_ref[:, F_CH:] = s.astype(BF16)


def _fourier_weights(w_fourier):
    n = jnp.arange(F_CH, dtype=jnp.int32)
    ang = ((n[:, None] * n[None, :]) % F_CH).astype(F32) * (2.0 * math.pi / F_CH)
    cs = jnp.stack([jnp.cos(ang), jnp.sin(ang)]) * (F_CH ** -0.5)
    return pl.pallas_call(
        _fourier_weights_kernel,
        grid=(F_GROUPS,),
        in_specs=[pl.BlockSpec((2, F_CH, F_CH), lambda g: (0, 0, 0)),
                  pl.BlockSpec((None, F_CH, F_CH), lambda g: (g, 0, 0))],
        out_specs=pl.BlockSpec((None, F_CH, 2 * F_CH), lambda g: (g, 0, 0)),
        out_shape=jax.ShapeDtypeStruct((F_GROUPS, F_CH, 2 * F_CH), BF16),
        name="fourier_weights",
    )(cs, w_fourier)


def _in_proj_kernel(x_ref, g_ref, wf_ref, wqt_ref, wk_ref, wvt_ref, mg_ref,
                    cost_ref, sint_ref, cos_ref, sin_ref,
                    ab_ref, qt_ref, k_ref, vt_ref):
    x = x_ref[...]
    r = lax.rsqrt(jnp.mean(x * x, axis=-1, keepdims=True) + EPS)
    h = ((x * r) * g_ref[...]).astype(BF16)

    zf = jnp.dot(h, wf_ref[...], preferred_element_type=F32).astype(BF16)
    for g in range(F_GROUPS):
        lo = g * F_CH
        ab = jnp.dot(zf[:, lo:lo + F_CH], mg_ref[g], preferred_element_type=F32)
        ab_ref[0, :, lo:lo + F_CH] = ab[:, :F_CH].astype(BF16)
        ab_ref[1, :, lo:lo + F_CH] = ab[:, F_CH:].astype(BF16)

    qt = lax.dot_general(wqt_ref[...], h, _NT, preferred_element_type=F32)
    cost = cost_ref[...]
    sint = sint_ref[...]
    scale = HEAD_DIM ** -0.5
    for hd in range(N_Q_HEADS):
        lo = hd * HEAD_DIM
        blk = qt[lo:lo + HEAD_DIM]
        rot = jnp.concatenate([blk[HALF_DIM:], blk[:HALF_DIM]], axis=0)
        qt_ref[lo:lo + HEAD_DIM, :] = ((blk * cost + rot * sint) * scale).astype(BF16)

    kk = jnp.dot(h, wk_ref[...], preferred_element_type=F32)
    cos = cos_ref[...]
    sin = sin_ref[...]
    first_half = (lax.broadcasted_iota(jnp.int32, cos.shape, 1) % HEAD_DIM) < HALF_DIM
    for c in range(KV_WIDTH // LANES):
        blk = kk[:, c * LANES:(c + 1) * LANES]
        rot = jnp.where(first_half,
                        pltpu.roll(blk, LANES - HALF_DIM, axis=1),
                        pltpu.roll(blk, HALF_DIM, axis=1))
        k_ref[:, c * LANES:(c + 1) * LANES] = (blk * cos + rot * sin).astype(BF16)

    vt = lax.dot_general(wvt_ref[...], h, _NT, preferred_element_type=F32)
    vt_ref[...] = vt.astype(BF16)


def _in_proj(x, g, wf, wqt, wk, wvt, mg, cost, sint, cos, sin, tm):
    B, S, _ = x.shape
    return pl.pallas_call(
        _in_proj_kernel,
        grid=(B, S // tm),
        in_specs=[
            pl.BlockSpec((None, tm, D_MODEL), lambda b, s: (b, s, 0)),
            _resident((1, D_MODEL)),
            _resident((D_MODEL, F_WIDTH)),
            _resident((A_WIDTH, D_MODEL)),
            _resident((D_MODEL, KV_WIDTH)),
            _resident((KV_WIDTH, D_MODEL)),
            _resident((F_GROUPS, F_CH, 2 * F_CH)),
            pl.BlockSpec((HEAD_DIM, tm), lambda b, s: (0, s)),
            pl.BlockSpec((HEAD_DIM, tm), lambda b, s: (0, s)),
            pl.BlockSpec((tm, LANES), lambda b, s: (s, 0)),
            pl.BlockSpec((tm, LANES), lambda b, s: (s, 0)),
        ],
        out_specs=[
            pl.BlockSpec((None, 2, tm, F_WIDTH), lambda b, s: (b, 0, s, 0)),
            pl.BlockSpec((None, A_WIDTH, tm), lambda b, s: (b, 0, s)),
            pl.BlockSpec((None, tm, KV_WIDTH), lambda b, s: (b, s, 0)),
            pl.BlockSpec((None, KV_WIDTH, tm), lambda b, s: (b, 0, s)),
        ],
        out_shape=[
            jax.ShapeDtypeStruct((B, 2, S, F_WIDTH), BF16),
            jax.ShapeDtypeStruct((B, A_WIDTH, S), BF16),
            jax.ShapeDtypeStruct((B, S, KV_WIDTH), BF16),
            jax.ShapeDtypeStruct((B, KV_WIDTH, S), BF16),
        ],
        compiler_params=_params("parallel", "parallel"),
        name="in_proj",
    )(x, g, wf, wqt, wk, wvt, mg, cost, sint, cos, sin)


REV = 256
ROW_PACK = 16
DFT_ROWS = 512


def _first_row(x):
    return jnp.where(lax.broadcasted_iota(jnp.int32, x.shape, 0) == 0, x, 0.0)


def _patch_first_row(tile, row):
    head = tile[:ROW_PACK] + _first_row(row)
    return jnp.concatenate([head, tile[ROW_PACK:]], axis=0)


def _seq_dft_kernel(ab_ref, c_ref, s_ref, j_ref, o_ref, ap_ref, bm_ref, u_ref, *, seq):
    half = seq // 2
    n_rev = half // REV
    jmat = j_ref[...]
    a_ref = ab_ref.at[0]
    b_ref = ab_ref.at[1]

    for t in range(n_rev):
        lo = t * REV
        src = seq - lo - REV
        ra = jnp.dot(jmat, a_ref[src:src + REV, :], preferred_element_type=F32)
        rb = jnp.dot(jmat, b_ref[src:src + REV, :], preferred_element_type=F32)
        if t > 0:
            ra = _patch_first_row(ra, a_ref[seq - lo:seq - lo + ROW_PACK, :].astype(F32))
            rb = _patch_first_row(rb, b_ref[seq - lo:seq - lo + ROW_PACK, :].astype(F32))
        ap_ref[lo:lo + REV, :] = (a_ref[lo:lo + REV, :].astype(F32) + ra).astype(BF16)
        bm_ref[lo:lo + REV, :] = (b_ref[lo:lo + REV, :].astype(F32) - rb).astype(BF16)

    a_half = a_ref[half:half + ROW_PACK, :].astype(F32)[0:1, :] * (seq ** -0.5)
    row_par = lax.broadcasted_iota(jnp.int32, (DFT_ROWS, 1), 0) % 2
    sgn_a_half = jnp.where(row_par == 0, a_half, -a_half)

    ap = ap_ref[...]
    bm = bm_ref[...]
    n_row = half // DFT_ROWS
    nyq = None
    for i in range(n_row):
        lo = i * DFT_ROWS
        extra = ROW_PACK if i == n_row - 1 else 0
        p = jnp.dot(c_ref[lo:lo + DFT_ROWS + extra, :], ap, preferred_element_type=F32)
        if extra:
            nyq = p[DFT_ROWS:] + a_half
            p = p[:DFT_ROWS]
        p = p + sgn_a_half
        q = jnp.dot(s_ref[lo:lo + DFT_ROWS, :], bm, preferred_element_type=F32)
        o_ref[lo:lo + DFT_ROWS, :] = (p - q).astype(BF16)
        u_ref[lo:lo + DFT_ROWS, :] = (p + q).astype(BF16)

    for t in range(n_rev):
        src = half - (t + 1) * REV
        r = jnp.dot(jmat, u_ref[src:src + REV, :], preferred_element_type=F32)
        if t == 0:
            r = _patch_first_row(r, nyq)
        else:
            r = _patch_first_row(r, u_ref[src + REV:src + REV + ROW_PACK, :].astype(F32))
        o_ref[half + t * REV:half + (t + 1) * REV, :] = r.astype(BF16)


def _seq_dft(ab, cmat, smat, jmat, n_col):
    B, _, S, _ = ab.shape
    half = S // 2
    assert half % DFT_ROWS == 0 and half % 2 == 0
    tn = F_WIDTH // n_col
    return pl.pallas_call(
        functools.partial(_seq_dft_kernel, seq=S),
        grid=(B, n_col),
        in_specs=[
            pl.BlockSpec((None, 2, S, tn), lambda b, c: (b, 0, 0, c)),
            _resident((half + ROW_PACK, half)),
            _resident((half, half)),
            _resident((REV, REV)),
        ],
        out_specs=pl.BlockSpec((None, S, tn), lambda b, c: (b, 0, c)),
        out_shape=jax.ShapeDtypeStruct((B, S, F_WIDTH), BF16),
        scratch_shapes=[pltpu.VMEM((half, tn), BF16),
                        pltpu.VMEM((half, tn), BF16),
                        pltpu.VMEM((half, tn), BF16)],
        compiler_params=_params("parallel", "parallel"),
        name="seq_dft",
    )(ab, cmat, smat, jmat)


def _window_attn_kernel(sink_ref, qt_ref, k_ref, kp_ref, kn_ref, vt_ref, vtp_ref, vtn_ref, g_ref,
                        o_ref, kfull, vtfull, ot_all, *, tq, n_blocks):
    s_idx = pl.program_id(1)
    r_blocks = tq // BLOCK
    span = 3 * BLOCK

    kfull[0:BLOCK, :] = kp_ref[...]
    kfull[BLOCK:BLOCK + tq, :] = k_ref[...]
    kfull[BLOCK + tq:, :] = kn_ref[...]
    vtfull[:, 0:BLOCK] = vtp_ref[...]
    vtfull[:, BLOCK:BLOCK + tq] = vt_ref[...]
    vtfull[:, BLOCK + tq:] = vtn_ref[...]

    key_i = lax.broadcasted_iota(jnp.int32, (span, BLOCK), 0)
    qry_i = lax.broadcasted_iota(jnp.int32, (span, BLOCK), 1)
    diff = key_i - qry_i
    band = (diff >= BLOCK - WINDOW) & (diff <= BLOCK + WINDOW)

    for n in range(r_blocks):
        nglob = s_idx * r_blocks + n
        lo = jnp.where(nglob == 0, BLOCK, 0)
        hi = jnp.where(nglob == n_blocks - 1, 2 * BLOCK, span)
        valid = band & (key_i >= lo) & (key_i < hi)
        kwin = kfull[n * BLOCK:n * BLOCK + span, :]
        for h in range(N_KV_HEADS):
            r0 = h * HEAD_DIM
            q4 = jnp.concatenate(
                [qt_ref[(h * GQA_GROUP + g) * HEAD_DIM:(h * GQA_GROUP + g + 1) * HEAD_DIM,
                        n * BLOCK:(n + 1) * BLOCK] for g in range(GQA_GROUP)], axis=1)
            parts = [q4]
            if r0 > 0:
                parts.insert(0, jnp.zeros((r0, GQA_GROUP * BLOCK), BF16))
            if r0 + HEAD_DIM < KV_WIDTH:
                parts.append(jnp.zeros((KV_WIDTH - r0 - HEAD_DIM, GQA_GROUP * BLOCK), BF16))
            qpad = jnp.concatenate(parts, axis=0)
            st_all = jnp.dot(kwin, qpad, preferred_element_type=F32)
            pn = []
            for g in range(GQA_GROUP):
                st = jnp.where(valid, st_all[:, g * BLOCK:(g + 1) * BLOCK], NEG_INF)
                sink = jnp.full((1, BLOCK), sink_ref[h * GQA_GROUP + g], F32)
                m = jnp.maximum(jnp.max(st, axis=0, keepdims=True), sink)
                p = jnp.exp(st - m)
                den = jnp.sum(p, axis=0, keepdims=True) + jnp.exp(sink - m)
                pn.append((p * (1.0 / den)).astype(BF16))
            vwin = vtfull[r0:r0 + HEAD_DIM, n * BLOCK:n * BLOCK + span]
            ot = jnp.dot(vwin, jnp.concatenate(pn, axis=1), preferred_element_type=F32)
            for g in range(GQA_GROUP):
                q0 = (h * GQA_GROUP + g) * HEAD_DIM
                ot_all[q0:q0 + HEAD_DIM, n * BLOCK:(n + 1) * BLOCK] = ot[:, g * BLOCK:(g + 1) * BLOCK]

    y = ot_all[...]
    r = lax.rsqrt(jnp.mean(y * y, axis=0, keepdims=True) + EPS)
    o_ref[...] = ((y * r).T * g_ref[...]).astype(BF16)


def _window_attn(sink, qt, k, vt, g, tq):
    B, _, S = qt.shape
    n_blocks = S // BLOCK
    r_blocks = tq // BLOCK
    prev_blk = lambda s: jnp.maximum(s * r_blocks - 1, 0)
    next_blk = lambda s: jnp.minimum((s + 1) * r_blocks, n_blocks - 1)
    return pl.pallas_call(
        functools.partial(_window_attn_kernel, tq=tq, n_blocks=n_blocks),
        grid=(B, S // tq),
        in_specs=[
            pl.BlockSpec(memory_space=pltpu.SMEM),
            pl.BlockSpec((None, A_WIDTH, tq), lambda b, s: (b, 0, s)),
            pl.BlockSpec((None, tq, KV_WIDTH), lambda b, s: (b, s, 0)),
            pl.BlockSpec((None, BLOCK, KV_WIDTH), lambda b, s: (b, prev_blk(s), 0)),
            pl.BlockSpec((None, BLOCK, KV_WIDTH), lambda b, s: (b, next_blk(s), 0)),
            pl.BlockSpec((None, KV_WIDTH, tq), lambda b, s: (b, 0, s)),
            pl.BlockSpec((None, KV_WIDTH, BLOCK), lambda b, s: (b, 0, prev_blk(s))),
            pl.BlockSpec((None, KV_WIDTH, BLOCK), lambda b, s: (b, 0, next_blk(s))),
            pl.BlockSpec((1, A_WIDTH), lambda b, s: (0, 0)),
        ],
        out_specs=pl.BlockSpec((None, tq, A_WIDTH), lambda b, s: (b, s, 0)),
        out_shape=jax.ShapeDtypeStruct((B, S, A_WIDTH), BF16),
        scratch_shapes=[
            pltpu.VMEM((tq + 2 * BLOCK, KV_WIDTH), BF16),
            pltpu.VMEM((KV_WIDTH, tq + 2 * BLOCK), BF16),
            pltpu.VMEM((A_WIDTH, tq), F32),
        ],
        compiler_params=_params("parallel", "parallel"),
        name="window_attn",
    )(sink, qt, k, k, k, vt, vt, vt, g)


def _out_proj_kernel(of_ref, oa_ref, x_ref, wof_ref, woa_ref, gof_ref, g_ref, x1_ref, h2_ref):
    of = of_ref[...].astype(F32)
    rf = lax.rsqrt(jnp.mean(of * of, axis=-1, keepdims=True) + EPS)
    of_n = ((of * rf) * gof_ref[...]).astype(BF16)
    y = jnp.dot(of_n, wof_ref[...], preferred_element_type=F32)
    y = y + jnp.dot(oa_ref[...], woa_ref[...], preferred_element_type=F32)
    x1 = x_ref[...] + y
    x1_ref[...] = x1
    r = lax.rsqrt(jnp.mean(x1 * x1, axis=-1, keepdims=True) + EPS)
    h2_ref[...] = ((x1 * r) * g_ref[...]).astype(BF16)


def _out_proj(of, oa_n, x, wof, woa, gof, g, tm):
    T = x.shape[0]
    return pl.pallas_call(
        _out_proj_kernel,
        grid=(T // tm,),
        in_specs=[
            pl.BlockSpec((tm, F_WIDTH), lambda i: (i, 0)),
            pl.BlockSpec((tm, A_WIDTH), lambda i: (i, 0)),
            pl.BlockSpec((tm, D_MODEL), lambda i: (i, 0)),
            _resident((F_WIDTH, D_MODEL)),
            _resident((A_WIDTH, D_MODEL)),
            _resident((1, F_WIDTH)),
            _resident((1, D_MODEL)),
        ],
        out_specs=[
            pl.BlockSpec((tm, D_MODEL), lambda i: (i, 0)),
            pl.BlockSpec((tm, D_MODEL), lambda i: (i, 0)),
        ],
        out_shape=[
            jax.ShapeDtypeStruct((T, D_MODEL), F32),
            jax.ShapeDtypeStruct((T, D_MODEL), BF16),
        ],
        compiler_params=_params("parallel"),
        name="out_proj",
    )(of, oa_n, x, wof, woa, gof, g)


def _mlp_kernel(h2_ref, x1_ref, wu_ref, wd_ref, g_ref, o_ref, acc_ref):
    j = pl.program_id(1)

    @pl.when(j == 0)
    def _():
        acc_ref[...] = x1_ref[...]

    u = jnp.maximum(jnp.dot(h2_ref[...], wu_ref[...], preferred_element_type=F32), 0.0)
    acc_ref[...] += jnp.dot((u * u).astype(BF16), wd_ref[...], preferred_element_type=F32)

    @pl.when(j == pl.num_programs(1) - 1)
    def _():
        y = acc_ref[...]
        r = lax.rsqrt(jnp.mean(y * y, axis=-1, keepdims=True) + EPS)
        o_ref[...] = (y * r) * g_ref[...]


def _mlp(h2, x1, wu, wd, g, tm, tf):
    T = h2.shape[0]
    return pl.pallas_call(
        _mlp_kernel,
        grid=(T // tm, D_FF // tf),
        in_specs=[
            pl.BlockSpec((tm, D_MODEL), lambda i, j: (i, 0)),
            pl.BlockSpec((tm, D_MODEL), lambda i, j: (i, 0)),
            pl.BlockSpec((D_MODEL, tf), lambda i, j: (0, j)),
            pl.BlockSpec((tf, D_MODEL), lambda i, j: (j, 0)),
            _resident((1, D_MODEL)),
        ],
        out_specs=pl.BlockSpec((tm, D_MODEL), lambda i, j: (i, 0)),
        out_shape=jax.ShapeDtypeStruct((T, D_MODEL), F32),
        scratch_shapes=[pltpu.VMEM((tm, D_MODEL), F32)],
        compiler_params=_params("parallel", "arbitrary"),
        name="mlp",
    )(h2, x1, wu, wd, g)


def _rope_tables(S):
    inv_freq = 1.0 / (ROPE_THETA ** (jnp.arange(0, HEAD_DIM, 2, dtype=F32) / HEAD_DIM))
    pos = jnp.arange(S, dtype=F32)
    ang = pos[:, None] * inv_freq[None, :]
    ang = jnp.concatenate([ang, ang], axis=-1)
    cos, sin = jnp.cos(ang), jnp.sin(ang)
    sign = jnp.where(jnp.arange(HEAD_DIM) < HALF_DIM, -1.0, 1.0).astype(F32)
    sin = sin * sign[None, :]
    reps = LANES // HEAD_DIM
    return cos.T, sin.T, jnp.tile(cos, (1, reps)), jnp.tile(sin, (1, reps))


def _seq_dft_matrices(S):
    half = S // 2
    hi_n = 64
    lo_n = half // hi_n
    k = jnp.arange(half, dtype=jnp.int32)
    w = 2.0 * math.pi / S
    a_hi = ((jnp.arange(hi_n, dtype=jnp.int32)[:, None] * lo_n * k[None, :]) % S).astype(F32) * w
    a_lo = ((jnp.arange(lo_n, dtype=jnp.int32)[:, None] * k[None, :]) % S).astype(F32) * w
    ch, sh = jnp.cos(a_hi)[:, None, :], jnp.sin(a_hi)[:, None, :]
    cl, sl = jnp.cos(a_lo)[None, :, :], jnp.sin(a_lo)[None, :, :]
    scale = S ** -0.5
    cos = ((ch * cl - sh * sl) * scale).reshape(half, half)
    sin = ((sh * cl + ch * sl) * scale).reshape(half, half)
    alt = jnp.where(k % 2 == 0, scale, -scale).astype(F32)[None, :]
    cos = jnp.concatenate([cos, alt, jnp.zeros((ROW_PACK - 1, half), F32)], axis=0)
    i = jnp.arange(REV, dtype=jnp.int32)
    jmat = ((i[:, None] + i[None, :]) == REV).astype(BF16)
    return cos.astype(BF16), sin.astype(BF16), jmat


def _trunk(x, w, tm_in, tq, dft_elems, tm_out, tm_mlp, tf):
    B, S, _ = x.shape
    cost, sint, cos, sin = _rope_tables(S)
    ab, qt, k, vt = _in_proj(x, w["g_mix"], w["wf"], w["wqt"], w["wk"], w["wvt"], w["mg"],
                             cost, sint, cos, sin, tm_in)
    of = _seq_dft(ab, *_seq_dft_matrices(S), n_col=F_WIDTH * S // dft_elems)
    oa_n = _window_attn(w["sink"], qt, k, vt, w["g_oa"], tq)
    T = B * S
    x2 = x.reshape(T, D_MODEL)
    x1, h2 = _out_proj(of.reshape(T, F_WIDTH), oa_n.reshape(T, A_WIDTH), x2,
                       w["wof"], w["woa"], w["g_of"], w["g_mlp"], tm_out)
    y = _mlp(h2, x1, w["wu"], w["wd"], w["g_final"], tm_mlp, tf)
    return y.reshape(B, S, D_MODEL)


def _prepare_weights(ln_mix_g, w_in, w_fourier, attn_sink, out_norm_fourier_g, out_norm_attn_g,
                     w_out, ln_mlp_g, w_up, w_down, ln_final_g):
    o1, o2, o3 = F_WIDTH, F_WIDTH + A_WIDTH, F_WIDTH + A_WIDTH + KV_WIDTH
    w_in_b = w_in.astype(BF16)
    w_out_b = w_out.astype(BF16)
    return {
        "g_mix": ln_mix_g.reshape(1, D_MODEL),
        "wf": w_in_b[:, :o1],
        "wqt": w_in_b[:, o1:o2].T,
        "wk": w_in_b[:, o2:o3],
        "wvt": w_in_b[:, o3:].T,
        "mg": _fourier_weights(w_fourier),
        "sink": attn_sink.astype(F32),
        "g_of": out_norm_fourier_g.reshape(1, F_WIDTH),
        "g_oa": out_norm_attn_g.reshape(1, A_WIDTH),
        "wof": w_out_b[:F_WIDTH],
        "woa": w_out_b[F_WIDTH:],
        "g_mlp": ln_mlp_g.reshape(1, D_MODEL),
        "wu": w_up.astype(BF16),
        "wd": w_down.astype(BF16),
        "g_final": ln_final_g.reshape(1, D_MODEL),
    }


def kernel(x_prompt, x_sample, ln_mix_g, w_in, w_fourier, attn_sink, out_norm_fourier_g,
           out_norm_attn_g, w_out, ln_mlp_g, w_up, w_down, ln_final_g):
    assert ln_mix_g.shape[0] == 1, "single-layer block"
    w = _prepare_weights(ln_mix_g[0], w_in[0], w_fourier[0], attn_sink[0], out_norm_fourier_g[0],
                         out_norm_attn_g[0], w_out[0], ln_mlp_g[0], w_up[0], w_down[0], ln_final_g)
    tiles = dict(tm_in=512, tq=512, dft_elems=4096 * 256, tm_out=512, tm_mlp=512, tf=1024)
    return (_trunk(x_prompt, w, **tiles), _trunk(x_sample, w, **tiles))
```

```python
import functools
import math

import jax
import jax.numpy as jnp
from jax import lax
from jax.experimental import pallas as pl
from jax.experimental.pallas import tpu as pltpu

D_MODEL = 2048
F_WIDTH = 1024
F_GROUPS = 8
F_CH = 128
A_WIDTH = 1024
HEAD_DIM = 64
HALF_DIM = HEAD_DIM // 2
N_Q_HEADS = 16
N_KV_HEADS = 4
GQA_GROUP = 4
KV_WIDTH = 256
WINDOW = 128
BLOCK = 128
ROPE_THETA = 10000.0
D_FF = 4 * D_MODEL
EPS = 1e-6
NEG_INF = -1e30
LOG2_E = math.log2(math.e)

LANES = 128
VMEM_LIMIT = 56 * 1024 * 1024

BF16 = jnp.bfloat16
F32 = jnp.float32

_NT = (((1,), (1,)), ((), ()))


def _params(*sem, flags=None):
    return pltpu.CompilerParams(dimension_semantics=sem, vmem_limit_bytes=VMEM_LIMIT, flags=flags)


def _resident(shape):
    nd = len(shape)
    return pl.BlockSpec(shape, lambda *_: (0,) * nd, pipeline_mode=pl.Buffered(1))


def _fourier_weights_kernel(cs_ref, w_ref, o_ref):
    w = w_ref[...]
    c = jnp.dot(cs_ref[0], w, preferred_element_type=F32, precision=lax.Precision.HIGHEST)
    s = jnp.dot(cs_ref[1], w, preferred_element_type=F32, precision=lax.Precision.HIGHEST)
    o_ref[:, :F_CH] = c.astype(BF16)
    o_ref[:, F_CH:] = s.astype(BF16)


def _fourier_weights(w_fourier):
    n = jnp.arange(F_CH, dtype=jnp.int32)
    ang = ((n[:, None] * n[None, :]) % F_CH).astype(F32) * (2.0 * math.pi / F_CH)
    cs = jnp.stack([jnp.cos(ang), jnp.sin(ang)]) * (F_CH ** -0.5)
    return pl.pallas_call(
        _fourier_weights_kernel,
        grid=(F_GROUPS,),
        in_specs=[pl.BlockSpec((2, F_CH, F_CH), lambda g: (0, 0, 0)),
                  pl.BlockSpec((None, F_CH, F_CH), lambda g: (g, 0, 0))],
        out_specs=pl.BlockSpec((None, F_CH, 2 * F_CH), lambda g: (g, 0, 0)),
        out_shape=jax.ShapeDtypeStruct((F_GROUPS, F_CH, 2 * F_CH), BF16),
        name="fourier_weights",
    )(cs, w_fourier)


def _in_proj_kernel(x_ref, g_ref, wf_ref, wqt_ref, wk_ref, wvt_ref, mg_ref,
                    cost_ref, sint_ref, cos_ref, sin_ref,
                    ab_ref, qt_ref, k_ref, vt_ref):
    x = x_ref[...]
    r = lax.rsqrt(jnp.mean(x * x, axis=-1, keepdims=True) + EPS)
    h = ((x * r) * g_ref[...]).astype(BF16)

    zf = jnp.dot(h, wf_ref[...], preferred_element_type=F32).astype(BF16)
    qt = lax.dot_general(wqt_ref[...], h, _NT, preferred_element_type=F32)
    kk = jnp.dot(h, wk_ref[...], preferred_element_type=F32)
    vt = lax.dot_general(wvt_ref[...], h, _NT, preferred_element_type=F32)

    for g in range(F_GROUPS):
        lo = g * F_CH
        ab = jnp.dot(zf[:, lo:lo + F_CH], mg_ref[g], preferred_element_type=F32)
        ab_ref[0, :, lo:lo + F_CH] = ab[:, :F_CH].astype(BF16)
        ab_ref[1, :, lo:lo + F_CH] = ab[:, F_CH:].astype(BF16)

    cost = cost_ref[...]
    sint = sint_ref[...]
    scale = HEAD_DIM ** -0.5 * LOG2_E
    for hd in range(N_Q_HEADS):
        lo = hd * HEAD_DIM
        blk = qt[lo:lo + HEAD_DIM]
        rot = jnp.concatenate([blk[HALF_DIM:], blk[:HALF_DIM]], axis=0)
        qt_ref[lo:lo + HEAD_DIM, :] = ((blk * cost + rot * sint) * scale).astype(BF16)

    cos = cos_ref[...]
    sin = sin_ref[...]
    first_half = (lax.broadcasted_iota(jnp.int32, cos.shape, 1) % HEAD_DIM) < HALF_DIM
    for c in range(KV_WIDTH // LANES):
        blk = kk[:, c * LANES:(c + 1) * LANES]
        rot = jnp.where(first_half,
                        pltpu.roll(blk, LANES - HALF_DIM, axis=1),
                        pltpu.roll(blk, HALF_DIM, axis=1))
        k_ref[:, c * LANES:(c + 1) * LANES] = (blk * cos + rot * sin).astype(BF16)

    vt_ref[...] = vt.astype(BF16)


def _in_proj(x, g, wf, wqt, wk, wvt, mg, cost, sint, cos, sin, tm):
    B, S, _ = x.shape
    return pl.pallas_call(
        _in_proj_kernel,
        grid=(B, S // tm),
        in_specs=[
            pl.BlockSpec((None, tm, D_MODEL), lambda b, s: (b, s, 0)),
            _resident((1, D_MODEL)),
            _resident((D_MODEL, F_WIDTH)),
            _resident((A_WIDTH, D_MODEL)),
            _resident((D_MODEL, KV_WIDTH)),
            _resident((KV_WIDTH, D_MODEL)),
            _resident((F_GROUPS, F_CH, 2 * F_CH)),
            pl.BlockSpec((HEAD_DIM, tm), lambda b, s: (0, s)),
            pl.BlockSpec((HEAD_DIM, tm), lambda b, s: (0, s)),
            pl.BlockSpec((tm, LANES), lambda b, s: (s, 0)),
            pl.BlockSpec((tm, LANES), lambda b, s: (s, 0)),
        ],
        out_specs=[
            pl.BlockSpec((None, 2, tm, F_WIDTH), lambda b, s: (b, 0, s, 0)),
            pl.BlockSpec((None, A_WIDTH, tm), lambda b, s: (b, 0, s)),
            pl.BlockSpec((None, tm, KV_WIDTH), lambda b, s: (b, s, 0)),
            pl.BlockSpec((None, KV_WIDTH, tm), lambda b, s: (b, 0, s)),
        ],
        out_shape=[
            jax.ShapeDtypeStruct((B, 2, S, F_WIDTH), BF16),
            jax.ShapeDtypeStruct((B, A_WIDTH, S), BF16),
            jax.ShapeDtypeStruct((B, S, KV_WIDTH), BF16),
            jax.ShapeDtypeStruct((B, KV_WIDTH, S), BF16),
        ],
        compiler_params=_params("parallel", "parallel"),
        name="in_proj",
    )(x, g, wf, wqt, wk, wvt, mg, cost, sint, cos, sin)


REV = 256
ROW_PACK = 16
DFT_ROWS = 512


def _first_row(x):
    return jnp.where(lax.broadcasted_iota(jnp.int32, x.shape, 0) == 0, x, 0.0)


def _patch_first_row(tile, row):
    head = tile[:ROW_PACK] + _first_row(row)
    return jnp.concatenate([head, tile[ROW_PACK:]], axis=0)


def _seq_dft_kernel(ab_ref, c_ref, s_ref, j_ref, o_ref, ap_ref, bm_ref, u_ref, *, seq):
    half = seq // 2
    n_rev = half // REV
    jmat = j_ref[...]
    a_ref = ab_ref.at[0]
    b_ref = ab_ref.at[1]

    for t in range(n_rev):
        lo = t * REV
        src = seq - lo - REV
        ra = jnp.dot(jmat, a_ref[src:src + REV, :], preferred_element_type=F32)
        rb = jnp.dot(jmat, b_ref[src:src + REV, :], preferred_element_type=F32)
        if t > 0:
            ra = _patch_first_row(ra, a_ref[seq - lo:seq - lo + ROW_PACK, :].astype(F32))
            rb = _patch_first_row(rb, b_ref[seq - lo:seq - lo + ROW_PACK, :].astype(F32))
        ap_ref[lo:lo + REV, :] = (a_ref[lo:lo + REV, :].astype(F32) + ra).astype(BF16)
        bm_ref[lo:lo + REV, :] = (b_ref[lo:lo + REV, :].astype(F32) - rb).astype(BF16)

    a_half = a_ref[half:half + ROW_PACK, :].astype(F32)[0:1, :] * (seq ** -0.5)
    row_par = lax.broadcasted_iota(jnp.int32, (DFT_ROWS, 1), 0) % 2
    sgn_a_half = jnp.where(row_par == 0, a_half, -a_half)

    ap = ap_ref[...]
    bm = bm_ref[...]
    n_row = half // DFT_ROWS
    nyq = None
    for i in range(n_row):
        lo = i * DFT_ROWS
        extra = ROW_PACK if i == n_row - 1 else 0
        p = jnp.dot(c_ref[lo:lo + DFT_ROWS + extra, :], ap, preferred_element_type=F32)
        if extra:
            nyq = p[DFT_ROWS:] + a_half
            p = p[:DFT_ROWS]
        p = p + sgn_a_half
        q = jnp.dot(s_ref[lo:lo + DFT_ROWS, :], bm, preferred_element_type=F32)
        o_ref[lo:lo + DFT_ROWS, :] = (p - q).astype(BF16)
        u_ref[lo:lo + DFT_ROWS, :] = (p + q).astype(BF16)

    for t in range(n_rev):
        src = half - (t + 1) * REV
        r = jnp.dot(jmat, u_ref[src:src + REV, :], preferred_element_type=F32)
        if t == 0:
            r = _patch_first_row(r, nyq)
        else:
            r = _patch_first_row(r, u_ref[src + REV:src + REV + ROW_PACK, :].astype(F32))
        o_ref[half + t * REV:half + (t + 1) * REV, :] = r.astype(BF16)


def _seq_dft(ab, cmat, smat, jmat, n_col):
    B, _, S, _ = ab.shape
    half = S // 2
    assert half % DFT_ROWS == 0 and half % 2 == 0
    tn = F_WIDTH // n_col
    return pl.pallas_call(
        functools.partial(_seq_dft_kernel, seq=S),
        grid=(B, n_col),
        in_specs=[
            pl.BlockSpec((None, 2, S, tn), lambda b, c: (b, 0, 0, c)),
            _resident((half + ROW_PACK, half)),
            _resident((half, half)),
            _resident((REV, REV)),
        ],
        out_specs=pl.BlockSpec((None, S, tn), lambda b, c: (b, 0, c)),
        out_shape=jax.ShapeDtypeStruct((B, S, F_WIDTH), BF16),
        scratch_shapes=[pltpu.VMEM((half, tn), BF16),
                        pltpu.VMEM((half, tn), BF16),
                        pltpu.VMEM((half, tn), BF16)],
        compiler_params=_params("parallel", "parallel"),
        name="seq_dft",
    )(ab, cmat, smat, jmat)


QK_AHEAD = 1


def _window_attn_kernel(sink_ref, qt_ref, k_ref, kp_ref, kn_ref, vt_ref, vtp_ref, vtn_ref, g_ref,
                        o_ref, kfull, vtfull, ot_all, *, tq, n_blocks):
    s_idx = pl.program_id(1)
    r_blocks = tq // BLOCK
    span = 3 * BLOCK

    kfull[0:BLOCK, :] = kp_ref[...]
    kfull[BLOCK:BLOCK + tq, :] = k_ref[...]
    kfull[BLOCK + tq:, :] = kn_ref[...]
    vtfull[:, 0:BLOCK] = vtp_ref[...]
    vtfull[:, BLOCK:BLOCK + tq] = vt_ref[...]
    vtfull[:, BLOCK + tq:] = vtn_ref[...]

    key_i = lax.broadcasted_iota(jnp.int32, (BLOCK, BLOCK), 0)
    qry_i = lax.broadcasted_iota(jnp.int32, (BLOCK, BLOCK), 1)
    assert WINDOW == BLOCK

    def scores(n, h):
        r0 = h * HEAD_DIM
        q4 = jnp.concatenate(
            [qt_ref[(h * GQA_GROUP + g) * HEAD_DIM:(h * GQA_GROUP + g + 1) * HEAD_DIM,
                    n * BLOCK:(n + 1) * BLOCK] for g in range(GQA_GROUP)], axis=1)
        parts = [q4]
        if r0 > 0:
            parts.insert(0, jnp.zeros((r0, GQA_GROUP * BLOCK), BF16))
        if r0 + HEAD_DIM < KV_WIDTH:
            parts.append(jnp.zeros((KV_WIDTH - r0 - HEAD_DIM, GQA_GROUP * BLOCK), BF16))
        qpad = jnp.concatenate(parts, axis=0)
        kwin = kfull[n * BLOCK:n * BLOCK + span, :]
        return jnp.dot(kwin, qpad, preferred_element_type=F32)

    def attend(n, h, st_all):
        nglob = s_idx * r_blocks + n
        valid_prev = (key_i >= qry_i) & (nglob > 0)
        valid_next = (key_i <= qry_i) & (nglob < n_blocks - 1)
        pn, den = [], []
        for g in range(GQA_GROUP):
            cols = slice(g * BLOCK, (g + 1) * BLOCK)
            s_prev = jnp.where(valid_prev, st_all[0:BLOCK, cols], NEG_INF)
            s_self = st_all[BLOCK:2 * BLOCK, cols]
            s_next = jnp.where(valid_next, st_all[2 * BLOCK:, cols], NEG_INF)
            sink = jnp.full((1, BLOCK), sink_ref[h * GQA_GROUP + g] * LOG2_E, F32)
            m = jnp.max(jnp.maximum(jnp.maximum(s_prev, s_self), s_next), axis=0, keepdims=True)
            m = jnp.maximum(m, sink)
            p_prev = jnp.exp2(s_prev - m)
            p_self = jnp.exp2(s_self - m)
            p_next = jnp.exp2(s_next - m)
            den.append(jnp.sum(p_prev + p_self + p_next, axis=0, keepdims=True)
                       + jnp.exp2(sink - m))
            pn.append(jnp.concatenate([p_prev, p_self, p_next], axis=0).astype(BF16))
        r0 = h * HEAD_DIM
        vwin = vtfull[r0:r0 + HEAD_DIM, n * BLOCK:n * BLOCK + span]
        ot = jnp.dot(vwin, jnp.concatenate(pn, axis=1), preferred_element_type=F32)
        ot = ot * (1.0 / jnp.concatenate(den, axis=1))
        for g in range(GQA_GROUP):
            q0 = (h * GQA_GROUP + g) * HEAD_DIM
            ot_all[q0:q0 + HEAD_DIM, n * BLOCK:(n + 1) * BLOCK] = ot[:, g * BLOCK:(g + 1) * BLOCK]

    pairs = [(n, h) for n in range(r_blocks) for h in range(N_KV_HEADS)]
    pending = [scores(*pr) for pr in pairs[:QK_AHEAD]]
    for i, (n, h) in enumerate(pairs):
        if i + QK_AHEAD < len(pairs):
            pending.append(scores(*pairs[i + QK_AHEAD]))
        attend(n, h, pending.pop(0))

    y = ot_all[...]
    r = lax.rsqrt(jnp.mean(y * y, axis=0, keepdims=True) + EPS)
    o_ref[...] = ((y * r).T * g_ref[...]).astype(BF16)


def _window_attn(sink, qt, k, vt, g, tq):
    B, _, S = qt.shape
    n_blocks = S // BLOCK
    r_blocks = tq // BLOCK
    prev_blk = lambda s: jnp.maximum(s * r_blocks - 1, 0)
    next_blk = lambda s: jnp.minimum((s + 1) * r_blocks, n_blocks - 1)
    return pl.pallas_call(
        functools.partial(_window_attn_kernel, tq=tq, n_blocks=n_blocks),
        grid=(B, S // tq),
        in_specs=[
            pl.BlockSpec(memory_space=pltpu.SMEM),
            pl.BlockSpec((None, A_WIDTH, tq), lambda b, s: (b, 0, s)),
            pl.BlockSpec((None, tq, KV_WIDTH), lambda b, s: (b, s, 0)),
            pl.BlockSpec((None, BLOCK, KV_WIDTH), lambda b, s: (b, prev_blk(s), 0)),
            pl.BlockSpec((None, BLOCK, KV_WIDTH), lambda b, s: (b, next_blk(s), 0)),
            pl.BlockSpec((None, KV_WIDTH, tq), lambda b, s: (b, 0, s)),
            pl.BlockSpec((None, KV_WIDTH, BLOCK), lambda b, s: (b, 0, prev_blk(s))),
            pl.BlockSpec((None, KV_WIDTH, BLOCK), lambda b, s: (b, 0, next_blk(s))),
            pl.BlockSpec((1, A_WIDTH), lambda b, s: (0, 0)),
        ],
        out_specs=pl.BlockSpec((None, tq, A_WIDTH), lambda b, s: (b, s, 0)),
        out_shape=jax.ShapeDtypeStruct((B, S, A_WIDTH), BF16),
        scratch_shapes=[
            pltpu.VMEM((tq + 2 * BLOCK, KV_WIDTH), BF16),
            pltpu.VMEM((KV_WIDTH, tq + 2 * BLOCK), BF16),
            pltpu.VMEM((A_WIDTH, tq), F32),
        ],
        compiler_params=_params("parallel", "parallel"),
        name="window_attn",
    )(sink, qt, k, k, k, vt, vt, vt, g)


def _out_proj_kernel(of_ref, oa_ref, x_ref, wof_ref, woa_ref, gof_ref, g_ref, x1_ref, h2_ref):
    of = of_ref[...].astype(F32)
    rf = lax.rsqrt(jnp.mean(of * of, axis=-1, keepdims=True) + EPS)
    of_n = ((of * rf) * gof_ref[...]).astype(BF16)
    y = jnp.dot(of_n, wof_ref[...], preferred_element_type=F32)
    y = y + jnp.dot(oa_ref[...], woa_ref[...], preferred_element_type=F32)
    x1 = x_ref[...] + y
    x1_ref[...] = x1
    r = lax.rsqrt(jnp.mean(x1 * x1, axis=-1, keepdims=True) + EPS)
    h2_ref[...] = ((x1 * r) * g_ref[...]).astype(BF16)


def _out_proj(of, oa_n, x, wof, woa, gof, g, tm):
    T = x.shape[0]
    return pl.pallas_call(
        _out_proj_kernel,
        grid=(T // tm,),
        in_specs=[
            pl.BlockSpec((tm, F_WIDTH), lambda i: (i, 0)),
            pl.BlockSpec((tm, A_WIDTH), lambda i: (i, 0)),
            pl.BlockSpec((tm, D_MODEL), lambda i: (i, 0)),
            _resident((F_WIDTH, D_MODEL)),
            _resident((A_WIDTH, D_MODEL)),
            _resident((1, F_WIDTH)),
            _resident((1, D_MODEL)),
        ],
        out_specs=[
            pl.BlockSpec((tm, D_MODEL), lambda i: (i, 0)),
            pl.BlockSpec((tm, D_MODEL), lambda i: (i, 0)),
        ],
        out_shape=[
            jax.ShapeDtypeStruct((T, D_MODEL), F32),
            jax.ShapeDtypeStruct((T, D_MODEL), BF16),
        ],
        compiler_params=_params("parallel"),
        name="out_proj",
    )(of, oa_n, x, wof, woa, gof, g)


def _mlp_kernel(h2_ref, x1_ref, wu_ref, wd_ref, g_ref, o_ref, acc_ref):
    j = pl.program_id(1)

    @pl.when(j == 0)
    def _():
        acc_ref[...] = x1_ref[...]

    u = jnp.maximum(jnp.dot(h2_ref[...], wu_ref[...], preferred_element_type=F32), 0.0)
    acc_ref[...] += jnp.dot((u * u).astype(BF16), wd_ref[...], preferred_element_type=F32)

    @pl.when(j == pl.num_programs(1) - 1)
    def _():
        y = acc_ref[...]
        r = lax.rsqrt(jnp.mean(y * y, axis=-1, keepdims=True) + EPS)
        o_ref[...] = (y * r) * g_ref[...]


def _mlp(h2, x1, wu, wd, g, tm, tf):
    T = h2.shape[0]
    return pl.pallas_call(
        _mlp_kernel,
        grid=(T // tm, D_FF // tf),
        in_specs=[
            pl.BlockSpec((tm, D_MODEL), lambda i, j: (i, 0)),
            pl.BlockSpec((tm, D_MODEL), lambda i, j: (i, 0)),
            pl.BlockSpec((D_MODEL, tf), lambda i, j: (0, j)),
            pl.BlockSpec((tf, D_MODEL), lambda i, j: (j, 0)),
            _resident((1, D_MODEL)),
        ],
        out_specs=pl.BlockSpec((tm, D_MODEL), lambda i, j: (i, 0)),
        out_shape=jax.ShapeDtypeStruct((T, D_MODEL), F32),
        scratch_shapes=[pltpu.VMEM((tm, D_MODEL), F32)],
        compiler_params=_params("parallel", "arbitrary"),
        name="mlp",
    )(h2, x1, wu, wd, g)


def _rope_tables(S):
    inv_freq = 1.0 / (ROPE_THETA ** (jnp.arange(0, HEAD_DIM, 2, dtype=F32) / HEAD_DIM))
    pos = jnp.arange(S, dtype=F32)
    ang = pos[:, None] * inv_freq[None, :]
    ang = jnp.concatenate([ang, ang], axis=-1)
    cos, sin = jnp.cos(ang), jnp.sin(ang)
    sign = jnp.where(jnp.arange(HEAD_DIM) < HALF_DIM, -1.0, 1.0).astype(F32)
    sin = sin * sign[None, :]
    reps = LANES // HEAD_DIM
    return cos.T, sin.T, jnp.tile(cos, (1, reps)), jnp.tile(sin, (1, reps))


def _seq_dft_matrices(S):
    half = S // 2
    hi_n = 64
    lo_n = half // hi_n
    k = jnp.arange(half, dtype=jnp.int32)
    w = 2.0 * math.pi / S
    a_hi = ((jnp.arange(hi_n, dtype=jnp.int32)[:, None] * lo_n * k[None, :]) % S).astype(F32) * w
    a_lo = ((jnp.arange(lo_n, dtype=jnp.int32)[:, None] * k[None, :]) % S).astype(F32) * w
    ch, sh = jnp.cos(a_hi)[:, None, :], jnp.sin(a_hi)[:, None, :]
    cl, sl = jnp.cos(a_lo)[None, :, :], jnp.sin(a_lo)[None, :, :]
    scale = S ** -0.5
    cos = ((ch * cl - sh * sl) * scale).reshape(half, half)
    sin = ((sh * cl + ch * sl) * scale).reshape(half, half)
    alt = jnp.where(k % 2 == 0, scale, -scale).astype(F32)[None, :]
    cos = jnp.concatenate([cos, alt, jnp.zeros((ROW_PACK - 1, half), F32)], axis=0)
    i = jnp.arange(REV, dtype=jnp.int32)
    jmat = ((i[:, None] + i[None, :]) == REV).astype(BF16)
    return cos.astype(BF16), sin.astype(BF16), jmat


def _trunk(x, w, tm_in, tq, dft_elems, tm_out, tm_mlp, tf):
    B, S, _ = x.shape
    cost, sint, cos, sin = _rope_tables(S)
    ab, qt, k, vt = _in_proj(x, w["g_mix"], w["wf"], w["wqt"], w["wk"], w["wvt"], w["mg"],
                             cost, sint, cos, sin, tm_in)
    of = _seq_dft(ab, *_seq_dft_matrices(S), n_col=F_WIDTH * S // dft_elems)
    oa_n = _window_attn(w["sink"], qt, k, vt, w["g_oa"], tq)
    T = B * S
    x2 = x.reshape(T, D_MODEL)
    x1, h2 = _out_proj(of.reshape(T, F_WIDTH), oa_n.reshape(T, A_WIDTH), x2,
                       w["wof"], w["woa"], w["g_of"], w["g_mlp"], tm_out)
    y = _mlp(h2, x1, w["wu"], w["wd"], w["g_final"], tm_mlp, tf)
    return y.reshape(B, S, D_MODEL)


def _prepare_weights(ln_mix_g, w_in, w_fourier, attn_sink, out_norm_fourier_g, out_norm_attn_g,
                     w_out, ln_mlp_g, w_up, w_down, ln_final_g):
    o1, o2, o3 = F_WIDTH, F_WIDTH + A_WIDTH, F_WIDTH + A_WIDTH + KV_WIDTH
    w_in_b = w_in.astype(BF16)
    w_out_b = w_out.astype(BF16)
    return {
        "g_mix": ln_mix_g.reshape(1, D_MODEL),
        "wf": w_in_b[:, :o1],
        "wqt": w_in_b[:, o1:o2].T,
        "wk": w_in_b[:, o2:o3],
        "wvt": w_in_b[:, o3:].T,
        "mg": _fourier_weights(w_fourier),
        "sink": attn_sink.astype(F32),
        "g_of": out_norm_fourier_g.reshape(1, F_WIDTH),
        "g_oa": out_norm_attn_g.reshape(1, A_WIDTH),
        "wof": w_out_b[:F_WIDTH],
        "woa": w_out_b[F_WIDTH:],
        "g_mlp": ln_mlp_g.reshape(1, D_MODEL),
        "wu": w_up.astype(BF16),
        "wd": w_down.astype(BF16),
        "g_final": ln_final_g.reshape(1, D_MODEL),
    }


def kernel(x_prompt, x_sample, ln_mix_g, w_in, w_fourier, attn_sink, out_norm_fourier_g,
           out_norm_attn_g, w_out, ln_mlp_g, w_up, w_down, ln_final_g):
    assert ln_mix_g.shape[0] == 1, "single-layer block"
    w = _prepare_weights(ln_mix_g[0], w_in[0], w_fourier[0], attn_sink[0], out_norm_fourier_g[0],
                         out_norm_attn_g[0], w_out[0], ln_mlp_g[0], w_up[0], w_down[0], ln_final_g)
    tiles = dict(tm_in=512, tq=512, dft_elems=4096 * 256, tm_out=512, tm_mlp=512, tf=1024)
    return (_trunk(x_prompt, w, **tiles), _trunk(x_sample, w, **tiles))
```

```python
import functools
import math

import jax
import jax.numpy as jnp
from jax import lax
from jax.experimental import pallas as pl
from jax.experimental.pallas import tpu as pltpu

D_MODEL = 2048
F_WIDTH = 1024
F_GROUPS = 8
F_CH = 128
A_WIDTH = 1024
HEAD_DIM = 64
HALF_DIM = HEAD_DIM // 2
N_Q_HEADS = 16
N_KV_HEADS = 4
GQA_GROUP = 4
KV_WIDTH = 256
WINDOW = 128
BLOCK = 128
ROPE_THETA = 10000.0
D_FF = 4 * D_MODEL
EPS = 1e-6
NEG_INF = -1e30
LOG2_E = math.log2(math.e)

LANES = 128
VMEM_LIMIT = 60 * 1024 * 1024

BF16 = jnp.bfloat16
F32 = jnp.float32

_NT = (((1,), (1,)), ((), ()))


def _params(*sem, flags=None):
    return pltpu.CompilerParams(dimension_semantics=sem, vmem_limit_bytes=VMEM_LIMIT, flags=flags)


def _resident(shape):
    nd = len(shape)
    return pl.BlockSpec(shape, lambda *_: (0,) * nd, pipeline_mode=pl.Buffered(1))


def _fourier_weights_kernel(cs_ref, w_ref, o_ref):
    w = w_ref[...]
    c = jnp.dot(cs_ref[0], w, preferred_element_type=F32, precision=lax.Precision.HIGHEST)
    s = jnp.dot(cs_ref[1], w, preferred_element_type=F32, precision=lax.Precision.HIGHEST)
    o_ref[:, :F_CH] = c.astype(BF16)
    o_ref[:, F_CH:] = s.astype(BF16)


def _fourier_weights(w_fourier):
    n = jnp.arange(F_CH, dtype=jnp.int32)
    ang = ((n[:, None] * n[None, :]) % F_CH).astype(F32) * (2.0 * math.pi / F_CH)
    cs = jnp.stack([jnp.cos(ang), jnp.sin(ang)]) * (F_CH ** -0.5)
    return pl.pallas_call(
        _fourier_weights_kernel,
        grid=(F_GROUPS,),
        in_specs=[pl.BlockSpec((2, F_CH, F_CH), lambda g: (0, 0, 0)),
                  pl.BlockSpec((None, F_CH, F_CH), lambda g: (g, 0, 0))],
        out_specs=pl.BlockSpec((None, F_CH, 2 * F_CH), lambda g: (g, 0, 0)),
        out_shape=jax.ShapeDtypeStruct((F_GROUPS, F_CH, 2 * F_CH), BF16),
        name="fourier_weights",
    )(cs, w_fourier)


IN_SUB = 2


def _in_proj_kernel(x_ref, g_ref, wf_ref, wqt_ref, wk_ref, wvt_ref, mg_ref,
                    cost_ref, sint_ref, cos_ref, sin_ref,
                    ab_ref, qt_ref, k_ref, vt_ref):
    tm = x_ref.shape[0]
    sub = tm // IN_SUB
    scale = HEAD_DIM ** -0.5 * LOG2_E
    first_half = (lax.broadcasted_iota(jnp.int32, (sub, LANES), 1) % HEAD_DIM) < HALF_DIM
    for i in range(IN_SUB):
        rows = slice(i * sub, (i + 1) * sub)
        x = x_ref[rows, :]
        r = lax.rsqrt(jnp.mean(x * x, axis=-1, keepdims=True) + EPS)
        h = ((x * r) * g_ref[...]).astype(BF16)

        zf = jnp.dot(h, wf_ref[...], preferred_element_type=F32).astype(BF16)
        qt = lax.dot_general(wqt_ref[...], h, _NT, preferred_element_type=F32)
        kk = jnp.dot(h, wk_ref[...], preferred_element_type=F32)
        vt = lax.dot_general(wvt_ref[...], h, _NT, preferred_element_type=F32)

        for g in range(F_GROUPS):
            lo = g * F_CH
            ab = jnp.dot(zf[:, lo:lo + F_CH], mg_ref[g], preferred_element_type=F32)
            ab_ref[0, rows, lo:lo + F_CH] = ab[:, :F_CH].astype(BF16)
            ab_ref[1, rows, lo:lo + F_CH] = ab[:, F_CH:].astype(BF16)

        cost = cost_ref[:, rows]
        sint = sint_ref[:, rows]
        for hd in range(N_Q_HEADS):
            lo = hd * HEAD_DIM
            blk = qt[lo:lo + HEAD_DIM]
            rot = jnp.concatenate([blk[HALF_DIM:], blk[:HALF_DIM]], axis=0)
            qt_ref[lo:lo + HEAD_DIM, rows] = ((blk * cost + rot * sint) * scale).astype(BF16)

        cos = cos_ref[rows, :]
        sin = sin_ref[rows, :]
        for c in range(KV_WIDTH // LANES):
            blk = kk[:, c * LANES:(c + 1) * LANES]
            rot = jnp.where(first_half,
                            pltpu.roll(blk, LANES - HALF_DIM, axis=1),
                            pltpu.roll(blk, HALF_DIM, axis=1))
            k_ref[rows, c * LANES:(c + 1) * LANES] = (blk * cos + rot * sin).astype(BF16)

        vt_ref[:, rows] = vt.astype(BF16)


def _in_proj(x, g, wf, wqt, wk, wvt, mg, cost, sint, cos, sin, tm):
    B, S, _ = x.shape
    return pl.pallas_call(
        _in_proj_kernel,
        grid=(B, S // tm),
        in_specs=[
            pl.BlockSpec((None, tm, D_MODEL), lambda b, s: (b, s, 0)),
            _resident((1, D_MODEL)),
            _resident((D_MODEL, F_WIDTH)),
            _resident((A_WIDTH, D_MODEL)),
            _resident((D_MODEL, KV_WIDTH)),
            _resident((KV_WIDTH, D_MODEL)),
            _resident((F_GROUPS, F_CH, 2 * F_CH)),
            pl.BlockSpec((HEAD_DIM, tm), lambda b, s: (0, s)),
            pl.BlockSpec((HEAD_DIM, tm), lambda b, s: (0, s)),
            pl.BlockSpec((tm, LANES), lambda b, s: (s, 0)),
            pl.BlockSpec((tm, LANES), lambda b, s: (s, 0)),
        ],
        out_specs=[
            pl.BlockSpec((None, 2, tm, F_WIDTH), lambda b, s: (b, 0, s, 0)),
            pl.BlockSpec((None, A_WIDTH, tm), lambda b, s: (b, 0, s)),
            pl.BlockSpec((None, tm, KV_WIDTH), lambda b, s: (b, s, 0)),
            pl.BlockSpec((None, KV_WIDTH, tm), lambda b, s: (b, 0, s)),
        ],
        out_shape=[
            jax.ShapeDtypeStruct((B, 2, S, F_WIDTH), BF16),
            jax.ShapeDtypeStruct((B, A_WIDTH, S), BF16),
            jax.ShapeDtypeStruct((B, S, KV_WIDTH), BF16),
            jax.ShapeDtypeStruct((B, KV_WIDTH, S), BF16),
        ],
        compiler_params=_params("parallel", "parallel"),
        name="in_proj",
    )(x, g, wf, wqt, wk, wvt, mg, cost, sint, cos, sin)


REV = 256
ROW_PACK = 16
DFT_ROWS = 512


def _first_row(x):
    return jnp.where(lax.broadcasted_iota(jnp.int32, x.shape, 0) == 0, x, 0.0)


def _patch_first_row(tile, row):
    head = tile[:ROW_PACK] + _first_row(row)
    return jnp.concatenate([head, tile[ROW_PACK:]], axis=0)


def _seq_dft_kernel(ab_ref, c_ref, s_ref, j_ref, o_ref, ap_ref, bm_ref, u_ref, *, seq):
    half = seq // 2
    n_rev = half // REV
    jmat = j_ref[...]
    a_ref = ab_ref.at[0]
    b_ref = ab_ref.at[1]

    for t in range(n_rev):
        lo = t * REV
        src = seq - lo - REV
        ra = jnp.dot(jmat, a_ref[src:src + REV, :], preferred_element_type=F32)
        rb = jnp.dot(jmat, b_ref[src:src + REV, :], preferred_element_type=F32)
        if t > 0:
            ra = _patch_first_row(ra, a_ref[seq - lo:seq - lo + ROW_PACK, :].astype(F32))
            rb = _patch_first_row(rb, b_ref[seq - lo:seq - lo + ROW_PACK, :].astype(F32))
        ap_ref[lo:lo + REV, :] = (a_ref[lo:lo + REV, :].astype(F32) + ra).astype(BF16)
        bm_ref[lo:lo + REV, :] = (b_ref[lo:lo + REV, :].astype(F32) - rb).astype(BF16)

    a_half = a_ref[half:half + ROW_PACK, :].astype(F32)[0:1, :] * (seq ** -0.5)
    row_par = lax.broadcasted_iota(jnp.int32, (DFT_ROWS, 1), 0) % 2
    sgn_a_half = jnp.where(row_par == 0, a_half, -a_half)

    ap = ap_ref[...]
    bm = bm_ref[...]
    n_row = half // DFT_ROWS
    nyq = None
    for i in range(n_row):
        lo = i * DFT_ROWS
        extra = ROW_PACK if i == n_row - 1 else 0
        p = jnp.dot(c_ref[lo:lo + DFT_ROWS + extra, :], ap, preferred_element_type=F32)
        if extra:
            nyq = p[DFT_ROWS:] + a_half
            p = p[:DFT_ROWS]
        p = p + sgn_a_half
        q = jnp.dot(s_ref[lo:lo + DFT_ROWS, :], bm, preferred_element_type=F32)
        o_ref[lo:lo + DFT_ROWS, :] = (p - q).astype(BF16)
        u_ref[lo:lo + DFT_ROWS, :] = (p + q).astype(BF16)

    for t in range(n_rev):
        src = half - (t + 1) * REV
        r = jnp.dot(jmat, u_ref[src:src + REV, :], preferred_element_type=F32)
        if t == 0:
            r = _patch_first_row(r, nyq)
        else:
            r = _patch_first_row(r, u_ref[src + REV:src + REV + ROW_PACK, :].astype(F32))
        o_ref[half + t * REV:half + (t + 1) * REV, :] = r.astype(BF16)


def _seq_dft(ab, cmat, smat, jmat, n_col):
    B, _, S, _ = ab.shape
    half = S // 2
    assert half % DFT_ROWS == 0 and half % 2 == 0
    tn = F_WIDTH // n_col
    return pl.pallas_call(
        functools.partial(_seq_dft_kernel, seq=S),
        grid=(B, n_col),
        in_specs=[
            pl.BlockSpec((None, 2, S, tn), lambda b, c: (b, 0, 0, c)),
            _resident((half + ROW_PACK, half)),
            _resident((half, half)),
            _resident((REV, REV)),
        ],
        out_specs=pl.BlockSpec((None, S, tn), lambda b, c: (b, 0, c)),
        out_shape=jax.ShapeDtypeStruct((B, S, F_WIDTH), BF16),
        scratch_shapes=[pltpu.VMEM((half, tn), BF16),
                        pltpu.VMEM((half, tn), BF16),
                        pltpu.VMEM((half, tn), BF16)],
        compiler_params=_params("parallel", "parallel"),
        name="seq_dft",
    )(ab, cmat, smat, jmat)


QK_AHEAD = 1


def _window_attn_kernel(sink_ref, qt_ref, k_ref, kp_ref, kn_ref, vt_ref, vtp_ref, vtn_ref, g_ref,
                        o_ref, kfull, vtfull, ot_all, *, tq, n_blocks):
    s_idx = pl.program_id(1)
    r_blocks = tq // BLOCK
    span = 3 * BLOCK

    kfull[0:BLOCK, :] = kp_ref[...]
    kfull[BLOCK:BLOCK + tq, :] = k_ref[...]
    kfull[BLOCK + tq:, :] = kn_ref[...]
    vtfull[:, 0:BLOCK] = vtp_ref[...]
    vtfull[:, BLOCK:BLOCK + tq] = vt_ref[...]
    vtfull[:, BLOCK + tq:] = vtn_ref[...]

    key_i = lax.broadcasted_iota(jnp.int32, (BLOCK, BLOCK), 0)
    qry_i = lax.broadcasted_iota(jnp.int32, (BLOCK, BLOCK), 1)
    assert WINDOW == BLOCK

    def scores(n, h):
        r0 = h * HEAD_DIM
        q4 = jnp.concatenate(
            [qt_ref[(h * GQA_GROUP + g) * HEAD_DIM:(h * GQA_GROUP + g + 1) * HEAD_DIM,
                    n * BLOCK:(n + 1) * BLOCK] for g in range(GQA_GROUP)], axis=1)
        parts = [q4]
        if r0 > 0:
            parts.insert(0, jnp.zeros((r0, GQA_GROUP * BLOCK), BF16))
        if r0 + HEAD_DIM < KV_WIDTH:
            parts.append(jnp.zeros((KV_WIDTH - r0 - HEAD_DIM, GQA_GROUP * BLOCK), BF16))
        qpad = jnp.concatenate(parts, axis=0)
        kwin = kfull[n * BLOCK:n * BLOCK + span, :]
        return jnp.dot(kwin, qpad, preferred_element_type=F32)

    def attend(n, h, st_all):
        nglob = s_idx * r_blocks + n
        valid_prev = (key_i >= qry_i) & (nglob > 0)
        valid_next = (key_i <= qry_i) & (nglob < n_blocks - 1)
        pn, den = [], []
        for g in range(GQA_GROUP):
            cols = slice(g * BLOCK, (g + 1) * BLOCK)
            s_prev = jnp.where(valid_prev, st_all[0:BLOCK, cols], NEG_INF)
            s_self = st_all[BLOCK:2 * BLOCK, cols]
            s_next = jnp.where(valid_next, st_all[2 * BLOCK:, cols], NEG_INF)
            sink = jnp.full((1, BLOCK), sink_ref[h * GQA_GROUP + g] * LOG2_E, F32)
            m = jnp.max(jnp.maximum(jnp.maximum(s_prev, s_self), s_next), axis=0, keepdims=True)
            m = jnp.maximum(m, sink)
            p_prev = jnp.exp2(s_prev - m)
            p_self = jnp.exp2(s_self - m)
            p_next = jnp.exp2(s_next - m)
            den.append(jnp.sum(p_prev + p_self + p_next, axis=0, keepdims=True)
                       + jnp.exp2(sink - m))
            pn.append(jnp.concatenate([p_prev, p_self, p_next], axis=0).astype(BF16))
        r0 = h * HEAD_DIM
        vwin = vtfull[r0:r0 + HEAD_DIM, n * BLOCK:n * BLOCK + span]
        ot = jnp.dot(vwin, jnp.concatenate(pn, axis=1), preferred_element_type=F32)
        ot = ot * (1.0 / jnp.concatenate(den, axis=1))
        for g in range(GQA_GROUP):
            q0 = (h * GQA_GROUP + g) * HEAD_DIM
            ot_all[q0:q0 + HEAD_DIM, n * BLOCK:(n + 1) * BLOCK] = ot[:, g * BLOCK:(g + 1) * BLOCK]

    pairs = [(n, h) for n in range(r_blocks) for h in range(N_KV_HEADS)]
    pending = [scores(*pr) for pr in pairs[:QK_AHEAD]]
    for i, (n, h) in enumerate(pairs):
        if i + QK_AHEAD < len(pairs):
            pending.append(scores(*pairs[i + QK_AHEAD]))
        attend(n, h, pending.pop(0))

    y = ot_all[...]
    r = lax.rsqrt(jnp.mean(y * y, axis=0, keepdims=True) + EPS)
    o_ref[...] = ((y * r).T * g_ref[...]).astype(BF16)


def _window_attn(sink, qt, k, vt, g, tq):
    B, _, S = qt.shape
    n_blocks = S // BLOCK
    r_blocks = tq // BLOCK
    prev_blk = lambda s: jnp.maximum(s * r_blocks - 1, 0)
    next_blk = lambda s: jnp.minimum((s + 1) * r_blocks, n_blocks - 1)
    return pl.pallas_call(
        functools.partial(_window_attn_kernel, tq=tq, n_blocks=n_blocks),
        grid=(B, S // tq),
        in_specs=[
            pl.BlockSpec(memory_space=pltpu.SMEM),
            pl.BlockSpec((None, A_WIDTH, tq), lambda b, s: (b, 0, s)),
            pl.BlockSpec((None, tq, KV_WIDTH), lambda b, s: (b, s, 0)),
            pl.BlockSpec((None, BLOCK, KV_WIDTH), lambda b, s: (b, prev_blk(s), 0)),
            pl.BlockSpec((None, BLOCK, KV_WIDTH), lambda b, s: (b, next_blk(s), 0)),
            pl.BlockSpec((None, KV_WIDTH, tq), lambda b, s: (b, 0, s)),
            pl.BlockSpec((None, KV_WIDTH, BLOCK), lambda b, s: (b, 0, prev_blk(s))),
            pl.BlockSpec((None, KV_WIDTH, BLOCK), lambda b, s: (b, 0, next_blk(s))),
            pl.BlockSpec((1, A_WIDTH), lambda b, s: (0, 0)),
        ],
        out_specs=pl.BlockSpec((None, tq, A_WIDTH), lambda b, s: (b, s, 0)),
        out_shape=jax.ShapeDtypeStruct((B, S, A_WIDTH), BF16),
        scratch_shapes=[
            pltpu.VMEM((tq + 2 * BLOCK, KV_WIDTH), BF16),
            pltpu.VMEM((KV_WIDTH, tq + 2 * BLOCK), BF16),
            pltpu.VMEM((A_WIDTH, tq), F32),
        ],
        compiler_params=_params("parallel", "parallel"),
        name="window_attn",
    )(sink, qt, k, k, k, vt, vt, vt, g)


def _out_proj_kernel(of_ref, oa_ref, x_ref, wof_ref, woa_ref, gof_ref, g_ref, x1_ref, h2_ref):
    of = of_ref[...].astype(F32)
    rf = lax.rsqrt(jnp.mean(of * of, axis=-1, keepdims=True) + EPS)
    of_n = ((of * rf) * gof_ref[...]).astype(BF16)
    y = jnp.dot(of_n, wof_ref[...], preferred_element_type=F32)
    y = y + jnp.dot(oa_ref[...], woa_ref[...], preferred_element_type=F32)
    x1 = x_ref[...] + y
    x1_ref[...] = x1
    r = lax.rsqrt(jnp.mean(x1 * x1, axis=-1, keepdims=True) + EPS)
    h2_ref[...] = ((x1 * r) * g_ref[...]).astype(BF16)


def _out_proj(of, oa_n, x, wof, woa, gof, g, tm):
    T = x.shape[0]
    return pl.pallas_call(
        _out_proj_kernel,
        grid=(T // tm,),
        in_specs=[
            pl.BlockSpec((tm, F_WIDTH), lambda i: (i, 0)),
            pl.BlockSpec((tm, A_WIDTH), lambda i: (i, 0)),
            pl.BlockSpec((tm, D_MODEL), lambda i: (i, 0)),
            _resident((F_WIDTH, D_MODEL)),
            _resident((A_WIDTH, D_MODEL)),
            _resident((1, F_WIDTH)),
            _resident((1, D_MODEL)),
        ],
        out_specs=[
            pl.BlockSpec((tm, D_MODEL), lambda i: (i, 0)),
            pl.BlockSpec((tm, D_MODEL), lambda i: (i, 0)),
        ],
        out_shape=[
            jax.ShapeDtypeStruct((T, D_MODEL), F32),
            jax.ShapeDtypeStruct((T, D_MODEL), BF16),
        ],
        compiler_params=_params("parallel"),
        name="out_proj",
    )(of, oa_n, x, wof, woa, gof, g)


MLP_CHUNKS = 2


def _mlp_kernel(h2_ref, x1_ref, wu_ref, wd_ref, g_ref, o_ref):
    j = pl.program_id(1)

    @pl.when(j == 0)
    def _():
        o_ref[...] = x1_ref[...]

    tf = wu_ref.shape[1]
    for c in range(MLP_CHUNKS):
        lo, hi = c * tf // MLP_CHUNKS, (c + 1) * tf // MLP_CHUNKS
        u = jnp.maximum(jnp.dot(h2_ref[...], wu_ref[:, lo:hi], preferred_element_type=F32), 0.0)
        o_ref[...] += jnp.dot((u * u).astype(BF16), wd_ref[lo:hi, :], preferred_element_type=F32)

    @pl.when(j == pl.num_programs(1) - 1)
    def _():
        y = o_ref[...]
        r = lax.rsqrt(jnp.mean(y * y, axis=-1, keepdims=True) + EPS)
        o_ref[...] = (y * r) * g_ref[...]


def _mlp(h2, x1, wu, wd, g, tm, tf):
    T = h2.shape[0]
    return pl.pallas_call(
        _mlp_kernel,
        grid=(T // tm, D_FF // tf),
        in_specs=[
            pl.BlockSpec((tm, D_MODEL), lambda i, j: (i, 0)),
            pl.BlockSpec((tm, D_MODEL), lambda i, j: (i, 0)),
            pl.BlockSpec((D_MODEL, tf), lambda i, j: (0, j)),
            pl.BlockSpec((tf, D_MODEL), lambda i, j: (j, 0)),
            _resident((1, D_MODEL)),
        ],
        out_specs=pl.BlockSpec((tm, D_MODEL), lambda i, j: (i, 0)),
        out_shape=jax.ShapeDtypeStruct((T, D_MODEL), F32),
        compiler_params=_params("parallel", "arbitrary"),
        name="mlp",
    )(h2, x1, wu, wd, g)


def _rope_tables(S):
    inv_freq = 1.0 / (ROPE_THETA ** (jnp.arange(0, HEAD_DIM, 2, dtype=F32) / HEAD_DIM))
    pos = jnp.arange(S, dtype=F32)
    ang = pos[:, None] * inv_freq[None, :]
    ang = jnp.concatenate([ang, ang], axis=-1)
    cos, sin = jnp.cos(ang), jnp.sin(ang)
    sign = jnp.where(jnp.arange(HEAD_DIM) < HALF_DIM, -1.0, 1.0).astype(F32)
    sin = sin * sign[None, :]
    reps = LANES // HEAD_DIM
    return cos.T, sin.T, jnp.tile(cos, (1, reps)), jnp.tile(sin, (1, reps))


def _seq_dft_matrices(S):
    half = S // 2
    hi_n = 64
    lo_n = half // hi_n
    k = jnp.arange(half, dtype=jnp.int32)
    w = 2.0 * math.pi / S
    a_hi = ((jnp.arange(hi_n, dtype=jnp.int32)[:, None] * lo_n * k[None, :]) % S).astype(F32) * w
    a_lo = ((jnp.arange(lo_n, dtype=jnp.int32)[:, None] * k[None, :]) % S).astype(F32) * w
    ch, sh = jnp.cos(a_hi)[:, None, :], jnp.sin(a_hi)[:, None, :]
    cl, sl = jnp.cos(a_lo)[None, :, :], jnp.sin(a_lo)[None, :, :]
    scale = S ** -0.5
    cos = ((ch * cl - sh * sl) * scale).reshape(half, half).astype(BF16)
    sin = ((sh * cl + ch * sl) * scale).reshape(half, half).astype(BF16)
    alt = jnp.where(k % 2 == 0, scale, -scale).astype(BF16)[None, :]
    cos = jnp.concatenate([cos, alt, jnp.zeros((ROW_PACK - 1, half), BF16)], axis=0)
    i = jnp.arange(REV, dtype=jnp.int32)
    jmat = ((i[:, None] + i[None, :]) == REV).astype(BF16)
    return cos, sin, jmat


def _trunk(x, w, tm_in, tq, dft_elems, tm_out, tm_mlp, tf):
    B, S, _ = x.shape
    cost, sint, cos, sin = _rope_tables(S)
    ab, qt, k, vt = _in_proj(x, w["g_mix"], w["wf"], w["wqt"], w["wk"], w["wvt"], w["mg"],
                             cost, sint, cos, sin, tm_in)
    of = _seq_dft(ab, *_seq_dft_matrices(S), n_col=F_WIDTH * S // dft_elems)
    oa_n = _window_attn(w["sink"], qt, k, vt, w["g_oa"], tq)
    T = B * S
    x2 = x.reshape(T, D_MODEL)
    x1, h2 = _out_proj(of.reshape(T, F_WIDTH), oa_n.reshape(T, A_WIDTH), x2,
                       w["wof"], w["woa"], w["g_of"], w["g_mlp"], tm_out)
    y = _mlp(h2, x1, w["wu"], w["wd"], w["g_final"], tm_mlp, tf)
    return y.reshape(B, S, D_MODEL)


def _prepare_weights(ln_mix_g, w_in, w_fourier, attn_sink, out_norm_fourier_g, out_norm_attn_g,
                     w_out, ln_mlp_g, w_up, w_down, ln_final_g):
    o1, o2, o3 = F_WIDTH, F_WIDTH + A_WIDTH, F_WIDTH + A_WIDTH + KV_WIDTH
    w_in_b = w_in.astype(BF16)
    w_out_b = w_out.astype(BF16)
    return {
        "g_mix": ln_mix_g.reshape(1, D_MODEL),
        "wf": w_in_b[:, :o1],
        "wqt": w_in_b[:, o1:o2].T,
        "wk": w_in_b[:, o2:o3],
        "wvt": w_in_b[:, o3:].T,
        "mg": _fourier_weights(w_fourier),
        "sink": attn_sink.astype(F32),
        "g_of": out_norm_fourier_g.reshape(1, F_WIDTH),
        "g_oa": out_norm_attn_g.reshape(1, A_WIDTH),
        "wof": w_out_b[:F_WIDTH],
        "woa": w_out_b[F_WIDTH:],
        "g_mlp": ln_mlp_g.reshape(1, D_MODEL),
        "wu": w_up.astype(BF16),
        "wd": w_down.astype(BF16),
        "g_final": ln_final_g.reshape(1, D_MODEL),
    }


def kernel(x_prompt, x_sample, ln_mix_g, w_in, w_fourier, attn_sink, out_norm_fourier_g,
           out_norm_attn_g, w_out, ln_mlp_g, w_up, w_down, ln_final_g):
    assert ln_mix_g.shape[0] == 1, "single-layer block"
    w = _prepare_weights(ln_mix_g[0], w_in[0], w_fourier[0], attn_sink[0], out_norm_fourier_g[0],
                         out_norm_attn_g[0], w_out[0], ln_mlp_g[0], w_up[0], w_down[0], ln_final_g)
    tiles = dict(tm_in=512, tq=512, dft_elems=4096 * 256, tm_out=512, tm_mlp=512, tf=2048)
    return (_trunk(x_prompt, w, **tiles), _trunk(x_sample, w, **tiles))
```

```python
import functools
import math

import jax
import jax.numpy as jnp
from jax import lax
from jax.experimental import pallas as pl
from jax.experimental.pallas import tpu as pltpu

D_MODEL = 2048
F_WIDTH = 1024
F_GROUPS = 8
F_CH = 128
A_WIDTH = 1024
HEAD_DIM = 64
HALF_DIM = HEAD_DIM // 2
N_Q_HEADS = 16
N_KV_HEADS = 4
GQA_GROUP = 4
KV_WIDTH = 256
WINDOW = 128
BLOCK = 128
ROPE_THETA = 10000.0
D_FF = 4 * D_MODEL
EPS = 1e-6
NEG_INF = -1e30
LOG2_E = math.log2(math.e)

LANES = 128
VMEM_LIMIT = 60 * 1024 * 1024

BF16 = jnp.bfloat16
F32 = jnp.float32

_NT = (((1,), (1,)), ((), ()))


def _params(*sem, flags=None):
    return pltpu.CompilerParams(dimension_semantics=sem, vmem_limit_bytes=VMEM_LIMIT, flags=flags)


def _resident(shape):
    nd = len(shape)
    return pl.BlockSpec(shape, lambda *_: (0,) * nd, pipeline_mode=pl.Buffered(1))


def _fourier_weights_kernel(cs_ref, w_ref, o_ref):
    w = w_ref[...]
    c = jnp.dot(cs_ref[0], w, preferred_element_type=F32, precision=lax.Precision.HIGHEST)
    s = jnp.dot(cs_ref[1], w, preferred_element_type=F32, precision=lax.Precision.HIGHEST)
    o_ref[:, :F_CH] = c.astype(BF16)
    o_ref[:, F_CH:] = s.astype(BF16)


def _fourier_weights(w_fourier):
    n = jnp.arange(F_CH, dtype=jnp.int32)
    ang = ((n[:, None] * n[None, :]) % F_CH).astype(F32) * (2.0 * math.pi / F_CH)
    cs = jnp.stack([jnp.cos(ang), jnp.sin(ang)]) * (F_CH ** -0.5)
    return pl.pallas_call(
        _fourier_weights_kernel,
        grid=(F_GROUPS,),
        in_specs=[pl.BlockSpec((2, F_CH, F_CH), lambda g: (0, 0, 0)),
                  pl.BlockSpec((None, F_CH, F_CH), lambda g: (g, 0, 0))],
        out_specs=pl.BlockSpec((None, F_CH, 2 * F_CH), lambda g: (g, 0, 0)),
        out_shape=jax.ShapeDtypeStruct((F_GROUPS, F_CH, 2 * F_CH), BF16),
        name="fourier_weights",
    )(cs, w_fourier)


IN_SUB = 4


def _in_proj_kernel(x_ref, g_ref, wf_ref, wqt_ref, wk_ref, wvt_ref, mg_ref,
                    cost_ref, sint_ref, cos_ref, sin_ref,
                    ab_ref, qt_ref, k_ref, vt_ref):
    tm = x_ref.shape[0]
    sub = tm // IN_SUB
    scale = HEAD_DIM ** -0.5 * LOG2_E
    first_half = (lax.broadcasted_iota(jnp.int32, (sub, LANES), 1) % HEAD_DIM) < HALF_DIM
    for i in range(IN_SUB):
        rows = slice(i * sub, (i + 1) * sub)
        x = x_ref[rows, :]
        r = lax.rsqrt(jnp.mean(x * x, axis=-1, keepdims=True) + EPS)
        h = ((x * r) * g_ref[...]).astype(BF16)

        zf = jnp.dot(h, wf_ref[...], preferred_element_type=F32).astype(BF16)
        qt = lax.dot_general(wqt_ref[...], h, _NT, preferred_element_type=F32)
        kk = jnp.dot(h, wk_ref[...], preferred_element_type=F32)
        vt = lax.dot_general(wvt_ref[...], h, _NT, preferred_element_type=F32)

        for g in range(F_GROUPS):
            lo = g * F_CH
            ab = jnp.dot(zf[:, lo:lo + F_CH], mg_ref[g], preferred_element_type=F32)
            ab_ref[0, rows, lo:lo + F_CH] = ab[:, :F_CH].astype(BF16)
            ab_ref[1, rows, lo:lo + F_CH] = ab[:, F_CH:].astype(BF16)

        cost = cost_ref[:, rows]
        sint = sint_ref[:, rows]
        for hd in range(N_Q_HEADS):
            lo = hd * HEAD_DIM
            blk = qt[lo:lo + HEAD_DIM]
            rot = jnp.concatenate([blk[HALF_DIM:], blk[:HALF_DIM]], axis=0)
            qt_ref[lo:lo + HEAD_DIM, rows] = ((blk * cost + rot * sint) * scale).astype(BF16)

        cos = cos_ref[rows, :]
        sin = sin_ref[rows, :]
        for c in range(KV_WIDTH // LANES):
            blk = kk[:, c * LANES:(c + 1) * LANES]
            rot = jnp.where(first_half,
                            pltpu.roll(blk, LANES - HALF_DIM, axis=1),
                            pltpu.roll(blk, HALF_DIM, axis=1))
            k_ref[rows, c * LANES:(c + 1) * LANES] = (blk * cos + rot * sin).astype(BF16)

        vt_ref[:, rows] = vt.astype(BF16)


def _in_proj(x, g, wf, wqt, wk, wvt, mg, cost, sint, cos, sin, tm):
    B, S, _ = x.shape
    return pl.pallas_call(
        _in_proj_kernel,
        grid=(B, S // tm),
        in_specs=[
            pl.BlockSpec((None, tm, D_MODEL), lambda b, s: (b, s, 0)),
            _resident((1, D_MODEL)),
            _resident((D_MODEL, F_WIDTH)),
            _resident((A_WIDTH, D_MODEL)),
            _resident((D_MODEL, KV_WIDTH)),
            _resident((KV_WIDTH, D_MODEL)),
            _resident((F_GROUPS, F_CH, 2 * F_CH)),
            pl.BlockSpec((HEAD_DIM, tm), lambda b, s: (0, s)),
            pl.BlockSpec((HEAD_DIM, tm), lambda b, s: (0, s)),
            pl.BlockSpec((tm, LANES), lambda b, s: (s, 0)),
            pl.BlockSpec((tm, LANES), lambda b, s: (s, 0)),
        ],
        out_specs=[
            pl.BlockSpec((None, 2, tm, F_WIDTH), lambda b, s: (b, 0, s, 0)),
            pl.BlockSpec((None, A_WIDTH, tm), lambda b, s: (b, 0, s)),
            pl.BlockSpec((None, tm, KV_WIDTH), lambda b, s: (b, s, 0)),
            pl.BlockSpec((None, KV_WIDTH, tm), lambda b, s: (b, 0, s)),
        ],
        out_shape=[
            jax.ShapeDtypeStruct((B, 2, S, F_WIDTH), BF16),
            jax.ShapeDtypeStruct((B, A_WIDTH, S), BF16),
            jax.ShapeDtypeStruct((B, S, KV_WIDTH), BF16),
            jax.ShapeDtypeStruct((B, KV_WIDTH, S), BF16),
        ],
        compiler_params=_params("parallel", "parallel"),
        name="in_proj",
    )(x, g, wf, wqt, wk, wvt, mg, cost, sint, cos, sin)


REV = 256
ROW_PACK = 16
DFT_ROWS = 512


def _first_row(x):
    return jnp.where(lax.broadcasted_iota(jnp.int32, x.shape, 0) == 0, x, 0.0)


def _patch_first_row(tile, row):
    head = tile[:ROW_PACK] + _first_row(row)
    return jnp.concatenate([head, tile[ROW_PACK:]], axis=0)


def _seq_dft_kernel(ab_ref, c_ref, s_ref, j_ref, o_ref, ap_ref, bm_ref, u_ref, *, seq):
    half = seq // 2
    n_rev = half // REV
    jmat = j_ref[...]
    a_ref = ab_ref.at[0]
    b_ref = ab_ref.at[1]

    for t in range(n_rev):
        lo = t * REV
        src = seq - lo - REV
        ra = jnp.dot(jmat, a_ref[src:src + REV, :], preferred_element_type=F32)
        rb = jnp.dot(jmat, b_ref[src:src + REV, :], preferred_element_type=F32)
        if t > 0:
            ra = _patch_first_row(ra, a_ref[seq - lo:seq - lo + ROW_PACK, :].astype(F32))
            rb = _patch_first_row(rb, b_ref[seq - lo:seq - lo + ROW_PACK, :].astype(F32))
        ap_ref[lo:lo + REV, :] = (a_ref[lo:lo + REV, :].astype(F32) + ra).astype(BF16)
        bm_ref[lo:lo + REV, :] = (b_ref[lo:lo + REV, :].astype(F32) - rb).astype(BF16)

    a_half = a_ref[half:half + ROW_PACK, :].astype(F32)[0:1, :] * (seq ** -0.5)
    row_par = lax.broadcasted_iota(jnp.int32, (DFT_ROWS, 1), 0) % 2
    sgn_a_half = jnp.where(row_par == 0, a_half, -a_half)

    ap = ap_ref[...]
    bm = bm_ref[...]
    n_row = half // DFT_ROWS
    nyq = None
    for i in range(n_row):
        lo = i * DFT_ROWS
        extra = ROW_PACK if i == n_row - 1 else 0
        p = jnp.dot(c_ref[lo:lo + DFT_ROWS + extra, :], ap, preferred_element_type=F32)
        if extra:
            nyq = p[DFT_ROWS:] + a_half
            p = p[:DFT_ROWS]
        p = p + sgn_a_half
        q = jnp.dot(s_ref[lo:lo + DFT_ROWS, :], bm, preferred_element_type=F32)
        o_ref[lo:lo + DFT_ROWS, :] = (p - q).astype(BF16)
        u_ref[lo:lo + DFT_ROWS, :] = (p + q).astype(BF16)

    for t in range(n_rev):
        src = half - (t + 1) * REV
        r = jnp.dot(jmat, u_ref[src:src + REV, :], preferred_element_type=F32)
        if t == 0:
            r = _patch_first_row(r, nyq)
        else:
            r = _patch_first_row(r, u_ref[src + REV:src + REV + ROW_PACK, :].astype(F32))
        o_ref[half + t * REV:half + (t + 1) * REV, :] = r.astype(BF16)


def _seq_dft(ab, cmat, smat, jmat, n_col):
    B, _, S, _ = ab.shape
    half = S // 2
    assert half % DFT_ROWS == 0 and half % 2 == 0
    tn = F_WIDTH // n_col
    return pl.pallas_call(
        functools.partial(_seq_dft_kernel, seq=S),
        grid=(B, n_col),
        in_specs=[
            pl.BlockSpec((None, 2, S, tn), lambda b, c: (b, 0, 0, c)),
            pl.BlockSpec((half + ROW_PACK, half), lambda b, c: (0, 0), pipeline_mode=pl.Buffered(1)),
            _resident((half, half)),
            _resident((REV, REV)),
        ],
        out_specs=pl.BlockSpec((None, S, tn), lambda b, c: (b, 0, c)),
        out_shape=jax.ShapeDtypeStruct((B, S, F_WIDTH), BF16),
        scratch_shapes=[pltpu.VMEM((half, tn), BF16),
                        pltpu.VMEM((half, tn), BF16),
                        pltpu.VMEM((half, tn), BF16)],
        compiler_params=_params("parallel", "parallel"),
        name="seq_dft",
    )(ab, cmat, smat, jmat)


QK_AHEAD = 1


def _window_attn_kernel(sink_ref, qt_ref, k_ref, kp_ref, kn_ref, vt_ref, vtp_ref, vtn_ref, g_ref,
                        o_ref, kfull, vtfull, ot_all, *, tq, n_blocks):
    s_idx = pl.program_id(1)
    r_blocks = tq // BLOCK
    span = 3 * BLOCK

    kfull[0:BLOCK, :] = kp_ref[...]
    kfull[BLOCK:BLOCK + tq, :] = k_ref[...]
    kfull[BLOCK + tq:, :] = kn_ref[...]
    vtfull[:, 0:BLOCK] = vtp_ref[...]
    vtfull[:, BLOCK:BLOCK + tq] = vt_ref[...]
    vtfull[:, BLOCK + tq:] = vtn_ref[...]

    key_i = lax.broadcasted_iota(jnp.int32, (BLOCK, BLOCK), 0)
    qry_i = lax.broadcasted_iota(jnp.int32, (BLOCK, BLOCK), 1)
    assert WINDOW == BLOCK

    def scores(n, h):
        r0 = h * HEAD_DIM
        q4 = jnp.concatenate(
            [qt_ref[(h * GQA_GROUP + g) * HEAD_DIM:(h * GQA_GROUP + g + 1) * HEAD_DIM,
                    n * BLOCK:(n + 1) * BLOCK] for g in range(GQA_GROUP)], axis=1)
        parts = [q4]
        if r0 > 0:
            parts.insert(0, jnp.zeros((r0, GQA_GROUP * BLOCK), BF16))
        if r0 + HEAD_DIM < KV_WIDTH:
            parts.append(jnp.zeros((KV_WIDTH - r0 - HEAD_DIM, GQA_GROUP * BLOCK), BF16))
        qpad = jnp.concatenate(parts, axis=0)
        kwin = kfull[n * BLOCK:n * BLOCK + span, :]
        return jnp.dot(kwin, qpad, preferred_element_type=F32)

    def attend(n, h, st_all):
        nglob = s_idx * r_blocks + n
        valid_prev = (key_i >= qry_i) & (nglob > 0)
        valid_next = (key_i <= qry_i) & (nglob < n_blocks - 1)
        pn, den = [], []
        for g in range(GQA_GROUP):
            cols = slice(g * BLOCK, (g + 1) * BLOCK)
            s_prev = jnp.where(valid_prev, st_all[0:BLOCK, cols], NEG_INF)
            s_self = st_all[BLOCK:2 * BLOCK, cols]
            s_next = jnp.where(valid_next, st_all[2 * BLOCK:, cols], NEG_INF)
            sink = jnp.full((1, BLOCK), sink_ref[h * GQA_GROUP + g] * LOG2_E, F32)
            m = jnp.max(jnp.maximum(jnp.maximum(s_prev, s_self), s_next), axis=0, keepdims=True)
            m = jnp.maximum(m, sink)
            p_prev = jnp.exp2(s_prev - m)
            p_self = jnp.exp2(s_self - m)
            p_next = jnp.exp2(s_next - m)
            den.append(jnp.sum(p_prev + p_self + p_next, axis=0, keepdims=True)
                       + jnp.exp2(sink - m))
            pn.append(jnp.concatenate([p_prev, p_self, p_next], axis=0).astype(BF16))
        r0 = h * HEAD_DIM
        vwin = vtfull[r0:r0 + HEAD_DIM, n * BLOCK:n * BLOCK + span]
        ot = jnp.dot(vwin, jnp.concatenate(pn, axis=1), preferred_element_type=F32)
        ot = ot * (1.0 / jnp.concatenate(den, axis=1))
        for g in range(GQA_GROUP):
            q0 = (h * GQA_GROUP + g) * HEAD_DIM
            ot_all[q0:q0 + HEAD_DIM, n * BLOCK:(n + 1) * BLOCK] = ot[:, g * BLOCK:(g + 1) * BLOCK]

    pairs = [(n, h) for n in range(r_blocks) for h in range(N_KV_HEADS)]
    pending = [scores(*pr) for pr in pairs[:QK_AHEAD]]
    for i, (n, h) in enumerate(pairs):
        if i + QK_AHEAD < len(pairs):
            pending.append(scores(*pairs[i + QK_AHEAD]))
        attend(n, h, pending.pop(0))

    y = ot_all[...]
    r = lax.rsqrt(jnp.mean(y * y, axis=0, keepdims=True) + EPS)
    o_ref[...] = ((y * r).T * g_ref[...]).astype(BF16)


def _window_attn(sink, qt, k, vt, g, tq):
    B, _, S = qt.shape
    n_blocks = S // BLOCK
    r_blocks = tq // BLOCK
    prev_blk = lambda s: jnp.maximum(s * r_blocks - 1, 0)
    next_blk = lambda s: jnp.minimum((s + 1) * r_blocks, n_blocks - 1)
    return pl.pallas_call(
        functools.partial(_window_attn_kernel, tq=tq, n_blocks=n_blocks),
        grid=(B, S // tq),
        in_specs=[
            pl.BlockSpec(memory_space=pltpu.SMEM),
            pl.BlockSpec((None, A_WIDTH, tq), lambda b, s: (b, 0, s)),
            pl.BlockSpec((None, tq, KV_WIDTH), lambda b, s: (b, s, 0)),
            pl.BlockSpec((None, BLOCK, KV_WIDTH), lambda b, s: (b, prev_blk(s), 0)),
            pl.BlockSpec((None, BLOCK, KV_WIDTH), lambda b, s: (b, next_blk(s), 0)),
            pl.BlockSpec((None, KV_WIDTH, tq), lambda b, s: (b, 0, s)),
            pl.BlockSpec((None, KV_WIDTH, BLOCK), lambda b, s: (b, 0, prev_blk(s))),
            pl.BlockSpec((None, KV_WIDTH, BLOCK), lambda b, s: (b, 0, next_blk(s))),
            pl.BlockSpec((1, A_WIDTH), lambda b, s: (0, 0)),
        ],
        out_specs=pl.BlockSpec((None, tq, A_WIDTH), lambda b, s: (b, s, 0)),
        out_shape=jax.ShapeDtypeStruct((B, S, A_WIDTH), BF16),
        scratch_shapes=[
            pltpu.VMEM((tq + 2 * BLOCK, KV_WIDTH), BF16),
            pltpu.VMEM((KV_WIDTH, tq + 2 * BLOCK), BF16),
            pltpu.VMEM((A_WIDTH, tq), F32),
        ],
        compiler_params=_params("parallel", "parallel"),
        name="window_attn",
    )(sink, qt, k, k, k, vt, vt, vt, g)


OUT_SUB = 2


def _out_proj_kernel(of_ref, oa_ref, x_ref, wof_ref, woa_ref, gof_ref, g_ref, x1_ref, h2_ref):
    sub = x_ref.shape[0] // OUT_SUB
    for i in range(OUT_SUB):
        rows = slice(i * sub, (i + 1) * sub)
        of = of_ref[rows, :].astype(F32)
        rf = lax.rsqrt(jnp.mean(of * of, axis=-1, keepdims=True) + EPS)
        of_n = ((of * rf) * gof_ref[...]).astype(BF16)
        y = jnp.dot(of_n, wof_ref[...], preferred_element_type=F32)
        y = y + jnp.dot(oa_ref[rows, :], woa_ref[...], preferred_element_type=F32)
        x1 = x_ref[rows, :] + y
        x1_ref[rows, :] = x1
        r = lax.rsqrt(jnp.mean(x1 * x1, axis=-1, keepdims=True) + EPS)
        h2_ref[rows, :] = ((x1 * r) * g_ref[...]).astype(BF16)


def _out_proj(of, oa_n, x, wof, woa, gof, g, tm):
    T = x.shape[0]
    return pl.pallas_call(
        _out_proj_kernel,
        grid=(T // tm,),
        in_specs=[
            pl.BlockSpec((tm, F_WIDTH), lambda i: (i, 0)),
            pl.BlockSpec((tm, A_WIDTH), lambda i: (i, 0)),
            pl.BlockSpec((tm, D_MODEL), lambda i: (i, 0)),
            _resident((F_WIDTH, D_MODEL)),
            _resident((A_WIDTH, D_MODEL)),
            _resident((1, F_WIDTH)),
            _resident((1, D_MODEL)),
        ],
        out_specs=[
            pl.BlockSpec((tm, D_MODEL), lambda i: (i, 0)),
            pl.BlockSpec((tm, D_MODEL), lambda i: (i, 0)),
        ],
        out_shape=[
            jax.ShapeDtypeStruct((T, D_MODEL), F32),
            jax.ShapeDtypeStruct((T, D_MODEL), BF16),
        ],
        compiler_params=_params("parallel"),
        name="out_proj",
    )(of, oa_n, x, wof, woa, gof, g)


MLP_CHUNKS = 2


def _mlp_kernel(h2_ref, x1_ref, wu_ref, wd_ref, g_ref, o_ref):
    j = pl.program_id(1)

    @pl.when(j == 0)
    def _():
        o_ref[...] = x1_ref[...]

    tf = wu_ref.shape[1]
    for c in range(MLP_CHUNKS):
        lo, hi = c * tf // MLP_CHUNKS, (c + 1) * tf // MLP_CHUNKS
        u = jnp.maximum(jnp.dot(h2_ref[...], wu_ref[:, lo:hi], preferred_element_type=F32), 0.0)
        o_ref[...] += jnp.dot((u * u).astype(BF16), wd_ref[lo:hi, :], preferred_element_type=F32)

    @pl.when(j == pl.num_programs(1) - 1)
    def _():
        y = o_ref[...]
        r = lax.rsqrt(jnp.mean(y * y, axis=-1, keepdims=True) + EPS)
        o_ref[...] = (y * r) * g_ref[...]


def _mlp(h2, x1, wu, wd, g, tm, tf):
    T = h2.shape[0]
    return pl.pallas_call(
        _mlp_kernel,
        grid=(T // tm, D_FF // tf),
        in_specs=[
            pl.BlockSpec((tm, D_MODEL), lambda i, j: (i, 0)),
            pl.BlockSpec((tm, D_MODEL), lambda i, j: (i, 0)),
            pl.BlockSpec((D_MODEL, tf), lambda i, j: (0, j)),
            pl.BlockSpec((tf, D_MODEL), lambda i, j: (j, 0)),
            _resident((1, D_MODEL)),
        ],
        out_specs=pl.BlockSpec((tm, D_MODEL), lambda i, j: (i, 0)),
        out_shape=jax.ShapeDtypeStruct((T, D_MODEL), F32),
        compiler_params=_params("parallel", "arbitrary"),
        name="mlp",
    )(h2, x1, wu, wd, g)


def _rope_tables(S):
    inv_freq = 1.0 / (ROPE_THETA ** (jnp.arange(0, HEAD_DIM, 2, dtype=F32) / HEAD_DIM))
    pos = jnp.arange(S, dtype=F32)
    ang = pos[:, None] * inv_freq[None, :]
    ang = jnp.concatenate([ang, ang], axis=-1)
    cos, sin = jnp.cos(ang), jnp.sin(ang)
    sign = jnp.where(jnp.arange(HEAD_DIM) < HALF_DIM, -1.0, 1.0).astype(F32)
    sin = sin * sign[None, :]
    reps = LANES // HEAD_DIM
    return cos.T, sin.T, jnp.tile(cos, (1, reps)), jnp.tile(sin, (1, reps))


def _seq_dft_matrices(S):
    half = S // 2
    lo_n = 64
    hi_n = half // lo_n
    k = jnp.arange(half, dtype=jnp.int32)
    w = 2.0 * math.pi / S
    a_hi = ((jnp.arange(hi_n + 1, dtype=jnp.int32)[:, None] * lo_n * k[None, :]) % S).astype(F32) * w
    a_lo = ((jnp.arange(lo_n, dtype=jnp.int32)[:, None] * k[None, :]) % S).astype(F32) * w
    ch, sh = jnp.cos(a_hi)[:, None, :], jnp.sin(a_hi)[:, None, :]
    cl, sl = jnp.cos(a_lo)[None, :, :], jnp.sin(a_lo)[None, :, :]
    scale = S ** -0.5
    cos = ((ch * cl - sh * sl) * scale).reshape(half + lo_n, half).astype(BF16)
    sin = ((sh[:hi_n] * cl + ch[:hi_n] * sl) * scale).reshape(half, half).astype(BF16)
    i = jnp.arange(REV, dtype=jnp.int32)
    jmat = ((i[:, None] + i[None, :]) == REV).astype(BF16)
    return cos, sin, jmat


def _trunk(x, w, tm_in, tq, dft_elems, tm_out, tm_mlp, tf):
    B, S, _ = x.shape
    cost, sint, cos, sin = _rope_tables(S)
    ab, qt, k, vt = _in_proj(x, w["g_mix"], w["wf"], w["wqt"], w["wk"], w["wvt"], w["mg"],
                             cost, sint, cos, sin, tm_in)
    of = _seq_dft(ab, *_seq_dft_matrices(S), n_col=F_WIDTH * S // dft_elems)
    oa_n = _window_attn(w["sink"], qt, k, vt, w["g_oa"], tq)
    T = B * S
    x2 = x.reshape(T, D_MODEL)
    x1, h2 = _out_proj(of.reshape(T, F_WIDTH), oa_n.reshape(T, A_WIDTH), x2,
                       w["wof"], w["woa"], w["g_of"], w["g_mlp"], tm_out)
    y = _mlp(h2, x1, w["wu"], w["wd"], w["g_final"], tm_mlp, tf)
    return y.reshape(B, S, D_MODEL)


def _prepare_weights(ln_mix_g, w_in, w_fourier, attn_sink, out_norm_fourier_g, out_norm_attn_g,
                     w_out, ln_mlp_g, w_up, w_down, ln_final_g):
    o1, o2, o3 = F_WIDTH, F_WIDTH + A_WIDTH, F_WIDTH + A_WIDTH + KV_WIDTH
    w_in_b = w_in.astype(BF16)
    w_out_b = w_out.astype(BF16)
    return {
        "g_mix": ln_mix_g.reshape(1, D_MODEL),
        "wf": w_in_b[:, :o1],
        "wqt": w_in_b[:, o1:o2].T,
        "wk": w_in_b[:, o2:o3],
        "wvt": w_in_b[:, o3:].T,
        "mg": _fourier_weights(w_fourier),
        "sink": attn_sink.astype(F32),
        "g_of": out_norm_fourier_g.reshape(1, F_WIDTH),
        "g_oa": out_norm_attn_g.reshape(1, A_WIDTH),
        "wof": w_out_b[:F_WIDTH],
        "woa": w_out_b[F_WIDTH:],
        "g_mlp": ln_mlp_g.reshape(1, D_MODEL),
        "wu": w_up.astype(BF16),
        "wd": w_down.astype(BF16),
        "g_final": ln_final_g.reshape(1, D_MODEL),
    }


def kernel(x_prompt, x_sample, ln_mix_g, w_in, w_fourier, attn_sink, out_norm_fourier_g,
           out_norm_attn_g, w_out, ln_mlp_g, w_up, w_down, ln_final_g):
    assert ln_mix_g.shape[0] == 1, "single-layer block"
    w = _prepare_weights(ln_mix_g[0], w_in[0], w_fourier[0], attn_sink[0], out_norm_fourier_g[0],
                         out_norm_attn_g[0], w_out[0], ln_mlp_g[0], w_up[0], w_down[0], ln_final_g)
    tiles = dict(tm_in=1024, tq=1024, dft_elems=4096 * 256, tm_out=512, tm_mlp=512, tf=2048)
    return (_trunk(x_prompt, w, **tiles), _trunk(x_sample, w, **tiles))
```

```python
import functools
import math

import jax
import jax.numpy as jnp
from jax import lax
from jax.experimental import pallas as pl
from jax.experimental.pallas import tpu as pltpu

D_MODEL = 2048
F_WIDTH = 1024
F_GROUPS = 8
F_CH = 128
A_WIDTH = 1024
HEAD_DIM = 64
HALF_DIM = HEAD_DIM // 2
N_Q_HEADS = 16
N_KV_HEADS = 4
GQA_GROUP = 4
KV_WIDTH = 256
WINDOW = 128
BLOCK = 128
ROPE_THETA = 10000.0
D_FF = 4 * D_MODEL
EPS = 1e-6
NEG_INF = -1e30
LOG2_E = math.log2(math.e)

LANES = 128
VMEM_LIMIT = 60 * 1024 * 1024

BF16 = jnp.bfloat16
F32 = jnp.float32

_NT = (((1,), (1,)), ((), ()))


def _params(*sem, flags=None):
    return pltpu.CompilerParams(dimension_semantics=sem, vmem_limit_bytes=VMEM_LIMIT, flags=flags)


def _resident(shape):
    nd = len(shape)
    return pl.BlockSpec(shape, lambda *_: (0,) * nd, pipeline_mode=pl.Buffered(1))


def _fourier_weights_kernel(cs_ref, w_ref, o_ref):
    w = w_ref[...]
    c = jnp.dot(cs_ref[0], w, preferred_element_type=F32, precision=lax.Precision.HIGHEST)
    s = jnp.dot(cs_ref[1], w, preferred_element_type=F32, precision=lax.Precision.HIGHEST)
    o_ref[:, :F_CH] = c.astype(BF16)
    o_ref[:, F_CH:] = s.astype(BF16)


def _fourier_weights(w_fourier):
    n = jnp.arange(F_CH, dtype=jnp.int32)
    ang = ((n[:, None] * n[None, :]) % F_CH).astype(F32) * (2.0 * math.pi / F_CH)
    cs = jnp.stack([jnp.cos(ang), jnp.sin(ang)]) * (F_CH ** -0.5)
    return pl.pallas_call(
        _fourier_weights_kernel,
        grid=(F_GROUPS,),
        in_specs=[pl.BlockSpec((2, F_CH, F_CH), lambda g: (0, 0, 0)),
                  pl.BlockSpec((None, F_CH, F_CH), lambda g: (g, 0, 0))],
        out_specs=pl.BlockSpec((None, F_CH, 2 * F_CH), lambda g: (g, 0, 0)),
        out_shape=jax.ShapeDtypeStruct((F_GROUPS, F_CH, 2 * F_CH), BF16),
        name="fourier_weights",
    )(cs, w_fourier)


IN_SUB = 4


def _in_proj_kernel(x_ref, g_ref, wf_ref, wqt_ref, wk_ref, wvt_ref, mg_ref,
                    cost_ref, sint_ref, cos_ref, sin_ref,
                    ab_ref, qt_ref, k_ref, vt_ref):
    tm = x_ref.shape[0]
    sub = tm // IN_SUB
    scale = HEAD_DIM ** -0.5 * LOG2_E
    first_half = (lax.broadcasted_iota(jnp.int32, (sub, LANES), 1) % HEAD_DIM) < HALF_DIM
    for i in range(IN_SUB):
        rows = slice(i * sub, (i + 1) * sub)
        x = x_ref[rows, :]
        r = lax.rsqrt(jnp.mean(x * x, axis=-1, keepdims=True) + EPS)
        h = ((x * r) * g_ref[...]).astype(BF16)

        zf = jnp.dot(h, wf_ref[...], preferred_element_type=F32).astype(BF16)
        qt = lax.dot_general(wqt_ref[...], h, _NT, preferred_element_type=F32)
        kk = jnp.dot(h, wk_ref[...], preferred_element_type=F32)
        vt = lax.dot_general(wvt_ref[...], h, _NT, preferred_element_type=F32)

        for g in range(F_GROUPS):
            lo = g * F_CH
            ab = jnp.dot(zf[:, lo:lo + F_CH], mg_ref[g], preferred_element_type=F32)
            ab_ref[0, rows, lo:lo + F_CH] = ab[:, :F_CH].astype(BF16)
            ab_ref[1, rows, lo:lo + F_CH] = ab[:, F_CH:].astype(BF16)

        cost = cost_ref[:, rows]
        sint = sint_ref[:, rows]
        for hd in range(N_Q_HEADS):
            lo = hd * HEAD_DIM
            blk = qt[lo:lo + HEAD_DIM]
            rot = jnp.concatenate([blk[HALF_DIM:], blk[:HALF_DIM]], axis=0)
            qt_ref[lo:lo + HEAD_DIM, rows] = ((blk * cost + rot * sint) * scale).astype(BF16)

        cos = cos_ref[rows, :]
        sin = sin_ref[rows, :]
        for c in range(KV_WIDTH // LANES):
            blk = kk[:, c * LANES:(c + 1) * LANES]
            rot = jnp.where(first_half,
                            pltpu.roll(blk, LANES - HALF_DIM, axis=1),
                            pltpu.roll(blk, HALF_DIM, axis=1))
            k_ref[rows, c * LANES:(c + 1) * LANES] = (blk * cos + rot * sin).astype(BF16)

        vt_ref[:, rows] = vt.astype(BF16)


def _in_proj(x, g, wf, wqt, wk, wvt, mg, cost, sint, cos, sin, tm):
    B, S, _ = x.shape
    return pl.pallas_call(
        _in_proj_kernel,
        grid=(B, S // tm),
        in_specs=[
            pl.BlockSpec((None, tm, D_MODEL), lambda b, s: (b, s, 0)),
            _resident((1, D_MODEL)),
            _resident((D_MODEL, F_WIDTH)),
            _resident((A_WIDTH, D_MODEL)),
            _resident((D_MODEL, KV_WIDTH)),
            _resident((KV_WIDTH, D_MODEL)),
            _resident((F_GROUPS, F_CH, 2 * F_CH)),
            pl.BlockSpec((HEAD_DIM, tm), lambda b, s: (0, s)),
            pl.BlockSpec((HEAD_DIM, tm), lambda b, s: (0, s)),
            pl.BlockSpec((tm, LANES), lambda b, s: (s, 0)),
            pl.BlockSpec((tm, LANES), lambda b, s: (s, 0)),
        ],
        out_specs=[
            pl.BlockSpec((None, 2, tm, F_WIDTH), lambda b, s: (b, 0, s, 0)),
            pl.BlockSpec((None, A_WIDTH, tm), lambda b, s: (b, 0, s)),
            pl.BlockSpec((None, tm, KV_WIDTH), lambda b, s: (b, s, 0)),
            pl.BlockSpec((None, KV_WIDTH, tm), lambda b, s: (b, 0, s)),
        ],
        out_shape=[
            jax.ShapeDtypeStruct((B, 2, S, F_WIDTH), BF16),
            jax.ShapeDtypeStruct((B, A_WIDTH, S), BF16),
            jax.ShapeDtypeStruct((B, S, KV_WIDTH), BF16),
            jax.ShapeDtypeStruct((B, KV_WIDTH, S), BF16),
        ],
        compiler_params=_params("parallel", "parallel"),
        name="in_proj",
    )(x, g, wf, wqt, wk, wvt, mg, cost, sint, cos, sin)


REV = 256
ROW_PACK = 16
DFT_ROWS = 512


def _first_row(x):
    return jnp.where(lax.broadcasted_iota(jnp.int32, x.shape, 0) == 0, x, 0.0)


def _patch_first_row(tile, row):
    head = tile[:ROW_PACK] + _first_row(row)
    return jnp.concatenate([head, tile[ROW_PACK:]], axis=0)


def _seq_dft_kernel(ab_ref, c_ref, s_ref, j_ref, o_ref, ap_ref, bm_ref, u_ref, *, seq):
    half = seq // 2
    n_rev = half // REV
    jmat = j_ref[...]
    a_ref = ab_ref.at[0]
    b_ref = ab_ref.at[1]

    for t in range(n_rev):
        lo = t * REV
        src = seq - lo - REV
        ra = jnp.dot(jmat, a_ref[src:src + REV, :], preferred_element_type=F32)
        rb = jnp.dot(jmat, b_ref[src:src + REV, :], preferred_element_type=F32)
        if t > 0:
            ra = _patch_first_row(ra, a_ref[seq - lo:seq - lo + ROW_PACK, :].astype(F32))
            rb = _patch_first_row(rb, b_ref[seq - lo:seq - lo + ROW_PACK, :].astype(F32))
        ap_ref[lo:lo + REV, :] = (a_ref[lo:lo + REV, :].astype(F32) + ra).astype(BF16)
        bm_ref[lo:lo + REV, :] = (b_ref[lo:lo + REV, :].astype(F32) - rb).astype(BF16)

    a_half = a_ref[half:half + ROW_PACK, :].astype(F32)[0:1, :] * (seq ** -0.5)
    row_par = lax.broadcasted_iota(jnp.int32, (DFT_ROWS, 1), 0) % 2
    sgn_a_half = jnp.where(row_par == 0, a_half, -a_half)

    ap = ap_ref[...]
    bm = bm_ref[...]
    n_row = half // DFT_ROWS
    nyq = None
    for i in range(n_row):
        lo = i * DFT_ROWS
        extra = ROW_PACK if i == n_row - 1 else 0
        p = jnp.dot(c_ref[lo:lo + DFT_ROWS + extra, :], ap, preferred_element_type=F32)
        if extra:
            nyq = p[DFT_ROWS:] + a_half
            p = p[:DFT_ROWS]
        p = p + sgn_a_half
        q = jnp.dot(s_ref[lo:lo + DFT_ROWS, :], bm, preferred_element_type=F32)
        o_ref[lo:lo + DFT_ROWS, :] = (p - q).astype(BF16)
        u_ref[lo:lo + DFT_ROWS, :] = (p + q).astype(BF16)

    for t in range(n_rev):
        src = half - (t + 1) * REV
        r = jnp.dot(jmat, u_ref[src:src + REV, :], preferred_element_type=F32)
        if t == 0:
            r = _patch_first_row(r, nyq)
        else:
            r = _patch_first_row(r, u_ref[src + REV:src + REV + ROW_PACK, :].astype(F32))
        o_ref[half + t * REV:half + (t + 1) * REV, :] = r.astype(BF16)


def _seq_dft(ab, cmat, smat, jmat, n_col):
    B, _, S, _ = ab.shape
    half = S // 2
    assert half % DFT_ROWS == 0 and half % 2 == 0
    tn = F_WIDTH // n_col
    return pl.pallas_call(
        functools.partial(_seq_dft_kernel, seq=S),
        grid=(B, n_col),
        in_specs=[
            pl.BlockSpec((None, 2, S, tn), lambda b, c: (b, 0, 0, c)),
            pl.BlockSpec((half + ROW_PACK, half), lambda b, c: (0, 0), pipeline_mode=pl.Buffered(1)),
            _resident((half, half)),
            _resident((REV, REV)),
        ],
        out_specs=pl.BlockSpec((None, S, tn), lambda b, c: (b, 0, c)),
        out_shape=jax.ShapeDtypeStruct((B, S, F_WIDTH), BF16),
        scratch_shapes=[pltpu.VMEM((half, tn), BF16),
                        pltpu.VMEM((half, tn), BF16),
                        pltpu.VMEM((half, tn), BF16)],
        compiler_params=_params("parallel", "parallel"),
        name="seq_dft",
    )(ab, cmat, smat, jmat)


QK_AHEAD = 1


def _window_attn_kernel(sink_ref, qt_ref, k_ref, kp_ref, kn_ref, vt_ref, vtp_ref, vtn_ref, g_ref,
                        o_ref, kfull, vtfull, *, tq, n_blocks):
    s_idx = pl.program_id(1)
    r_blocks = tq // BLOCK
    span = 3 * BLOCK

    kfull[0:BLOCK, :] = kp_ref[...]
    kfull[BLOCK:BLOCK + tq, :] = k_ref[...]
    kfull[BLOCK + tq:, :] = kn_ref[...]
    vtfull[:, 0:BLOCK] = vtp_ref[...]
    vtfull[:, BLOCK:BLOCK + tq] = vt_ref[...]
    vtfull[:, BLOCK + tq:] = vtn_ref[...]

    key_i = lax.broadcasted_iota(jnp.int32, (BLOCK, BLOCK), 0)
    qry_i = lax.broadcasted_iota(jnp.int32, (BLOCK, BLOCK), 1)
    assert WINDOW == BLOCK

    def scores(n, h):
        r0 = h * HEAD_DIM
        q4 = jnp.concatenate(
            [qt_ref[(h * GQA_GROUP + g) * HEAD_DIM:(h * GQA_GROUP + g + 1) * HEAD_DIM,
                    n * BLOCK:(n + 1) * BLOCK] for g in range(GQA_GROUP)], axis=1)
        parts = [q4]
        if r0 > 0:
            parts.insert(0, jnp.zeros((r0, GQA_GROUP * BLOCK), BF16))
        if r0 + HEAD_DIM < KV_WIDTH:
            parts.append(jnp.zeros((KV_WIDTH - r0 - HEAD_DIM, GQA_GROUP * BLOCK), BF16))
        qpad = jnp.concatenate(parts, axis=0)
        kwin = kfull[n * BLOCK:n * BLOCK + span, :]
        return jnp.dot(kwin, qpad, preferred_element_type=F32)

    def attend(n, h, st_all):
        nglob = s_idx * r_blocks + n
        valid_prev = (key_i >= qry_i) & (nglob > 0)
        valid_next = (key_i <= qry_i) & (nglob < n_blocks - 1)
        pn, tail = [], []
        for g in range(GQA_GROUP):
            cols = slice(g * BLOCK, (g + 1) * BLOCK)
            s_prev = jnp.where(valid_prev, st_all[0:BLOCK, cols], NEG_INF)
            s_self = st_all[BLOCK:2 * BLOCK, cols]
            s_next = jnp.where(valid_next, st_all[2 * BLOCK:, cols], NEG_INF)
            sink = jnp.full((1, BLOCK), sink_ref[h * GQA_GROUP + g] * LOG2_E, F32)
            m = jnp.max(jnp.maximum(jnp.maximum(s_prev, s_self), s_next), axis=0, keepdims=True)
            m = jnp.maximum(m, sink)
            tail.append(jnp.exp2(sink - m))
            pn.append(jnp.concatenate([jnp.exp2(s_prev - m), jnp.exp2(s_self - m),
                                       jnp.exp2(s_next - m)], axis=0).astype(BF16))
        r0 = h * HEAD_DIM
        vwin = vtfull[r0:r0 + HEAD_DIM, n * BLOCK:n * BLOCK + span]
        vext = jnp.concatenate([vwin, jnp.ones((ROW_PACK, span), BF16)], axis=0)
        ot = jnp.dot(vext, jnp.concatenate(pn, axis=1), preferred_element_type=F32)
        den = ot[HEAD_DIM:HEAD_DIM + 1] + jnp.concatenate(tail, axis=1)
        ot = ot[:HEAD_DIM] * (1.0 / den)
        return [ot[:, g * BLOCK:(g + 1) * BLOCK] for g in range(GQA_GROUP)]

    pairs = [(n, h) for n in range(r_blocks) for h in range(N_KV_HEADS)]
    pending = [scores(*pr) for pr in pairs[:QK_AHEAD]]
    heads = []
    for i, (n, h) in enumerate(pairs):
        if i + QK_AHEAD < len(pairs):
            pending.append(scores(*pairs[i + QK_AHEAD]))
        heads.extend(attend(n, h, pending.pop(0)))
        if h == N_KV_HEADS - 1:
            y = jnp.concatenate(heads, axis=0)
            heads = []
            r = lax.rsqrt(jnp.mean(y * y, axis=0, keepdims=True) + EPS)
            o_ref[n * BLOCK:(n + 1) * BLOCK, :] = ((y * r).T * g_ref[...]).astype(BF16)


def _window_attn(sink, qt, k, vt, g, tq):
    B, _, S = qt.shape
    n_blocks = S // BLOCK
    r_blocks = tq // BLOCK
    prev_blk = lambda s: jnp.maximum(s * r_blocks - 1, 0)
    next_blk = lambda s: jnp.minimum((s + 1) * r_blocks, n_blocks - 1)
    return pl.pallas_call(
        functools.partial(_window_attn_kernel, tq=tq, n_blocks=n_blocks),
        grid=(B, S // tq),
        in_specs=[
            pl.BlockSpec(memory_space=pltpu.SMEM),
            pl.BlockSpec((None, A_WIDTH, tq), lambda b, s: (b, 0, s)),
            pl.BlockSpec((None, tq, KV_WIDTH), lambda b, s: (b, s, 0)),
            pl.BlockSpec((None, BLOCK, KV_WIDTH), lambda b, s: (b, prev_blk(s), 0)),
            pl.BlockSpec((None, BLOCK, KV_WIDTH), lambda b, s: (b, next_blk(s), 0)),
            pl.BlockSpec((None, KV_WIDTH, tq), lambda b, s: (b, 0, s)),
            pl.BlockSpec((None, KV_WIDTH, BLOCK), lambda b, s: (b, 0, prev_blk(s))),
            pl.BlockSpec((None, KV_WIDTH, BLOCK), lambda b, s: (b, 0, next_blk(s))),
            pl.BlockSpec((1, A_WIDTH), lambda b, s: (0, 0)),
        ],
        out_specs=pl.BlockSpec((None, tq, A_WIDTH), lambda b, s: (b, s, 0)),
        out_shape=jax.ShapeDtypeStruct((B, S, A_WIDTH), BF16),
        scratch_shapes=[
            pltpu.VMEM((tq + 2 * BLOCK, KV_WIDTH), BF16),
            pltpu.VMEM((KV_WIDTH, tq + 2 * BLOCK), BF16),
        ],
        compiler_params=_params("parallel", "parallel"),
        name="window_attn",
    )(sink, qt, k, k, k, vt, vt, vt, g)


OUT_SUB = 2


def _out_proj_kernel(of_ref, oa_ref, x_ref, wof_ref, woa_ref, gof_ref, g_ref, x1_ref, h2_ref):
    sub = x_ref.shape[0] // OUT_SUB
    for i in range(OUT_SUB):
        rows = slice(i * sub, (i + 1) * sub)
        of = of_ref[rows, :].astype(F32)
        rf = lax.rsqrt(jnp.mean(of * of, axis=-1, keepdims=True) + EPS)
        of_n = ((of * rf) * gof_ref[...]).astype(BF16)
        y = jnp.dot(of_n, wof_ref[...], preferred_element_type=F32)
        y = y + jnp.dot(oa_ref[rows, :], woa_ref[...], preferred_element_type=F32)
        x1 = x_ref[rows, :] + y
        x1_ref[rows, :] = x1
        r = lax.rsqrt(jnp.mean(x1 * x1, axis=-1, keepdims=True) + EPS)
        h2_ref[rows, :] = ((x1 * r) * g_ref[...]).astype(BF16)


def _out_proj(of, oa_n, x, wof, woa, gof, g, tm):
    T = x.shape[0]
    return pl.pallas_call(
        _out_proj_kernel,
        grid=(T // tm,),
        in_specs=[
            pl.BlockSpec((tm, F_WIDTH), lambda i: (i, 0)),
            pl.BlockSpec((tm, A_WIDTH), lambda i: (i, 0)),
            pl.BlockSpec((tm, D_MODEL), lambda i: (i, 0)),
            _resident((F_WIDTH, D_MODEL)),
            _resident((A_WIDTH, D_MODEL)),
            _resident((1, F_WIDTH)),
            _resident((1, D_MODEL)),
        ],
        out_specs=[
            pl.BlockSpec((tm, D_MODEL), lambda i: (i, 0)),
            pl.BlockSpec((tm, D_MODEL), lambda i: (i, 0)),
        ],
        out_shape=[
            jax.ShapeDtypeStruct((T, D_MODEL), F32),
            jax.ShapeDtypeStruct((T, D_MODEL), BF16),
        ],
        compiler_params=_params("parallel"),
        name="out_proj",
    )(of, oa_n, x, wof, woa, gof, g)


MLP_CHUNKS = 2


def _mlp_kernel(h2_ref, x1_ref, wu_ref, wd_ref, g_ref, o_ref):
    j = pl.program_id(1)

    @pl.when(j == 0)
    def _():
        o_ref[...] = x1_ref[...]

    tf = wu_ref.shape[1]
    for c in range(MLP_CHUNKS):
        lo, hi = c * tf // MLP_CHUNKS, (c + 1) * tf // MLP_CHUNKS
        u = jnp.maximum(jnp.dot(h2_ref[...], wu_ref[:, lo:hi], preferred_element_type=F32), 0.0)
        o_ref[...] += jnp.dot((u * u).astype(BF16), wd_ref[lo:hi, :], preferred_element_type=F32)

    @pl.when(j == pl.num_programs(1) - 1)
    def _():
        y = o_ref[...]
        r = lax.rsqrt(jnp.mean(y * y, axis=-1, keepdims=True) + EPS)
        o_ref[...] = (y * r) * g_ref[...]


def _mlp(h2, x1, wu, wd, g, tm, tf):
    T = h2.shape[0]
    return pl.pallas_call(
        _mlp_kernel,
        grid=(T // tm, D_FF // tf),
        in_specs=[
            pl.BlockSpec((tm, D_MODEL), lambda i, j: (i, 0)),
            pl.BlockSpec((tm, D_MODEL), lambda i, j: (i, 0)),
            pl.BlockSpec((D_MODEL, tf), lambda i, j: (0, j)),
            pl.BlockSpec((tf, D_MODEL), lambda i, j: (j, 0)),
            _resident((1, D_MODEL)),
        ],
        out_specs=pl.BlockSpec((tm, D_MODEL), lambda i, j: (i, 0)),
        out_shape=jax.ShapeDtypeStruct((T, D_MODEL), F32),
        compiler_params=_params("parallel", "arbitrary"),
        name="mlp",
    )(h2, x1, wu, wd, g)


def _rope_tables(S):
    inv_freq = 1.0 / (ROPE_THETA ** (jnp.arange(0, HEAD_DIM, 2, dtype=F32) / HEAD_DIM))
    pos = jnp.arange(S, dtype=F32)
    ang = pos[:, None] * inv_freq[None, :]
    ang = jnp.concatenate([ang, ang], axis=-1)
    cos, sin = jnp.cos(ang), jnp.sin(ang)
    sign = jnp.where(jnp.arange(HEAD_DIM) < HALF_DIM, -1.0, 1.0).astype(F32)
    sin = sin * sign[None, :]
    reps = LANES // HEAD_DIM
    return cos.T, sin.T, jnp.tile(cos, (1, reps)), jnp.tile(sin, (1, reps))


def _seq_dft_matrices(S):
    half = S // 2
    lo_n = 64
    hi_n = half // lo_n
    k = jnp.arange(half, dtype=jnp.int32)
    w = 2.0 * math.pi / S
    a_hi = ((jnp.arange(hi_n + 1, dtype=jnp.int32)[:, None] * lo_n * k[None, :]) % S).astype(F32) * w
    a_lo = ((jnp.arange(lo_n, dtype=jnp.int32)[:, None] * k[None, :]) % S).astype(F32) * w
    ch, sh = jnp.cos(a_hi)[:, None, :], jnp.sin(a_hi)[:, None, :]
    cl, sl = jnp.cos(a_lo)[None, :, :], jnp.sin(a_lo)[None, :, :]
    scale = S ** -0.5
    cos = ((ch * cl - sh * sl) * scale).reshape(half + lo_n, half).astype(BF16)
    sin = ((sh[:hi_n] * cl + ch[:hi_n] * sl) * scale).reshape(half, half).astype(BF16)
    i = jnp.arange(REV, dtype=jnp.int32)
    jmat = ((i[:, None] + i[None, :]) == REV).astype(BF16)
    return cos, sin, jmat


def _trunk(x, w, tm_in, tq, dft_elems, tm_out, tm_mlp, tf):
    B, S, _ = x.shape
    cost, sint, cos, sin = _rope_tables(S)
    ab, qt, k, vt = _in_proj(x, w["g_mix"], w["wf"], w["wqt"], w["wk"], w["wvt"], w["mg"],
                             cost, sint, cos, sin, tm_in)
    of = _seq_dft(ab, *_seq_dft_matrices(S), n_col=F_WIDTH * S // dft_elems)
    oa_n = _window_attn(w["sink"], qt, k, vt, w["g_oa"], tq)
    T = B * S
    x2 = x.reshape(T, D_MODEL)
    x1, h2 = _out_proj(of.reshape(T, F_WIDTH), oa_n.reshape(T, A_WIDTH), x2,
                       w["wof"], w["woa"], w["g_of"], w["g_mlp"], tm_out)
    y = _mlp(h2, x1, w["wu"], w["wd"], w["g_final"], tm_mlp, tf)
    return y.reshape(B, S, D_MODEL)


def _prepare_weights(ln_mix_g, w_in, w_fourier, attn_sink, out_norm_fourier_g, out_norm_attn_g,
                     w_out, ln_mlp_g, w_up, w_down, ln_final_g):
    o1, o2, o3 = F_WIDTH, F_WIDTH + A_WIDTH, F_WIDTH + A_WIDTH + KV_WIDTH
    w_in_b = w_in.astype(BF16)
    w_out_b = w_out.astype(BF16)
    return {
        "g_mix": ln_mix_g.reshape(1, D_MODEL),
        "wf": w_in_b[:, :o1],
        "wqt": w_in_b[:, o1:o2].T,
        "wk": w_in_b[:, o2:o3],
        "wvt": w_in_b[:, o3:].T,
        "mg": _fourier_weights(w_fourier),
        "sink": attn_sink.astype(F32),
        "g_of": out_norm_fourier_g.reshape(1, F_WIDTH),
        "g_oa": out_norm_attn_g.reshape(1, A_WIDTH),
        "wof": w_out_b[:F_WIDTH],
        "woa": w_out_b[F_WIDTH:],
        "g_mlp": ln_mlp_g.reshape(1, D_MODEL),
        "wu": w_up.astype(BF16),
        "wd": w_down.astype(BF16),
        "g_final": ln_final_g.reshape(1, D_MODEL),
    }


def kernel(x_prompt, x_sample, ln_mix_g, w_in, w_fourier, attn_sink, out_norm_fourier_g,
           out_norm_attn_g, w_out, ln_mlp_g, w_up, w_down, ln_final_g):
    assert ln_mix_g.shape[0] == 1, "single-layer block"
    w = _prepare_weights(ln_mix_g[0], w_in[0], w_fourier[0], attn_sink[0], out_norm_fourier_g[0],
                         out_norm_attn_g[0], w_out[0], ln_mlp_g[0], w_up[0], w_down[0], ln_final_g)
    tiles = dict(tm_in=1024, tq=1024, dft_elems=4096 * 256, tm_out=512, tm_mlp=512, tf=2048)
    return (_trunk(x_prompt, w, **tiles), _trunk(x_sample, w, **tiles))
```

```python
import functools
import math

import jax
import jax.numpy as jnp
from jax import lax
from jax.experimental import pallas as pl
from jax.experimental.pallas import tpu as pltpu

D_MODEL = 2048
F_WIDTH = 1024
F_GROUPS = 8
F_CH = 128
A_WIDTH = 1024
HEAD_DIM = 64
HALF_DIM = HEAD_DIM // 2
N_Q_HEADS = 16
N_KV_HEADS = 4
GQA_GROUP = 4
KV_WIDTH = 256
WINDOW = 128
BLOCK = 128
ROPE_THETA = 10000.0
D_FF = 4 * D_MODEL
EPS = 1e-6
NEG_INF = -1e30
LOG2_E = math.log2(math.e)

LANES = 128
VMEM_LIMIT = 60 * 1024 * 1024

BF16 = jnp.bfloat16
F32 = jnp.float32

_NT = (((1,), (1,)), ((), ()))


def _params(*sem, flags=None):
    return pltpu.CompilerParams(dimension_semantics=sem, vmem_limit_bytes=VMEM_LIMIT, flags=flags)


def _resident(shape):
    nd = len(shape)
    return pl.BlockSpec(shape, lambda *_: (0,) * nd, pipeline_mode=pl.Buffered(1))


def _fourier_weights_kernel(cs_ref, w_ref, o_ref):
    for g in range(F_GROUPS):
        w = w_ref[g]
        c = jnp.dot(cs_ref[0], w, preferred_element_type=F32, precision=lax.Precision.HIGHEST)
        s = jnp.dot(cs_ref[1], w, preferred_element_type=F32, precision=lax.Precision.HIGHEST)
        o_ref[g, :, :F_CH] = c.astype(BF16)
        o_ref[g, :, F_CH:] = s.astype(BF16)


def _fourier_weights(w_fourier):
    n = jnp.arange(F_CH, dtype=jnp.int32)
    ang = ((n[:, None] * n[None, :]) % F_CH).astype(F32) * (2.0 * math.pi / F_CH)
    cs = jnp.stack([jnp.cos(ang), jnp.sin(ang)]) * (F_CH ** -0.5)
    return pl.pallas_call(
        _fourier_weights_kernel,
        out_shape=jax.ShapeDtypeStruct((F_GROUPS, F_CH, 2 * F_CH), BF16),
        name="fourier_weights",
    )(cs, w_fourier)


IN_SUB = 4


def _in_proj_kernel(x_ref, g_ref, wf_ref, wqt_ref, wk_ref, wvt_ref, mg_ref,
                    cost_ref, sint_ref, cos_ref, sin_ref,
                    ab_ref, qt_ref, k_ref, vt_ref):
    tm = x_ref.shape[0]
    sub = tm // IN_SUB
    scale = HEAD_DIM ** -0.5 * LOG2_E
    first_half = (lax.broadcasted_iota(jnp.int32, (sub, LANES), 1) % HEAD_DIM) < HALF_DIM
    for i in range(IN_SUB):
        rows = slice(i * sub, (i + 1) * sub)
        x = x_ref[rows, :]
        r = lax.rsqrt(jnp.mean(x * x, axis=-1, keepdims=True) + EPS)
        h = ((x * r) * g_ref[...]).astype(BF16)

        zf = jnp.dot(h, wf_ref[...], preferred_element_type=F32).astype(BF16)
        qt = lax.dot_general(wqt_ref[...], h, _NT, preferred_element_type=F32)
        kk = jnp.dot(h, wk_ref[...], preferred_element_type=F32)
        vt = lax.dot_general(wvt_ref[...], h, _NT, preferred_element_type=F32)

        for g in range(F_GROUPS):
            lo = g * F_CH
            ab = jnp.dot(zf[:, lo:lo + F_CH], mg_ref[g], preferred_element_type=F32)
            ab_ref[0, rows, lo:lo + F_CH] = ab[:, :F_CH].astype(BF16)
            ab_ref[1, rows, lo:lo + F_CH] = ab[:, F_CH:].astype(BF16)

        cost = cost_ref[:, rows]
        sint = sint_ref[:, rows]
        for hd in range(N_Q_HEADS):
            lo = hd * HEAD_DIM
            blk = qt[lo:lo + HEAD_DIM]
            rot = jnp.concatenate([blk[HALF_DIM:], blk[:HALF_DIM]], axis=0)
            qt_ref[lo:lo + HEAD_DIM, rows] = ((blk * cost + rot * sint) * scale).astype(BF16)

        cos = cos_ref[rows, :]
        sin = sin_ref[rows, :]
        for c in range(KV_WIDTH // LANES):
            blk = kk[:, c * LANES:(c + 1) * LANES]
            rot = jnp.where(first_half,
                            pltpu.roll(blk, LANES - HALF_DIM, axis=1),
                            pltpu.roll(blk, HALF_DIM, axis=1))
            k_ref[rows, c * LANES:(c + 1) * LANES] = (blk * cos + rot * sin).astype(BF16)

        vt_ref[:, rows] = vt.astype(BF16)


def _in_proj(x, g, wf, wqt, wk, wvt, mg, cost, sint, cos, sin, tm):
    B, S, _ = x.shape
    return pl.pallas_call(
        _in_proj_kernel,
        grid=(B, S // tm),
        in_specs=[
            pl.BlockSpec((None, tm, D_MODEL), lambda b, s: (b, s, 0)),
            _resident((1, D_MODEL)),
            _resident((D_MODEL, F_WIDTH)),
            _resident((A_WIDTH, D_MODEL)),
            _resident((D_MODEL, KV_WIDTH)),
            _resident((KV_WIDTH, D_MODEL)),
            _resident((F_GROUPS, F_CH, 2 * F_CH)),
            pl.BlockSpec((HEAD_DIM, tm), lambda b, s: (0, s)),
            pl.BlockSpec((HEAD_DIM, tm), lambda b, s: (0, s)),
            pl.BlockSpec((tm, LANES), lambda b, s: (s, 0)),
            pl.BlockSpec((tm, LANES), lambda b, s: (s, 0)),
        ],
        out_specs=[
            pl.BlockSpec((None, 2, tm, F_WIDTH), lambda b, s: (b, 0, s, 0)),
            pl.BlockSpec((None, A_WIDTH, tm), lambda b, s: (b, 0, s)),
            pl.BlockSpec((None, tm, KV_WIDTH), lambda b, s: (b, s, 0)),
            pl.BlockSpec((None, KV_WIDTH, tm), lambda b, s: (b, 0, s)),
        ],
        out_shape=[
            jax.ShapeDtypeStruct((B, 2, S, F_WIDTH), BF16),
            jax.ShapeDtypeStruct((B, A_WIDTH, S), BF16),
            jax.ShapeDtypeStruct((B, S, KV_WIDTH), BF16),
            jax.ShapeDtypeStruct((B, KV_WIDTH, S), BF16),
        ],
        compiler_params=_params("parallel", "parallel"),
        name="in_proj",
    )(x, g, wf, wqt, wk, wvt, mg, cost, sint, cos, sin)


REV = 256
ROW_PACK = 16
DFT_ROWS = 512


def _first_row(x):
    return jnp.where(lax.broadcasted_iota(jnp.int32, x.shape, 0) == 0, x, 0.0)


def _patch_first_row(tile, row):
    head = tile[:ROW_PACK] + _first_row(row)
    return jnp.concatenate([head, tile[ROW_PACK:]], axis=0)


def _seq_dft_kernel(ab_ref, c_ref, s_ref, j_ref, o_ref, ap_ref, bm_ref, u_ref, *, seq):
    half = seq // 2
    n_rev = half // REV
    jmat = j_ref[...]
    a_ref = ab_ref.at[0]
    b_ref = ab_ref.at[1]

    for t in range(n_rev):
        lo = t * REV
        src = seq - lo - REV
        ra = jnp.dot(jmat, a_ref[src:src + REV, :], preferred_element_type=F32)
        rb = jnp.dot(jmat, b_ref[src:src + REV, :], preferred_element_type=F32)
        if t > 0:
            ra = _patch_first_row(ra, a_ref[seq - lo:seq - lo + ROW_PACK, :].astype(F32))
            rb = _patch_first_row(rb, b_ref[seq - lo:seq - lo + ROW_PACK, :].astype(F32))
        ap_ref[lo:lo + REV, :] = (a_ref[lo:lo + REV, :].astype(F32) + ra).astype(BF16)
        bm_ref[lo:lo + REV, :] = (b_ref[lo:lo + REV, :].astype(F32) - rb).astype(BF16)

    a_half = a_ref[half:half + ROW_PACK, :].astype(F32)[0:1, :] * (seq ** -0.5)
    row_par = lax.broadcasted_iota(jnp.int32, (DFT_ROWS, 1), 0) % 2
    sgn_a_half = jnp.where(row_par == 0, a_half, -a_half)

    ap = ap_ref[...]
    bm = bm_ref[...]
    n_row = half // DFT_ROWS
    nyq = None
    for i in range(n_row):
        lo = i * DFT_ROWS
        extra = ROW_PACK if i == n_row - 1 else 0
        p = jnp.dot(c_ref[lo:lo + DFT_ROWS + extra, :], ap, preferred_element_type=F32)
        if extra:
            nyq = p[DFT_ROWS:] + a_half
            p = p[:DFT_ROWS]
        p = p + sgn_a_half
        q = jnp.dot(s_ref[lo:lo + DFT_ROWS, :], bm, preferred_element_type=F32)
        o_ref[lo:lo + DFT_ROWS, :] = (p - q).astype(BF16)
        u_ref[lo:lo + DFT_ROWS, :] = (p + q).astype(BF16)

    for t in range(n_rev):
        src = half - (t + 1) * REV
        r = jnp.dot(jmat, u_ref[src:src + REV, :], preferred_element_type=F32)
        if t == 0:
            r = _patch_first_row(r, nyq)
        else:
            r = _patch_first_row(r, u_ref[src + REV:src + REV + ROW_PACK, :].astype(F32))
        o_ref[half + t * REV:half + (t + 1) * REV, :] = r.astype(BF16)


def _seq_dft(ab, cmat, smat, jmat, n_col):
    B, _, S, _ = ab.shape
    half = S // 2
    assert half % DFT_ROWS == 0 and half % 2 == 0
    tn = F_WIDTH // n_col
    return pl.pallas_call(
        functools.partial(_seq_dft_kernel, seq=S),
        grid=(B, n_col),
        in_specs=[
            pl.BlockSpec((None, 2, S, tn), lambda b, c: (b, 0, 0, c)),
            pl.BlockSpec((half + ROW_PACK, half), lambda b, c: (0, 0), pipeline_mode=pl.Buffered(1)),
            _resident((half, half)),
            _resident((REV, REV)),
        ],
        out_specs=pl.BlockSpec((None, S, tn), lambda b, c: (b, 0, c)),
        out_shape=jax.ShapeDtypeStruct((B, S, F_WIDTH), BF16),
        scratch_shapes=[pltpu.VMEM((half, tn), BF16),
                        pltpu.VMEM((half, tn), BF16),
                        pltpu.VMEM((half, tn), BF16)],
        compiler_params=_params("parallel", "parallel"),
        name="seq_dft",
    )(ab, cmat, smat, jmat)


QK_AHEAD = 1


def _window_attn_kernel(sink_ref, qt_ref, k_ref, kp_ref, kn_ref, vt_ref, vtp_ref, vtn_ref, g_ref,
                        o_ref, kfull, vtfull, *, tq, n_blocks):
    s_idx = pl.program_id(1)
    r_blocks = tq // BLOCK
    span = 3 * BLOCK

    kfull[0:BLOCK, :] = kp_ref[...]
    kfull[BLOCK:BLOCK + tq, :] = k_ref[...]
    kfull[BLOCK + tq:, :] = kn_ref[...]
    vtfull[:, 0:BLOCK] = vtp_ref[...]
    vtfull[:, BLOCK:BLOCK + tq] = vt_ref[...]
    vtfull[:, BLOCK + tq:] = vtn_ref[...]

    key_i = lax.broadcasted_iota(jnp.int32, (BLOCK, BLOCK), 0)
    qry_i = lax.broadcasted_iota(jnp.int32, (BLOCK, BLOCK), 1)
    assert WINDOW == BLOCK

    def scores(n, h):
        r0 = h * HEAD_DIM
        q4 = jnp.concatenate(
            [qt_ref[(h * GQA_GROUP + g) * HEAD_DIM:(h * GQA_GROUP + g + 1) * HEAD_DIM,
                    n * BLOCK:(n + 1) * BLOCK] for g in range(GQA_GROUP)], axis=1)
        parts = [q4]
        if r0 > 0:
            parts.insert(0, jnp.zeros((r0, GQA_GROUP * BLOCK), BF16))
        if r0 + HEAD_DIM < KV_WIDTH:
            parts.append(jnp.zeros((KV_WIDTH - r0 - HEAD_DIM, GQA_GROUP * BLOCK), BF16))
        qpad = jnp.concatenate(parts, axis=0)
        kwin = kfull[n * BLOCK:n * BLOCK + span, :]
        return jnp.dot(kwin, qpad, preferred_element_type=F32)

    def attend(n, h, st_all):
        nglob = s_idx * r_blocks + n
        valid_prev = (key_i >= qry_i) & (nglob > 0)
        valid_next = (key_i <= qry_i) & (nglob < n_blocks - 1)
        pn, tail = [], []
        for g in range(GQA_GROUP):
            cols = slice(g * BLOCK, (g + 1) * BLOCK)
            s_prev = jnp.where(valid_prev, st_all[0:BLOCK, cols], NEG_INF)
            s_self = st_all[BLOCK:2 * BLOCK, cols]
            s_next = jnp.where(valid_next, st_all[2 * BLOCK:, cols], NEG_INF)
            sink = jnp.full((1, BLOCK), sink_ref[h * GQA_GROUP + g] * LOG2_E, F32)
            m = jnp.max(jnp.maximum(jnp.maximum(s_prev, s_self), s_next), axis=0, keepdims=True)
            m = jnp.maximum(m, sink)
            tail.append(jnp.exp2(sink - m))
            pn.append(jnp.concatenate([jnp.exp2(s_prev - m), jnp.exp2(s_self - m),
                                       jnp.exp2(s_next - m)], axis=0).astype(BF16))
        r0 = h * HEAD_DIM
        vwin = vtfull[r0:r0 + HEAD_DIM, n * BLOCK:n * BLOCK + span]
        vext = jnp.concatenate([vwin, jnp.ones((ROW_PACK, span), BF16)], axis=0)
        ot = jnp.dot(vext, jnp.concatenate(pn, axis=1), preferred_element_type=F32)
        den = ot[HEAD_DIM:HEAD_DIM + 1] + jnp.concatenate(tail, axis=1)
        ot = ot[:HEAD_DIM] * (1.0 / den)
        return [ot[:, g * BLOCK:(g + 1) * BLOCK] for g in range(GQA_GROUP)]

    pairs = [(n, h) for n in range(r_blocks) for h in range(N_KV_HEADS)]
    pending = [scores(*pr) for pr in pairs[:QK_AHEAD]]
    heads = []
    for i, (n, h) in enumerate(pairs):
        if i + QK_AHEAD < len(pairs):
            pending.append(scores(*pairs[i + QK_AHEAD]))
        heads.extend(attend(n, h, pending.pop(0)))
        if h == N_KV_HEADS - 1:
            y = jnp.concatenate(heads, axis=0)
            heads = []
            r = lax.rsqrt(jnp.mean(y * y, axis=0, keepdims=True) + EPS)
            o_ref[n * BLOCK:(n + 1) * BLOCK, :] = ((y * r).T * g_ref[...]).astype(BF16)


def _window_attn(sink, qt, k, vt, g, tq):
    B, _, S = qt.shape
    n_blocks = S // BLOCK
    r_blocks = tq // BLOCK
    prev_blk = lambda s: jnp.maximum(s * r_blocks - 1, 0)
    next_blk = lambda s: jnp.minimum((s + 1) * r_blocks, n_blocks - 1)
    return pl.pallas_call(
        functools.partial(_window_attn_kernel, tq=tq, n_blocks=n_blocks),
        grid=(B, S // tq),
        in_specs=[
            pl.BlockSpec(memory_space=pltpu.SMEM),
            pl.BlockSpec((None, A_WIDTH, tq), lambda b, s: (b, 0, s)),
            pl.BlockSpec((None, tq, KV_WIDTH), lambda b, s: (b, s, 0)),
            pl.BlockSpec((None, BLOCK, KV_WIDTH), lambda b, s: (b, prev_blk(s), 0)),
            pl.BlockSpec((None, BLOCK, KV_WIDTH), lambda b, s: (b, next_blk(s), 0)),
            pl.BlockSpec((None, KV_WIDTH, tq), lambda b, s: (b, 0, s)),
            pl.BlockSpec((None, KV_WIDTH, BLOCK), lambda b, s: (b, 0, prev_blk(s))),
            pl.BlockSpec((None, KV_WIDTH, BLOCK), lambda b, s: (b, 0, next_blk(s))),
            pl.BlockSpec((1, A_WIDTH), lambda b, s: (0, 0)),
        ],
        out_specs=pl.BlockSpec((None, tq, A_WIDTH), lambda b, s: (b, s, 0)),
        out_shape=jax.ShapeDtypeStruct((B, S, A_WIDTH), BF16),
        scratch_shapes=[
            pltpu.VMEM((tq + 2 * BLOCK, KV_WIDTH), BF16),
            pltpu.VMEM((KV_WIDTH, tq + 2 * BLOCK), BF16),
        ],
        compiler_params=_params("parallel", "parallel"),
        name="window_attn",
    )(sink, qt, k, k, k, vt, vt, vt, g)


OUT_SUB = 2


def _out_proj_kernel(of_ref, oa_ref, x_ref, wof_ref, woa_ref, gof_ref, g_ref, x1_ref, h2_ref):
    sub = x_ref.shape[0] // OUT_SUB
    for i in range(OUT_SUB):
        rows = slice(i * sub, (i + 1) * sub)
        of = of_ref[rows, :].astype(F32)
        rf = lax.rsqrt(jnp.mean(of * of, axis=-1, keepdims=True) + EPS)
        of_n = ((of * rf) * gof_ref[...]).astype(BF16)
        y = jnp.dot(of_n, wof_ref[...], preferred_element_type=F32)
        y = y + jnp.dot(oa_ref[rows, :], woa_ref[...], preferred_element_type=F32)
        x1 = x_ref[rows, :] + y
        x1_ref[rows, :] = x1
        r = lax.rsqrt(jnp.mean(x1 * x1, axis=-1, keepdims=True) + EPS)
        h2_ref[rows, :] = ((x1 * r) * g_ref[...]).astype(BF16)


def _out_proj(of, oa_n, x, wof, woa, gof, g, tm):
    T = x.shape[0]
    return pl.pallas_call(
        _out_proj_kernel,
        grid=(T // tm,),
        in_specs=[
            pl.BlockSpec((tm, F_WIDTH), lambda i: (i, 0)),
            pl.BlockSpec((tm, A_WIDTH), lambda i: (i, 0)),
            pl.BlockSpec((tm, D_MODEL), lambda i: (i, 0)),
            _resident((F_WIDTH, D_MODEL)),
            _resident((A_WIDTH, D_MODEL)),
            _resident((1, F_WIDTH)),
            _resident((1, D_MODEL)),
        ],
        out_specs=[
            pl.BlockSpec((tm, D_MODEL), lambda i: (i, 0)),
            pl.BlockSpec((tm, D_MODEL), lambda i: (i, 0)),
        ],
        out_shape=[
            jax.ShapeDtypeStruct((T, D_MODEL), F32),
            jax.ShapeDtypeStruct((T, D_MODEL), BF16),
        ],
        compiler_params=_params("parallel"),
        name="out_proj",
    )(of, oa_n, x, wof, woa, gof, g)


MLP_CHUNKS = 2


def _mlp_kernel(h2_ref, x1_ref, wu_ref, wd_ref, g_ref, o_ref):
    j = pl.program_id(1)

    @pl.when(j == 0)
    def _():
        o_ref[...] = x1_ref[...]

    tf = wu_ref.shape[1]
    for c in range(MLP_CHUNKS):
        lo, hi = c * tf // MLP_CHUNKS, (c + 1) * tf // MLP_CHUNKS
        u = jnp.maximum(jnp.dot(h2_ref[...], wu_ref[:, lo:hi], preferred_element_type=F32), 0.0)
        o_ref[...] += jnp.dot((u * u).astype(BF16), wd_ref[lo:hi, :], preferred_element_type=F32)

    @pl.when(j == pl.num_programs(1) - 1)
    def _():
        y = o_ref[...]
        r = lax.rsqrt(jnp.mean(y * y, axis=-1, keepdims=True) + EPS)
        o_ref[...] = (y * r) * g_ref[...]


def _mlp(h2, x1, wu, wd, g, tm, tf):
    T = h2.shape[0]
    return pl.pallas_call(
        _mlp_kernel,
        grid=(T // tm, D_FF // tf),
        in_specs=[
            pl.BlockSpec((tm, D_MODEL), lambda i, j: (i, 0)),
            pl.BlockSpec((tm, D_MODEL), lambda i, j: (i, 0)),
            pl.BlockSpec((D_MODEL, tf), lambda i, j: (0, j)),
            pl.BlockSpec((tf, D_MODEL), lambda i, j: (j, 0)),
            _resident((1, D_MODEL)),
        ],
        out_specs=pl.BlockSpec((tm, D_MODEL), lambda i, j: (i, 0)),
        out_shape=jax.ShapeDtypeStruct((T, D_MODEL), F32),
        compiler_params=_params("parallel", "arbitrary"),
        name="mlp",
    )(h2, x1, wu, wd, g)


def _rope_tables(S):
    inv_freq = 1.0 / (ROPE_THETA ** (jnp.arange(0, HEAD_DIM, 2, dtype=F32) / HEAD_DIM))
    pos = jnp.arange(S, dtype=F32)
    ang = pos[:, None] * inv_freq[None, :]
    ang = jnp.concatenate([ang, ang], axis=-1)
    cos, sin = jnp.cos(ang), jnp.sin(ang)
    sign = jnp.where(jnp.arange(HEAD_DIM) < HALF_DIM, -1.0, 1.0).astype(F32)
    sin = sin * sign[None, :]
    reps = LANES // HEAD_DIM
    return cos.T, sin.T, jnp.tile(cos, (1, reps)), jnp.tile(sin, (1, reps))


def _seq_dft_matrices(S):
    half = S // 2
    lo_n = 64
    hi_n = half // lo_n
    k = jnp.arange(half, dtype=jnp.int32)
    w = 2.0 * math.pi / S
    a_hi = ((jnp.arange(hi_n + 1, dtype=jnp.int32)[:, None] * lo_n * k[None, :]) % S).astype(F32) * w
    a_lo = ((jnp.arange(lo_n, dtype=jnp.int32)[:, None] * k[None, :]) % S).astype(F32) * w
    ch, sh = jnp.cos(a_hi)[:, None, :], jnp.sin(a_hi)[:, None, :]
    cl, sl = jnp.cos(a_lo)[None, :, :], jnp.sin(a_lo)[None, :, :]
    scale = S ** -0.5
    cos = ((ch * cl - sh * sl) * scale).reshape(half + lo_n, half).astype(BF16)
    sin = ((sh[:hi_n] * cl + ch[:hi_n] * sl) * scale).reshape(half, half).astype(BF16)
    i = jnp.arange(REV, dtype=jnp.int32)
    jmat = ((i[:, None] + i[None, :]) == REV).astype(BF16)
    return cos, sin, jmat


def _trunk(x, w, tm_in, tq, dft_elems, tm_out, tm_mlp, tf):
    B, S, _ = x.shape
    cost, sint, cos, sin = _rope_tables(S)
    ab, qt, k, vt = _in_proj(x, w["g_mix"], w["wf"], w["wqt"], w["wk"], w["wvt"], w["mg"],
                             cost, sint, cos, sin, tm_in)
    of = _seq_dft(ab, *_seq_dft_matrices(S), n_col=F_WIDTH * S // dft_elems)
    oa_n = _window_attn(w["sink"], qt, k, vt, w["g_oa"], tq)
    T = B * S
    x2 = x.reshape(T, D_MODEL)
    x1, h2 = _out_proj(of.reshape(T, F_WIDTH), oa_n.reshape(T, A_WIDTH), x2,
                       w["wof"], w["woa"], w["g_of"], w["g_mlp"], tm_out)
    y = _mlp(h2, x1, w["wu"], w["wd"], w["g_final"], tm_mlp, tf)
    return y.reshape(B, S, D_MODEL)


def _prepare_weights(ln_mix_g, w_in, w_fourier, attn_sink, out_norm_fourier_g, out_norm_attn_g,
                     w_out, ln_mlp_g, w_up, w_down, ln_final_g):
    o1, o2, o3 = F_WIDTH, F_WIDTH + A_WIDTH, F_WIDTH + A_WIDTH + KV_WIDTH
    w_out_b = w_out.astype(BF16)
    return {
        "g_mix": ln_mix_g.reshape(1, D_MODEL),
        "wf": w_in[:, :o1].astype(BF16),
        "wqt": w_in[:, o1:o2].T.astype(BF16),
        "wk": w_in[:, o2:o3].astype(BF16),
        "wvt": w_in[:, o3:].T.astype(BF16),
        "mg": _fourier_weights(w_fourier),
        "sink": attn_sink.astype(F32),
        "g_of": out_norm_fourier_g.reshape(1, F_WIDTH),
        "g_oa": out_norm_attn_g.reshape(1, A_WIDTH),
        "wof": w_out_b[:F_WIDTH],
        "woa": w_out_b[F_WIDTH:],
        "g_mlp": ln_mlp_g.reshape(1, D_MODEL),
        "wu": w_up.astype(BF16),
        "wd": w_down.astype(BF16),
        "g_final": ln_final_g.reshape(1, D_MODEL),
    }


def kernel(x_prompt, x_sample, ln_mix_g, w_in, w_fourier, attn_sink, out_norm_fourier_g,
           out_norm_attn_g, w_out, ln_mlp_g, w_up, w_down, ln_final_g):
    assert ln_mix_g.shape[0] == 1, "single-layer block"
    w = _prepare_weights(ln_mix_g[0], w_in[0], w_fourier[0], attn_sink[0], out_norm_fourier_g[0],
                         out_norm_attn_g[0], w_out[0], ln_mlp_g[0], w_up[0], w_down[0], ln_final_g)
    tiles = dict(tm_in=1024, tq=1024, dft_elems=4096 * 512, tm_out=512, tm_mlp=512, tf=2048)
    return (_trunk(x_prompt, w, **tiles), _trunk(x_sample, w, **tiles))
```

```python
import functools
import math

import jax
import jax.numpy as jnp
from jax import lax
from jax.experimental import pallas as pl
from jax.experimental.pallas import tpu as pltpu

D_MODEL = 2048
F_WIDTH = 1024
F_GROUPS = 8
F_CH = 128
A_WIDTH = 1024
HEAD_DIM = 64
HALF_DIM = HEAD_DIM // 2
N_Q_HEADS = 16
N_KV_HEADS = 4
GQA_GROUP = 4
KV_WIDTH = 256
WINDOW = 128
BLOCK = 128
ROPE_THETA = 10000.0
D_FF = 4 * D_MODEL
EPS = 1e-6
NEG_INF = -1e30
LOG2_E = math.log2(math.e)

LANES = 128
VMEM_LIMIT = 60 * 1024 * 1024

BF16 = jnp.bfloat16
F32 = jnp.float32

_NT = (((1,), (1,)), ((), ()))


def _params(*sem, flags=None):
    return pltpu.CompilerParams(dimension_semantics=sem, vmem_limit_bytes=VMEM_LIMIT, flags=flags)


def _resident(shape):
    nd = len(shape)
    return pl.BlockSpec(shape, lambda *_: (0,) * nd, pipeline_mode=pl.Buffered(1))


def _fourier_weights_kernel(cs_ref, w_ref, o_ref):
    for g in range(F_GROUPS):
        w = w_ref[g]
        c = jnp.dot(cs_ref[0], w, preferred_element_type=F32, precision=lax.Precision.HIGHEST)
        s = jnp.dot(cs_ref[1], w, preferred_element_type=F32, precision=lax.Precision.HIGHEST)
        o_ref[g, :F_CH, :F_CH] = c.astype(BF16)
        o_ref[g, :F_CH, F_CH:] = c.astype(BF16)
        o_ref[g, F_CH:, :F_CH] = (-s).astype(BF16)
        o_ref[g, F_CH:, F_CH:] = s.astype(BF16)


def _fourier_weights(w_fourier):
    n = jnp.arange(F_CH, dtype=jnp.int32)
    ang = ((n[:, None] * n[None, :]) % F_CH).astype(F32) * (2.0 * math.pi / F_CH)
    cs = jnp.stack([jnp.cos(ang), jnp.sin(ang)]) * (F_CH ** -0.5)
    return pl.pallas_call(
        _fourier_weights_kernel,
        out_shape=jax.ShapeDtypeStruct((F_GROUPS, 2 * F_CH, 2 * F_CH), BF16),
        name="fourier_weights",
    )(cs, w_fourier)


IN_SUB = 4


def _in_proj_kernel(x_ref, g_ref, wf_ref, wqt_ref, wk_ref, wvt_ref,
                    cost_ref, sint_ref, cos_ref, sin_ref,
                    zf_ref, qt_ref, k_ref, vt_ref):
    tm = x_ref.shape[0]
    sub = tm // IN_SUB
    scale = HEAD_DIM ** -0.5 * LOG2_E
    first_half = (lax.broadcasted_iota(jnp.int32, (sub, LANES), 1) % HEAD_DIM) < HALF_DIM
    for i in range(IN_SUB):
        rows = slice(i * sub, (i + 1) * sub)
        x = x_ref[rows, :]
        r = lax.rsqrt(jnp.mean(x * x, axis=-1, keepdims=True) + EPS)
        h = ((x * r) * g_ref[...]).astype(BF16)

        zf = jnp.dot(h, wf_ref[...], preferred_element_type=F32)
        qt = lax.dot_general(wqt_ref[...], h, _NT, preferred_element_type=F32)
        kk = jnp.dot(h, wk_ref[...], preferred_element_type=F32)
        vt = lax.dot_general(wvt_ref[...], h, _NT, preferred_element_type=F32)

        zf_ref[rows, :] = zf.astype(BF16)

        cost = cost_ref[:, rows]
        sint = sint_ref[:, rows]
        for hd in range(N_Q_HEADS):
            lo = hd * HEAD_DIM
            blk = qt[lo:lo + HEAD_DIM]
            rot = jnp.concatenate([blk[HALF_DIM:], blk[:HALF_DIM]], axis=0)
            qt_ref[lo:lo + HEAD_DIM, rows] = ((blk * cost + rot * sint) * scale).astype(BF16)

        cos = cos_ref[rows, :]
        sin = sin_ref[rows, :]
        for c in range(KV_WIDTH // LANES):
            blk = kk[:, c * LANES:(c + 1) * LANES]
            rot = jnp.where(first_half,
                            pltpu.roll(blk, LANES - HALF_DIM, axis=1),
                            pltpu.roll(blk, HALF_DIM, axis=1))
            k_ref[rows, c * LANES:(c + 1) * LANES] = (blk * cos + rot * sin).astype(BF16)

        vt_ref[:, rows] = vt.astype(BF16)


def _in_proj(x, g, wf, wqt, wk, wvt, cost, sint, cos, sin, tm):
    B, S, _ = x.shape
    return pl.pallas_call(
        _in_proj_kernel,
        grid=(B, S // tm),
        in_specs=[
            pl.BlockSpec((None, tm, D_MODEL), lambda b, s: (b, s, 0)),
            _resident((1, D_MODEL)),
            _resident((D_MODEL, F_WIDTH)),
            _resident((A_WIDTH, D_MODEL)),
            _resident((D_MODEL, KV_WIDTH)),
            _resident((KV_WIDTH, D_MODEL)),
            pl.BlockSpec((HEAD_DIM, tm), lambda b, s: (0, s)),
            pl.BlockSpec((HEAD_DIM, tm), lambda b, s: (0, s)),
            pl.BlockSpec((tm, LANES), lambda b, s: (s, 0)),
            pl.BlockSpec((tm, LANES), lambda b, s: (s, 0)),
        ],
        out_specs=[
            pl.BlockSpec((None, tm, F_WIDTH), lambda b, s: (b, s, 0)),
            pl.BlockSpec((None, A_WIDTH, tm), lambda b, s: (b, 0, s)),
            pl.BlockSpec((None, tm, KV_WIDTH), lambda b, s: (b, s, 0)),
            pl.BlockSpec((None, KV_WIDTH, tm), lambda b, s: (b, 0, s)),
        ],
        out_shape=[
            jax.ShapeDtypeStruct((B, S, F_WIDTH), BF16),
            jax.ShapeDtypeStruct((B, A_WIDTH, S), BF16),
            jax.ShapeDtypeStruct((B, S, KV_WIDTH), BF16),
            jax.ShapeDtypeStruct((B, KV_WIDTH, S), BF16),
        ],
        compiler_params=_params("parallel", "parallel"),
        name="in_proj",
    )(x, g, wf, wqt, wk, wvt, cost, sint, cos, sin)


REV = 256
ROW_PACK = 16
DFT_ROWS = 512


def _first_row(x):
    return jnp.where(lax.broadcasted_iota(jnp.int32, x.shape, 0) == 0, x, 0.0)


def _patch_first_row(tile, row):
    head = tile[:ROW_PACK] + _first_row(row)
    return jnp.concatenate([head, tile[ROW_PACK:]], axis=0)


def _seq_dft_kernel(z_ref, c_ref, s_ref, j_ref, mix_ref, o_ref, zp_ref, zm_ref, u_ref, *, seq):
    half = seq // 2
    n_rev = half // REV
    n_grp = z_ref.shape[1] // F_CH
    jmat = j_ref[...]

    for t in range(n_rev):
        lo = t * REV
        src = seq - lo - REV
        r = jnp.dot(jmat, z_ref[src:src + REV, :], preferred_element_type=F32)
        if t > 0:
            r = _patch_first_row(r, z_ref[seq - lo:seq - lo + ROW_PACK, :].astype(F32))
        z_lo = z_ref[lo:lo + REV, :].astype(F32)
        zp_ref[lo:lo + REV, :] = (z_lo + r).astype(BF16)
        zm_ref[lo:lo + REV, :] = (z_lo - r).astype(BF16)

    z_half = z_ref[half:half + ROW_PACK, :].astype(F32)[0:1, :] * (seq ** -0.5)
    row_par = lax.broadcasted_iota(jnp.int32, (DFT_ROWS, 1), 0) % 2
    sgn_z_half = jnp.where(row_par == 0, z_half, -z_half)

    zp = zp_ref[...]
    zm = zm_ref[...]
    n_row = half // DFT_ROWS

    def spectra(i):
        lo = i * DFT_ROWS
        extra = ROW_PACK if i == n_row - 1 else 0
        p = jnp.dot(c_ref[lo:lo + DFT_ROWS + extra, :], zp, preferred_element_type=F32)
        q = jnp.dot(s_ref[lo:lo + DFT_ROWS, :], zm, preferred_element_type=F32)
        if extra:
            p = jnp.concatenate([p[:DFT_ROWS] + sgn_z_half, p[DFT_ROWS:] + z_half], axis=0)
            q = jnp.concatenate([q, jnp.zeros((extra, q.shape[1]), F32)], axis=0)
        else:
            p = p + sgn_z_half
        return p.astype(BF16), q.astype(BF16)

    nyq = []

    def mix(i, pq):
        p, q = pq
        lo = i * DFT_ROWS
        for g in range(n_grp):
            cols = slice(g * F_CH, (g + 1) * F_CH)
            lu = jnp.dot(jnp.concatenate([p[:, cols], q[:, cols]], axis=1), mix_ref[g],
                         preferred_element_type=F32)
            o_ref[lo:lo + DFT_ROWS, cols] = lu[:DFT_ROWS, :F_CH].astype(BF16)
            u_ref[lo:lo + DFT_ROWS, cols] = lu[:DFT_ROWS, F_CH:].astype(BF16)
            if lu.shape[0] > DFT_ROWS:
                nyq.append(lu[DFT_ROWS:, :F_CH])

    pending = spectra(0)
    for i in range(n_row):
        cur = pending
        if i + 1 < n_row:
            pending = spectra(i + 1)
        mix(i, cur)
    nyq_row = jnp.concatenate(nyq, axis=1)

    for t in range(n_rev):
        src = half - (t + 1) * REV
        r = jnp.dot(jmat, u_ref[src:src + REV, :], preferred_element_type=F32)
        if t == 0:
            r = _patch_first_row(r, nyq_row)
        else:
            r = _patch_first_row(r, u_ref[src + REV:src + REV + ROW_PACK, :].astype(F32))
        o_ref[half + t * REV:half + (t + 1) * REV, :] = r.astype(BF16)


def _seq_dft(zf, cmat, smat, jmat, mix, tn):
    B, S, _ = zf.shape
    half = S // 2
    assert half % DFT_ROWS == 0 and half % 2 == 0 and tn % F_CH == 0
    return pl.pallas_call(
        functools.partial(_seq_dft_kernel, seq=S),
        grid=(B, F_WIDTH // tn),
        in_specs=[
            pl.BlockSpec((None, S, tn), lambda b, c: (b, 0, c)),
            pl.BlockSpec((half + ROW_PACK, half), lambda b, c: (0, 0), pipeline_mode=pl.Buffered(1)),
            _resident((half, half)),
            _resident((REV, REV)),
            pl.BlockSpec((tn // F_CH, 2 * F_CH, 2 * F_CH), lambda b, c: (c, 0, 0)),
        ],
        out_specs=pl.BlockSpec((None, S, tn), lambda b, c: (b, 0, c)),
        out_shape=jax.ShapeDtypeStruct((B, S, F_WIDTH), BF16),
        scratch_shapes=[pltpu.VMEM((half, tn), BF16),
                        pltpu.VMEM((half, tn), BF16),
                        pltpu.VMEM((half, tn), BF16)],
        compiler_params=_params("parallel", "parallel"),
        name="seq_dft",
    )(zf, cmat, smat, jmat, mix)


QK_AHEAD = 1


def _window_attn_kernel(sink_ref, qt_ref, k_ref, kp_ref, kn_ref, vt_ref, vtp_ref, vtn_ref, g_ref,
                        o_ref, kfull, vtfull, *, tq, n_blocks):
    s_idx = pl.program_id(1)
    r_blocks = tq // BLOCK
    span = 3 * BLOCK

    kfull[0:BLOCK, :] = kp_ref[...]
    kfull[BLOCK:BLOCK + tq, :] = k_ref[...]
    kfull[BLOCK + tq:, :] = kn_ref[...]
    vtfull[:, 0:BLOCK] = vtp_ref[...]
    vtfull[:, BLOCK:BLOCK + tq] = vt_ref[...]
    vtfull[:, BLOCK + tq:] = vtn_ref[...]

    key_i = lax.broadcasted_iota(jnp.int32, (BLOCK, BLOCK), 0)
    qry_i = lax.broadcasted_iota(jnp.int32, (BLOCK, BLOCK), 1)
    assert WINDOW == BLOCK

    def scores(n, h):
        r0 = h * HEAD_DIM
        q4 = jnp.concatenate(
            [qt_ref[(h * GQA_GROUP + g) * HEAD_DIM:(h * GQA_GROUP + g + 1) * HEAD_DIM,
                    n * BLOCK:(n + 1) * BLOCK] for g in range(GQA_GROUP)], axis=1)
        parts = [q4]
        if r0 > 0:
            parts.insert(0, jnp.zeros((r0, GQA_GROUP * BLOCK), BF16))
        if r0 + HEAD_DIM < KV_WIDTH:
            parts.append(jnp.zeros((KV_WIDTH - r0 - HEAD_DIM, GQA_GROUP * BLOCK), BF16))
        qpad = jnp.concatenate(parts, axis=0)
        kwin = kfull[n * BLOCK:n * BLOCK + span, :]
        return jnp.dot(kwin, qpad, preferred_element_type=F32)

    def attend(n, h, st_all):
        nglob = s_idx * r_blocks + n
        valid_prev = (key_i >= qry_i) & (nglob > 0)
        valid_next = (key_i <= qry_i) & (nglob < n_blocks - 1)
        pn, tail = [], []
        for g in range(GQA_GROUP):
            cols = slice(g * BLOCK, (g + 1) * BLOCK)
            s_prev = jnp.where(valid_prev, st_all[0:BLOCK, cols], NEG_INF)
            s_self = st_all[BLOCK:2 * BLOCK, cols]
            s_next = jnp.where(valid_next, st_all[2 * BLOCK:, cols], NEG_INF)
            sink = jnp.full((1, BLOCK), sink_ref[h * GQA_GROUP + g] * LOG2_E, F32)
            m = jnp.max(jnp.maximum(jnp.maximum(s_prev, s_self), s_next), axis=0, keepdims=True)
            m = jnp.maximum(m, sink)
            tail.append(jnp.exp2(sink - m))
            pn.append(jnp.concatenate([jnp.exp2(s_prev - m), jnp.exp2(s_self - m),
                                       jnp.exp2(s_next - m)], axis=0).astype(BF16))
        r0 = h * HEAD_DIM
        vwin = vtfull[r0:r0 + HEAD_DIM, n * BLOCK:n * BLOCK + span]
        vext = jnp.concatenate([vwin, jnp.ones((ROW_PACK, span), BF16)], axis=0)
        ot = jnp.dot(vext, jnp.concatenate(pn, axis=1), preferred_element_type=F32)
        den = ot[HEAD_DIM:HEAD_DIM + 1] + jnp.concatenate(tail, axis=1)
        ot = ot[:HEAD_DIM] * (1.0 / den)
        return [ot[:, g * BLOCK:(g + 1) * BLOCK] for g in range(GQA_GROUP)]

    pairs = [(n, h) for n in range(r_blocks) for h in range(N_KV_HEADS)]
    pending = [scores(*pr) for pr in pairs[:QK_AHEAD]]
    heads = []
    for i, (n, h) in enumerate(pairs):
        if i + QK_AHEAD < len(pairs):
            pending.append(scores(*pairs[i + QK_AHEAD]))
        heads.extend(attend(n, h, pending.pop(0)))
        if h == N_KV_HEADS - 1:
            y = jnp.concatenate(heads, axis=0)
            heads = []
            r = lax.rsqrt(jnp.mean(y * y, axis=0, keepdims=True) + EPS)
            o_ref[n * BLOCK:(n + 1) * BLOCK, :] = ((y * r).T * g_ref[...]).astype(BF16)


def _window_attn(sink, qt, k, vt, g, tq):
    B, _, S = qt.shape
    n_blocks = S // BLOCK
    r_blocks = tq // BLOCK
    prev_blk = lambda s: jnp.maximum(s * r_blocks - 1, 0)
    next_blk = lambda s: jnp.minimum((s + 1) * r_blocks, n_blocks - 1)
    return pl.pallas_call(
        functools.partial(_window_attn_kernel, tq=tq, n_blocks=n_blocks),
        grid=(B, S // tq),
        in_specs=[
            pl.BlockSpec(memory_space=pltpu.SMEM),
            pl.BlockSpec((None, A_WIDTH, tq), lambda b, s: (b, 0, s)),
            pl.BlockSpec((None, tq, KV_WIDTH), lambda b, s: (b, s, 0)),
            pl.BlockSpec((None, BLOCK, KV_WIDTH), lambda b, s: (b, prev_blk(s), 0)),
            pl.BlockSpec((None, BLOCK, KV_WIDTH), lambda b, s: (b, next_blk(s), 0)),
            pl.BlockSpec((None, KV_WIDTH, tq), lambda b, s: (b, 0, s)),
            pl.BlockSpec((None, KV_WIDTH, BLOCK), lambda b, s: (b, 0, prev_blk(s))),
            pl.BlockSpec((None, KV_WIDTH, BLOCK), lambda b, s: (b, 0, next_blk(s))),
            pl.BlockSpec((1, A_WIDTH), lambda b, s: (0, 0)),
        ],
        out_specs=pl.BlockSpec((None, tq, A_WIDTH), lambda b, s: (b, s, 0)),
        out_shape=jax.ShapeDtypeStruct((B, S, A_WIDTH), BF16),
        scratch_shapes=[
            pltpu.VMEM((tq + 2 * BLOCK, KV_WIDTH), BF16),
            pltpu.VMEM((KV_WIDTH, tq + 2 * BLOCK), BF16),
        ],
        compiler_params=_params("parallel", "parallel"),
        name="window_attn",
    )(sink, qt, k, k, k, vt, vt, vt, g)


OUT_SUB = 2


def _out_proj_kernel(of_ref, oa_ref, x_ref, wof_ref, woa_ref, gof_ref, g_ref, x1_ref, h2_ref):
    sub = x_ref.shape[0] // OUT_SUB
    for i in range(OUT_SUB):
        rows = slice(i * sub, (i + 1) * sub)
        of = of_ref[rows, :].astype(F32)
        rf = lax.rsqrt(jnp.mean(of * of, axis=-1, keepdims=True) + EPS)
        of_n = ((of * rf) * gof_ref[...]).astype(BF16)
        y = jnp.dot(of_n, wof_ref[...], preferred_element_type=F32)
        y = y + jnp.dot(oa_ref[rows, :], woa_ref[...], preferred_element_type=F32)
        x1 = x_ref[rows, :] + y
        x1_ref[rows, :] = x1
        r = lax.rsqrt(jnp.mean(x1 * x1, axis=-1, keepdims=True) + EPS)
        h2_ref[rows, :] = ((x1 * r) * g_ref[...]).astype(BF16)


def _out_proj(of, oa_n, x, wof, woa, gof, g, tm):
    T = x.shape[0]
    return pl.pallas_call(
        _out_proj_kernel,
        grid=(T // tm,),
        in_specs=[
            pl.BlockSpec((tm, F_WIDTH), lambda i: (i, 0)),
            pl.BlockSpec((tm, A_WIDTH), lambda i: (i, 0)),
            pl.BlockSpec((tm, D_MODEL), lambda i: (i, 0)),
            _resident((F_WIDTH, D_MODEL)),
            _resident((A_WIDTH, D_MODEL)),
            _resident((1, F_WIDTH)),
            _resident((1, D_MODEL)),
        ],
        out_specs=[
            pl.BlockSpec((tm, D_MODEL), lambda i: (i, 0)),
            pl.BlockSpec((tm, D_MODEL), lambda i: (i, 0)),
        ],
        out_shape=[
            jax.ShapeDtypeStruct((T, D_MODEL), F32),
            jax.ShapeDtypeStruct((T, D_MODEL), BF16),
        ],
        compiler_params=_params("parallel"),
        name="out_proj",
    )(of, oa_n, x, wof, woa, gof, g)


MLP_CHUNKS = 2


def _mlp_kernel(h2_ref, x1_ref, wu_ref, wd_ref, g_ref, o_ref):
    j = pl.program_id(1)

    @pl.when(j == 0)
    def _():
        o_ref[...] = x1_ref[...]

    tf = wu_ref.shape[1]
    for c in range(MLP_CHUNKS):
        lo, hi = c * tf // MLP_CHUNKS, (c + 1) * tf // MLP_CHUNKS
        u = jnp.maximum(jnp.dot(h2_ref[...], wu_ref[:, lo:hi], preferred_element_type=F32), 0.0)
        o_ref[...] += jnp.dot((u * u).astype(BF16), wd_ref[lo:hi, :], preferred_element_type=F32)

    @pl.when(j == pl.num_programs(1) - 1)
    def _():
        y = o_ref[...]
        r = lax.rsqrt(jnp.mean(y * y, axis=-1, keepdims=True) + EPS)
        o_ref[...] = (y * r) * g_ref[...]


def _mlp(h2, x1, wu, wd, g, tm, tf):
    T = h2.shape[0]
    return pl.pallas_call(
        _mlp_kernel,
        grid=(T // tm, D_FF // tf),
        in_specs=[
            pl.BlockSpec((tm, D_MODEL), lambda i, j: (i, 0)),
            pl.BlockSpec((tm, D_MODEL), lambda i, j: (i, 0)),
            pl.BlockSpec((D_MODEL, tf), lambda i, j: (0, j)),
            pl.BlockSpec((tf, D_MODEL), lambda i, j: (j, 0)),
            _resident((1, D_MODEL)),
        ],
        out_specs=pl.BlockSpec((tm, D_MODEL), lambda i, j: (i, 0)),
        out_shape=jax.ShapeDtypeStruct((T, D_MODEL), F32),
        compiler_params=_params("parallel", "arbitrary"),
        name="mlp",
    )(h2, x1, wu, wd, g)


def _rope_tables(S):
    inv_freq = 1.0 / (ROPE_THETA ** (jnp.arange(0, HEAD_DIM, 2, dtype=F32) / HEAD_DIM))
    pos = jnp.arange(S, dtype=F32)
    ang = pos[:, None] * inv_freq[None, :]
    ang = jnp.concatenate([ang, ang], axis=-1)
    cos, sin = jnp.cos(ang), jnp.sin(ang)
    sign = jnp.where(jnp.arange(HEAD_DIM) < HALF_DIM, -1.0, 1.0).astype(F32)
    sin = sin * sign[None, :]
    reps = LANES // HEAD_DIM
    return cos.T, sin.T, jnp.tile(cos, (1, reps)), jnp.tile(sin, (1, reps))


def _seq_dft_matrices(S):
    half = S // 2
    lo_n = 64
    hi_n = half // lo_n
    k = jnp.arange(half, dtype=jnp.int32)
    w = 2.0 * math.pi / S
    a_hi = ((jnp.arange(hi_n + 1, dtype=jnp.int32)[:, None] * lo_n * k[None, :]) % S).astype(F32) * w
    a_lo = ((jnp.arange(lo_n, dtype=jnp.int32)[:, None] * k[None, :]) % S).astype(F32) * w
    ch, sh = jnp.cos(a_hi)[:, None, :], jnp.sin(a_hi)[:, None, :]
    cl, sl = jnp.cos(a_lo)[None, :, :], jnp.sin(a_lo)[None, :, :]
    scale = S ** -0.5
    cos = ((ch * cl - sh * sl) * scale).reshape(half + lo_n, half).astype(BF16)
    sin = ((sh[:hi_n] * cl + ch[:hi_n] * sl) * scale).reshape(half, half).astype(BF16)
    i = jnp.arange(REV, dtype=jnp.int32)
    jmat = ((i[:, None] + i[None, :]) == REV).astype(BF16)
    return cos, sin, jmat


def _trunk(x, w, tm_in, tq, dft_tn, tm_out, tm_mlp, tf):
    B, S, _ = x.shape
    cost, sint, cos, sin = _rope_tables(S)
    zf, qt, k, vt = _in_proj(x, w["g_mix"], w["wf"], w["wqt"], w["wk"], w["wvt"],
                             cost, sint, cos, sin, tm_in)
    of = _seq_dft(zf, *_seq_dft_matrices(S), w["mix"], dft_tn)
    oa_n = _window_attn(w["sink"], qt, k, vt, w["g_oa"], tq)
    T = B * S
    x2 = x.reshape(T, D_MODEL)
    x1, h2 = _out_proj(of.reshape(T, F_WIDTH), oa_n.reshape(T, A_WIDTH), x2,
                       w["wof"], w["woa"], w["g_of"], w["g_mlp"], tm_out)
    y = _mlp(h2, x1, w["wu"], w["wd"], w["g_final"], tm_mlp, tf)
    return y.reshape(B, S, D_MODEL)


def _prepare_weights(ln_mix_g, w_in, w_fourier, attn_sink, out_norm_fourier_g, out_norm_attn_g,
                     w_out, ln_mlp_g, w_up, w_down, ln_final_g):
    o1, o2, o3 = F_WIDTH, F_WIDTH + A_WIDTH, F_WIDTH + A_WIDTH + KV_WIDTH
    w_out_b = w_out.astype(BF16)
    return {
        "g_mix": ln_mix_g.reshape(1, D_MODEL),
        "wf": w_in[:, :o1].astype(BF16),
        "wqt": w_in[:, o1:o2].T.astype(BF16),
        "wk": w_in[:, o2:o3].astype(BF16),
        "wvt": w_in[:, o3:].T.astype(BF16),
        "mix": _fourier_weights(w_fourier),
        "sink": attn_sink.astype(F32),
        "g_of": out_norm_fourier_g.reshape(1, F_WIDTH),
        "g_oa": out_norm_attn_g.reshape(1, A_WIDTH),
        "wof": w_out_b[:F_WIDTH],
        "woa": w_out_b[F_WIDTH:],
        "g_mlp": ln_mlp_g.reshape(1, D_MODEL),
        "wu": w_up.astype(BF16),
        "wd": w_down.astype(BF16),
        "g_final": ln_final_g.reshape(1, D_MODEL),
    }


def kernel(x_prompt, x_sample, ln_mix_g, w_in, w_fourier, attn_sink, out_norm_fourier_g,
           out_norm_attn_g, w_out, ln_mlp_g, w_up, w_down, ln_final_g):
    assert ln_mix_g.shape[0] == 1, "single-layer block"
    w = _prepare_weights(ln_mix_g[0], w_in[0], w_fourier[0], attn_sink[0], out_norm_fourier_g[0],
                         out_norm_attn_g[0], w_out[0], ln_mlp_g[0], w_up[0], w_down[0], ln_final_g)
    tiles = dict(tm_in=1024, tq=1024, dft_tn=512, tm_out=512, tm_mlp=512, tf=2048)
    return (_trunk(x_prompt, w, **tiles), _trunk(x_sample, w, **tiles))
```

```python
import functools
import math

import jax
import jax.numpy as jnp
from jax import lax
from jax.experimental import pallas as pl
from jax.experimental.pallas import tpu as pltpu

D_MODEL = 2048
F_WIDTH = 1024
F_GROUPS = 8
F_CH = 128
A_WIDTH = 1024
HEAD_DIM = 64
HALF_DIM = HEAD_DIM // 2
N_Q_HEADS = 16
N_KV_HEADS = 4
GQA_GROUP = 4
KV_WIDTH = 256
WINDOW = 128
BLOCK = 128
ROPE_THETA = 10000.0
D_FF = 4 * D_MODEL
EPS = 1e-6
NEG_INF = -1e30
LOG2_E = math.log2(math.e)

LANES = 128
VMEM_LIMIT = 60 * 1024 * 1024

BF16 = jnp.bfloat16
F32 = jnp.float32

_NT = (((1,), (1,)), ((), ()))


def _params(*sem, flags=None):
    return pltpu.CompilerParams(dimension_semantics=sem, vmem_limit_bytes=VMEM_LIMIT, flags=flags)


def _resident(shape):
    nd = len(shape)
    return pl.BlockSpec(shape, lambda *_: (0,) * nd, pipeline_mode=pl.Buffered(1))


def _fourier_weights_kernel(cs_ref, w_ref, o_ref):
    for g in range(F_GROUPS):
        w = w_ref[g]
        c = jnp.dot(cs_ref[0], w, preferred_element_type=F32, precision=lax.Precision.HIGHEST)
        s = jnp.dot(cs_ref[1], w, preferred_element_type=F32, precision=lax.Precision.HIGHEST)
        o_ref[g, :F_CH, :F_CH] = c.astype(BF16)
        o_ref[g, :F_CH, F_CH:] = c.astype(BF16)
        o_ref[g, F_CH:, :F_CH] = (-s).astype(BF16)
        o_ref[g, F_CH:, F_CH:] = s.astype(BF16)


def _fourier_weights(w_fourier):
    n = jnp.arange(F_CH, dtype=jnp.int32)
    ang = ((n[:, None] * n[None, :]) % F_CH).astype(F32) * (2.0 * math.pi / F_CH)
    cs = jnp.stack([jnp.cos(ang), jnp.sin(ang)]) * (F_CH ** -0.5)
    return pl.pallas_call(
        _fourier_weights_kernel,
        out_shape=jax.ShapeDtypeStruct((F_GROUPS, 2 * F_CH, 2 * F_CH), BF16),
        name="fourier_weights",
    )(cs, w_fourier)


IN_SUB = 4


def _in_proj_kernel(x_ref, g_ref, wf_ref, wqt_ref, wk_ref, wvt_ref,
                    cost_ref, sint_ref, cos_ref, sin_ref,
                    zf_ref, qt_ref, k_ref, vt_ref):
    tm = x_ref.shape[0]
    sub = tm // IN_SUB
    scale = HEAD_DIM ** -0.5 * LOG2_E
    first_half = (lax.broadcasted_iota(jnp.int32, (sub, LANES), 1) % HEAD_DIM) < HALF_DIM
    for i in range(IN_SUB):
        rows = slice(i * sub, (i + 1) * sub)
        x = x_ref[rows, :]
        r = lax.rsqrt(jnp.mean(x * x, axis=-1, keepdims=True) + EPS)
        h = ((x * r) * g_ref[...]).astype(BF16)

        zf = jnp.dot(h, wf_ref[...], preferred_element_type=F32)
        qt = lax.dot_general(wqt_ref[...], h, _NT, preferred_element_type=F32)
        kk = jnp.dot(h, wk_ref[...], preferred_element_type=F32)
        vt = lax.dot_general(wvt_ref[...], h, _NT, preferred_element_type=F32)

        zf_ref[rows, :] = zf.astype(BF16)

        cost = cost_ref[:, rows]
        sint = sint_ref[:, rows]
        for hd in range(N_Q_HEADS):
            lo = hd * HEAD_DIM
            blk = qt[lo:lo + HEAD_DIM]
            rot = jnp.concatenate([blk[HALF_DIM:], blk[:HALF_DIM]], axis=0)
            qt_ref[lo:lo + HEAD_DIM, rows] = ((blk * cost + rot * sint) * scale).astype(BF16)

        cos = cos_ref[rows, :]
        sin = sin_ref[rows, :]
        for c in range(KV_WIDTH // LANES):
            blk = kk[:, c * LANES:(c + 1) * LANES]
            rot = jnp.where(first_half,
                            pltpu.roll(blk, LANES - HALF_DIM, axis=1),
                            pltpu.roll(blk, HALF_DIM, axis=1))
            k_ref[rows, c * LANES:(c + 1) * LANES] = (blk * cos + rot * sin).astype(BF16)

        vt_ref[:, rows] = vt.astype(BF16)


def _in_proj(x, g, wf, wqt, wk, wvt, cost, sint, cos, sin, tm):
    B, S, _ = x.shape
    return pl.pallas_call(
        _in_proj_kernel,
        grid=(B, S // tm),
        in_specs=[
            pl.BlockSpec((None, tm, D_MODEL), lambda b, s: (b, s, 0)),
            _resident((1, D_MODEL)),
            _resident((D_MODEL, F_WIDTH)),
            _resident((A_WIDTH, D_MODEL)),
            _resident((D_MODEL, KV_WIDTH)),
            _resident((KV_WIDTH, D_MODEL)),
            pl.BlockSpec((HEAD_DIM, tm), lambda b, s: (0, s)),
            pl.BlockSpec((HEAD_DIM, tm), lambda b, s: (0, s)),
            pl.BlockSpec((tm, LANES), lambda b, s: (s, 0)),
            pl.BlockSpec((tm, LANES), lambda b, s: (s, 0)),
        ],
        out_specs=[
            pl.BlockSpec((None, tm, F_WIDTH), lambda b, s: (b, s, 0)),
            pl.BlockSpec((None, A_WIDTH, tm), lambda b, s: (b, 0, s)),
            pl.BlockSpec((None, tm, KV_WIDTH), lambda b, s: (b, s, 0)),
            pl.BlockSpec((None, KV_WIDTH, tm), lambda b, s: (b, 0, s)),
        ],
        out_shape=[
            jax.ShapeDtypeStruct((B, S, F_WIDTH), BF16),
            jax.ShapeDtypeStruct((B, A_WIDTH, S), BF16),
            jax.ShapeDtypeStruct((B, S, KV_WIDTH), BF16),
            jax.ShapeDtypeStruct((B, KV_WIDTH, S), BF16),
        ],
        compiler_params=_params("parallel", "parallel"),
        name="in_proj",
    )(x, g, wf, wqt, wk, wvt, cost, sint, cos, sin)


REV = 256
ROW_PACK = 16
DFT_ROWS = 512


def _first_row(x):
    return jnp.where(lax.broadcasted_iota(jnp.int32, x.shape, 0) == 0, x, 0.0)


def _patch_first_row(tile, row):
    head = tile[:ROW_PACK] + _first_row(row)
    return jnp.concatenate([head, tile[ROW_PACK:]], axis=0)


def _seq_dft_kernel(z_ref, c_ref, s_ref, j_ref, mix_ref, o_ref, zp_ref, zm_ref, u_ref, *, seq):
    half = seq // 2
    n_rev = half // REV
    n_grp = z_ref.shape[1] // F_CH
    jmat = j_ref[...]

    for t in range(n_rev):
        lo = t * REV
        src = seq - lo - REV
        r = jnp.dot(jmat, z_ref[src:src + REV, :], preferred_element_type=F32)
        if t > 0:
            r = _patch_first_row(r, z_ref[seq - lo:seq - lo + ROW_PACK, :].astype(F32))
        z_lo = z_ref[lo:lo + REV, :].astype(F32)
        zp_ref[lo:lo + REV, :] = (z_lo + r).astype(BF16)
        zm_ref[lo:lo + REV, :] = (z_lo - r).astype(BF16)

    z_half = z_ref[half:half + ROW_PACK, :].astype(F32)[0:1, :] * (seq ** -0.5)
    row_par = lax.broadcasted_iota(jnp.int32, (DFT_ROWS, 1), 0) % 2
    sgn_z_half = jnp.where(row_par == 0, z_half, -z_half)

    zp = zp_ref[...]
    zm = zm_ref[...]
    n_row = half // DFT_ROWS

    def spectra(i):
        lo = i * DFT_ROWS
        extra = ROW_PACK if i == n_row - 1 else 0
        p = jnp.dot(c_ref[lo:lo + DFT_ROWS + extra, :], zp, preferred_element_type=F32)
        q = jnp.dot(s_ref[lo:lo + DFT_ROWS, :], zm, preferred_element_type=F32)
        if extra:
            p = jnp.concatenate([p[:DFT_ROWS] + sgn_z_half, p[DFT_ROWS:] + z_half], axis=0)
            q = jnp.concatenate([q, jnp.zeros((extra, q.shape[1]), F32)], axis=0)
        else:
            p = p + sgn_z_half
        return p.astype(BF16), q.astype(BF16)

    nyq = []

    def mix(i, pq):
        p, q = pq
        lo = i * DFT_ROWS
        for g in range(n_grp):
            cols = slice(g * F_CH, (g + 1) * F_CH)
            lu = jnp.dot(jnp.concatenate([p[:, cols], q[:, cols]], axis=1), mix_ref[g],
                         preferred_element_type=F32)
            o_ref[lo:lo + DFT_ROWS, cols] = lu[:DFT_ROWS, :F_CH].astype(BF16)
            u_ref[lo:lo + DFT_ROWS, cols] = lu[:DFT_ROWS, F_CH:].astype(BF16)
            if lu.shape[0] > DFT_ROWS:
                nyq.append(lu[DFT_ROWS:, :F_CH])

    pending = spectra(0)
    for i in range(n_row):
        cur = pending
        if i + 1 < n_row:
            pending = spectra(i + 1)
        mix(i, cur)
    nyq_row = jnp.concatenate(nyq, axis=1)

    for t in range(n_rev):
        src = half - (t + 1) * REV
        r = jnp.dot(jmat, u_ref[src:src + REV, :], preferred_element_type=F32)
        if t == 0:
            r = _patch_first_row(r, nyq_row)
        else:
            r = _patch_first_row(r, u_ref[src + REV:src + REV + ROW_PACK, :].astype(F32))
        o_ref[half + t * REV:half + (t + 1) * REV, :] = r.astype(BF16)


def _seq_dft(zf, cmat, smat, jmat, mix, tn):
    B, S, _ = zf.shape
    half = S // 2
    assert half % DFT_ROWS == 0 and half % 2 == 0 and tn % F_CH == 0
    return pl.pallas_call(
        functools.partial(_seq_dft_kernel, seq=S),
        grid=(B, F_WIDTH // tn),
        in_specs=[
            pl.BlockSpec((None, S, tn), lambda b, c: (b, 0, c)),
            pl.BlockSpec((half + ROW_PACK, half), lambda b, c: (0, 0), pipeline_mode=pl.Buffered(1)),
            _resident((half, half)),
            _resident((REV, REV)),
            pl.BlockSpec((tn // F_CH, 2 * F_CH, 2 * F_CH), lambda b, c: (c, 0, 0)),
        ],
        out_specs=pl.BlockSpec((None, S, tn), lambda b, c: (b, 0, c)),
        out_shape=jax.ShapeDtypeStruct((B, S, F_WIDTH), BF16),
        scratch_shapes=[pltpu.VMEM((half, tn), BF16),
                        pltpu.VMEM((half, tn), BF16),
                        pltpu.VMEM((half, tn), BF16)],
        compiler_params=_params("parallel", "parallel"),
        name="seq_dft",
    )(zf, cmat, smat, jmat, mix)


def _window_attn_kernel(sink_ref, qt_ref, k_ref, kp_ref, kn_ref, vt_ref, vtp_ref, vtn_ref, g_ref,
                        o_ref, kfull, vtfull, *, tq, n_blocks):
    s_idx = pl.program_id(1)
    r_blocks = tq // BLOCK
    span = 3 * BLOCK

    kfull[0:BLOCK, :] = kp_ref[...]
    kfull[BLOCK:BLOCK + tq, :] = k_ref[...]
    kfull[BLOCK + tq:, :] = kn_ref[...]
    vtfull[:, 0:BLOCK] = vtp_ref[...]
    vtfull[:, BLOCK:BLOCK + tq] = vt_ref[...]
    vtfull[:, BLOCK + tq:] = vtn_ref[...]

    key_i = lax.broadcasted_iota(jnp.int32, (BLOCK, BLOCK), 0)
    qry_i = lax.broadcasted_iota(jnp.int32, (BLOCK, BLOCK), 1)
    assert WINDOW == BLOCK

    def scores(n, h):
        r0 = h * HEAD_DIM
        q4 = jnp.concatenate(
            [qt_ref[(h * GQA_GROUP + g) * HEAD_DIM:(h * GQA_GROUP + g + 1) * HEAD_DIM,
                    n * BLOCK:(n + 1) * BLOCK] for g in range(GQA_GROUP)], axis=1)
        parts = [q4]
        if r0 > 0:
            parts.insert(0, jnp.zeros((r0, GQA_GROUP * BLOCK), BF16))
        if r0 + HEAD_DIM < KV_WIDTH:
            parts.append(jnp.zeros((KV_WIDTH - r0 - HEAD_DIM, GQA_GROUP * BLOCK), BF16))
        qpad = jnp.concatenate(parts, axis=0)
        kwin = kfull[n * BLOCK:n * BLOCK + span, :]
        return jnp.dot(kwin, qpad, preferred_element_type=F32)

    def softmax(n, h, st_all):
        nglob = s_idx * r_blocks + n
        valid_prev = (key_i >= qry_i) & (nglob > 0)
        valid_next = (key_i <= qry_i) & (nglob < n_blocks - 1)
        pn, tail = [], []
        for g in range(GQA_GROUP):
            cols = slice(g * BLOCK, (g + 1) * BLOCK)
            s_prev = jnp.where(valid_prev, st_all[0:BLOCK, cols], NEG_INF)
            s_self = st_all[BLOCK:2 * BLOCK, cols]
            s_next = jnp.where(valid_next, st_all[2 * BLOCK:, cols], NEG_INF)
            sink = jnp.full((1, BLOCK), sink_ref[h * GQA_GROUP + g] * LOG2_E, F32)
            m = jnp.max(jnp.maximum(jnp.maximum(s_prev, s_self), s_next), axis=0, keepdims=True)
            m = jnp.maximum(m, sink)
            tail.append(jnp.exp2(sink - m))
            pn.append(jnp.concatenate([jnp.exp2(s_prev - m), jnp.exp2(s_self - m),
                                       jnp.exp2(s_next - m)], axis=0).astype(BF16))
        return jnp.concatenate(pn, axis=1), jnp.concatenate(tail, axis=1)

    def weighted_values(n, h, p, tail):
        r0 = h * HEAD_DIM
        vwin = vtfull[r0:r0 + HEAD_DIM, n * BLOCK:n * BLOCK + span]
        vext = jnp.concatenate([vwin, jnp.ones((ROW_PACK, span), BF16)], axis=0)
        ot = jnp.dot(vext, p, preferred_element_type=F32)
        den = ot[HEAD_DIM:HEAD_DIM + 1] + tail
        ot = ot[:HEAD_DIM] * (1.0 / den)
        return [ot[:, g * BLOCK:(g + 1) * BLOCK] for g in range(GQA_GROUP)]

    def finish_block(n, heads):
        y = jnp.concatenate(heads, axis=0)
        r = lax.rsqrt(jnp.mean(y * y, axis=0, keepdims=True) + EPS)
        o_ref[n * BLOCK:(n + 1) * BLOCK, :] = ((y * r).T * g_ref[...]).astype(BF16)

    pairs = [(n, h) for n in range(r_blocks) for h in range(N_KV_HEADS)]
    heads = []

    def retire(n, h, p, tail):
        heads.extend(weighted_values(n, h, p, tail))
        if h == N_KV_HEADS - 1:
            finish_block(n, list(heads))
            heads.clear()

    st_next = scores(*pairs[0])
    held = None
    for i, (n, h) in enumerate(pairs):
        st_cur = st_next
        if i + 1 < len(pairs):
            st_next = scores(*pairs[i + 1])
        if held is not None:
            retire(*held)
        held = (n, h) + softmax(n, h, st_cur)
    retire(*held)


def _window_attn(sink, qt, k, vt, g, tq):
    B, _, S = qt.shape
    n_blocks = S // BLOCK
    r_blocks = tq // BLOCK
    prev_blk = lambda s: jnp.maximum(s * r_blocks - 1, 0)
    next_blk = lambda s: jnp.minimum((s + 1) * r_blocks, n_blocks - 1)
    return pl.pallas_call(
        functools.partial(_window_attn_kernel, tq=tq, n_blocks=n_blocks),
        grid=(B, S // tq),
        in_specs=[
            pl.BlockSpec(memory_space=pltpu.SMEM),
            pl.BlockSpec((None, A_WIDTH, tq), lambda b, s: (b, 0, s)),
            pl.BlockSpec((None, tq, KV_WIDTH), lambda b, s: (b, s, 0)),
            pl.BlockSpec((None, BLOCK, KV_WIDTH), lambda b, s: (b, prev_blk(s), 0)),
            pl.BlockSpec((None, BLOCK, KV_WIDTH), lambda b, s: (b, next_blk(s), 0)),
            pl.BlockSpec((None, KV_WIDTH, tq), lambda b, s: (b, 0, s)),
            pl.BlockSpec((None, KV_WIDTH, BLOCK), lambda b, s: (b, 0, prev_blk(s))),
            pl.BlockSpec((None, KV_WIDTH, BLOCK), lambda b, s: (b, 0, next_blk(s))),
            pl.BlockSpec((1, A_WIDTH), lambda b, s: (0, 0)),
        ],
        out_specs=pl.BlockSpec((None, tq, A_WIDTH), lambda b, s: (b, s, 0)),
        out_shape=jax.ShapeDtypeStruct((B, S, A_WIDTH), BF16),
        scratch_shapes=[
            pltpu.VMEM((tq + 2 * BLOCK, KV_WIDTH), BF16),
            pltpu.VMEM((KV_WIDTH, tq + 2 * BLOCK), BF16),
        ],
        compiler_params=_params("parallel", "parallel"),
        name="window_attn",
    )(sink, qt, k, k, k, vt, vt, vt, g)


OUT_SUB = 2


def _out_proj_kernel(of_ref, oa_ref, x_ref, wof_ref, woa_ref, gof_ref, g_ref, x1_ref, h2_ref):
    sub = x_ref.shape[0] // OUT_SUB
    for i in range(OUT_SUB):
        rows = slice(i * sub, (i + 1) * sub)
        of = of_ref[rows, :].astype(F32)
        rf = lax.rsqrt(jnp.mean(of * of, axis=-1, keepdims=True) + EPS)
        of_n = ((of * rf) * gof_ref[...]).astype(BF16)
        y = jnp.dot(of_n, wof_ref[...], preferred_element_type=F32)
        y = y + jnp.dot(oa_ref[rows, :], woa_ref[...], preferred_element_type=F32)
        x1 = x_ref[rows, :] + y
        x1_ref[rows, :] = x1
        r = lax.rsqrt(jnp.mean(x1 * x1, axis=-1, keepdims=True) + EPS)
        h2_ref[rows, :] = ((x1 * r) * g_ref[...]).astype(BF16)


def _out_proj(of, oa_n, x, wof, woa, gof, g, tm):
    T = x.shape[0]
    return pl.pallas_call(
        _out_proj_kernel,
        grid=(T // tm,),
        in_specs=[
            pl.BlockSpec((tm, F_WIDTH), lambda i: (i, 0)),
            pl.BlockSpec((tm, A_WIDTH), lambda i: (i, 0)),
            pl.BlockSpec((tm, D_MODEL), lambda i: (i, 0)),
            _resident((F_WIDTH, D_MODEL)),
            _resident((A_WIDTH, D_MODEL)),
            _resident((1, F_WIDTH)),
            _resident((1, D_MODEL)),
        ],
        out_specs=[
            pl.BlockSpec((tm, D_MODEL), lambda i: (i, 0)),
            pl.BlockSpec((tm, D_MODEL), lambda i: (i, 0)),
        ],
        out_shape=[
            jax.ShapeDtypeStruct((T, D_MODEL), F32),
            jax.ShapeDtypeStruct((T, D_MODEL), BF16),
        ],
        compiler_params=_params("parallel"),
        name="out_proj",
    )(of, oa_n, x, wof, woa, gof, g)


MLP_CHUNKS = 2
LAST_SUB = 2


def _mlp_kernel(h2_ref, x1_ref, wu_ref, wd_ref, g_ref, o_ref):
    j = pl.program_id(1)
    last = pl.num_programs(1) - 1
    tm = h2_ref.shape[0]
    tf = wu_ref.shape[1]
    cuts = [(c * tf // MLP_CHUNKS, (c + 1) * tf // MLP_CHUNKS) for c in range(MLP_CHUNKS)]

    def contribution(rows, c):
        lo, hi = cuts[c]
        u = jnp.maximum(jnp.dot(h2_ref[rows, :], wu_ref[:, lo:hi], preferred_element_type=F32), 0.0)
        return jnp.dot((u * u).astype(BF16), wd_ref[lo:hi, :], preferred_element_type=F32)

    @pl.when(j == 0)
    def _():
        rows = slice(0, tm)
        o_ref[...] = x1_ref[...] + contribution(rows, 0)
        for c in range(1, MLP_CHUNKS):
            o_ref[...] += contribution(rows, c)

    @pl.when((j > 0) & (j < last))
    def _():
        rows = slice(0, tm)
        for c in range(MLP_CHUNKS):
            o_ref[...] += contribution(rows, c)

    @pl.when(j == last)
    def _():
        sub = tm // LAST_SUB
        for i in range(LAST_SUB):
            rows = slice(i * sub, (i + 1) * sub)
            y = o_ref[rows, :]
            for c in range(MLP_CHUNKS):
                y = y + contribution(rows, c)
            r = lax.rsqrt(jnp.mean(y * y, axis=-1, keepdims=True) + EPS)
            o_ref[rows, :] = (y * r) * g_ref[...]


def _mlp(h2, x1, wu, wd, g, tm, tf):
    T = h2.shape[0]
    assert D_FF // tf >= 2, "the first and last d_ff steps must be different steps"
    return pl.pallas_call(
        _mlp_kernel,
        grid=(T // tm, D_FF // tf),
        in_specs=[
            pl.BlockSpec((tm, D_MODEL), lambda i, j: (i, 0)),
            pl.BlockSpec((tm, D_MODEL), lambda i, j: (i, 0)),
            pl.BlockSpec((D_MODEL, tf), lambda i, j: (0, j)),
            pl.BlockSpec((tf, D_MODEL), lambda i, j: (j, 0)),
            _resident((1, D_MODEL)),
        ],
        out_specs=pl.BlockSpec((tm, D_MODEL), lambda i, j: (i, 0)),
        out_shape=jax.ShapeDtypeStruct((T, D_MODEL), F32),
        compiler_params=_params("parallel", "arbitrary"),
        name="mlp",
    )(h2, x1, wu, wd, g)


def _rope_tables(S):
    inv_freq = 1.0 / (ROPE_THETA ** (jnp.arange(0, HEAD_DIM, 2, dtype=F32) / HEAD_DIM))
    pos = jnp.arange(S, dtype=F32)
    ang = pos[:, None] * inv_freq[None, :]
    ang = jnp.concatenate([ang, ang], axis=-1)
    cos, sin = jnp.cos(ang), jnp.sin(ang)
    sign = jnp.where(jnp.arange(HEAD_DIM) < HALF_DIM, -1.0, 1.0).astype(F32)
    sin = sin * sign[None, :]
    reps = LANES // HEAD_DIM
    return cos.T, sin.T, jnp.tile(cos, (1, reps)), jnp.tile(sin, (1, reps))


def _seq_dft_matrices(S):
    half = S // 2
    lo_n = 64
    hi_n = half // lo_n
    k = jnp.arange(half, dtype=jnp.int32)
    w = 2.0 * math.pi / S
    a_hi = ((jnp.arange(hi_n + 1, dtype=jnp.int32)[:, None] * lo_n * k[None, :]) % S).astype(F32) * w
    a_lo = ((jnp.arange(lo_n, dtype=jnp.int32)[:, None] * k[None, :]) % S).astype(F32) * w
    ch, sh = jnp.cos(a_hi)[:, None, :], jnp.sin(a_hi)[:, None, :]
    cl, sl = jnp.cos(a_lo)[None, :, :], jnp.sin(a_lo)[None, :, :]
    scale = S ** -0.5
    cos = ((ch * cl - sh * sl) * scale).reshape(half + lo_n, half).astype(BF16)
    sin = ((sh[:hi_n] * cl + ch[:hi_n] * sl) * scale).reshape(half, half).astype(BF16)
    i = jnp.arange(REV, dtype=jnp.int32)
    jmat = ((i[:, None] + i[None, :]) == REV).astype(BF16)
    return cos, sin, jmat


def _trunk(x, w, tm_in, tq, dft_tn, tm_out, tm_mlp, tf):
    B, S, _ = x.shape
    cost, sint, cos, sin = _rope_tables(S)
    zf, qt, k, vt = _in_proj(x, w["g_mix"], w["wf"], w["wqt"], w["wk"], w["wvt"],
                             cost, sint, cos, sin, tm_in)
    of = _seq_dft(zf, *_seq_dft_matrices(S), w["mix"], dft_tn)
    oa_n = _window_attn(w["sink"], qt, k, vt, w["g_oa"], tq)
    T = B * S
    x2 = x.reshape(T, D_MODEL)
    x1, h2 = _out_proj(of.reshape(T, F_WIDTH), oa_n.reshape(T, A_WIDTH), x2,
                       w["wof"], w["woa"], w["g_of"], w["g_mlp"], tm_out)
    y = _mlp(h2, x1, w["wu"], w["wd"], w["g_final"], tm_mlp, tf)
    return y.reshape(B, S, D_MODEL)


def _prepare_weights(ln_mix_g, w_in, w_fourier, attn_sink, out_norm_fourier_g, out_norm_attn_g,
                     w_out, ln_mlp_g, w_up, w_down, ln_final_g):
    o1, o2, o3 = F_WIDTH, F_WIDTH + A_WIDTH, F_WIDTH + A_WIDTH + KV_WIDTH
    w_out_b = w_out.astype(BF16)
    return {
        "g_mix": ln_mix_g.reshape(1, D_MODEL),
        "wf": w_in[:, :o1].astype(BF16),
        "wqt": w_in[:, o1:o2].T.astype(BF16),
        "wk": w_in[:, o2:o3].astype(BF16),
        "wvt": w_in[:, o3:].T.astype(BF16),
        "mix": _fourier_weights(w_fourier),
        "sink": attn_sink.astype(F32),
        "g_of": out_norm_fourier_g.reshape(1, F_WIDTH),
        "g_oa": out_norm_attn_g.reshape(1, A_WIDTH),
        "wof": w_out_b[:F_WIDTH],
        "woa": w_out_b[F_WIDTH:],
        "g_mlp": ln_mlp_g.reshape(1, D_MODEL),
        "wu": w_up.astype(BF16),
        "wd": w_down.astype(BF16),
        "g_final": ln_final_g.reshape(1, D_MODEL),
    }


def kernel(x_prompt, x_sample, ln_mix_g, w_in, w_fourier, attn_sink, out_norm_fourier_g,
           out_norm_attn_g, w_out, ln_mlp_g, w_up, w_down, ln_final_g):
    assert ln_mix_g.shape[0] == 1, "single-layer block"
    w = _prepare_weights(ln_mix_g[0], w_in[0], w_fourier[0], attn_sink[0], out_norm_fourier_g[0],
                         out_norm_attn_g[0], w_out[0], ln_mlp_g[0], w_up[0], w_down[0], ln_final_g)
    tiles = dict(tm_in=1024, tq=1024, dft_tn=512, tm_out=512, tm_mlp=512, tf=2048)
    return (_trunk(x_prompt, w, **tiles), _trunk(x_sample, w, **tiles))
```

```python
import functools
import math

import jax
import jax.numpy as jnp
from jax import lax
from jax.experimental import pallas as pl
from jax.experimental.pallas import tpu as pltpu

D_MODEL = 2048
F_WIDTH = 1024
F_GROUPS = 8
F_CH = 128
A_WIDTH = 1024
HEAD_DIM = 64
HALF_DIM = HEAD_DIM // 2
N_Q_HEADS = 16
N_KV_HEADS = 4
GQA_GROUP = 4
KV_WIDTH = 256
WINDOW = 128
BLOCK = 128
ROPE_THETA = 10000.0
D_FF = 4 * D_MODEL
EPS = 1e-6
NEG_INF = -1e30
LOG2_E = math.log2(math.e)

LANES = 128
VMEM_LIMIT = 60 * 1024 * 1024

BF16 = jnp.bfloat16
F32 = jnp.float32

_NT = (((1,), (1,)), ((), ()))


def _params(*sem, flags=None):
    return pltpu.CompilerParams(dimension_semantics=sem, vmem_limit_bytes=VMEM_LIMIT, flags=flags)


def _resident(shape):
    nd = len(shape)
    return pl.BlockSpec(shape, lambda *_: (0,) * nd, pipeline_mode=pl.Buffered(1))


def _fourier_weights_kernel(cs_ref, w_ref, o_ref):
    for g in range(F_GROUPS):
        w = w_ref[g]
        c = jnp.dot(cs_ref[0], w, preferred_element_type=F32, precision=lax.Precision.HIGHEST)
        s = jnp.dot(cs_ref[1], w, preferred_element_type=F32, precision=lax.Precision.HIGHEST)
        o_ref[g, :F_CH, :F_CH] = c.astype(BF16)
        o_ref[g, :F_CH, F_CH:] = c.astype(BF16)
        o_ref[g, F_CH:, :F_CH] = (-s).astype(BF16)
        o_ref[g, F_CH:, F_CH:] = s.astype(BF16)


def _fourier_weights(w_fourier):
    n = jnp.arange(F_CH, dtype=jnp.int32)
    ang = ((n[:, None] * n[None, :]) % F_CH).astype(F32) * (2.0 * math.pi / F_CH)
    cs = jnp.stack([jnp.cos(ang), jnp.sin(ang)]) * (F_CH ** -0.5)
    return pl.pallas_call(
        _fourier_weights_kernel,
        out_shape=jax.ShapeDtypeStruct((F_GROUPS, 2 * F_CH, 2 * F_CH), BF16),
        name="fourier_weights",
    )(cs, w_fourier)


IN_SUB = 4


def _in_proj_kernel(x_ref, g_ref, wf_ref, wqt_ref, wk_ref, wvt_ref,
                    cost_ref, sint_ref, cos_ref, sin_ref,
                    zf_ref, qt_ref, k_ref, vt_ref):
    tm = x_ref.shape[0]
    sub = tm // IN_SUB
    scale = HEAD_DIM ** -0.5 * LOG2_E
    first_half = (lax.broadcasted_iota(jnp.int32, (sub, LANES), 1) % HEAD_DIM) < HALF_DIM
    for i in range(IN_SUB):
        rows = slice(i * sub, (i + 1) * sub)
        x = x_ref[rows, :]
        r = lax.rsqrt(jnp.mean(x * x, axis=-1, keepdims=True) + EPS)
        h = ((x * r) * g_ref[...]).astype(BF16)

        zf = jnp.dot(h, wf_ref[...], preferred_element_type=F32)
        qt = lax.dot_general(wqt_ref[...], h, _NT, preferred_element_type=F32)
        kk = jnp.dot(h, wk_ref[...], preferred_element_type=F32)
        vt = lax.dot_general(wvt_ref[...], h, _NT, preferred_element_type=F32)

        zf_ref[rows, :] = zf.astype(BF16)

        cost = cost_ref[:, rows]
        sint = sint_ref[:, rows]
        for hd in range(N_Q_HEADS):
            lo = hd * HEAD_DIM
            blk = qt[lo:lo + HEAD_DIM]
            rot = jnp.concatenate([blk[HALF_DIM:], blk[:HALF_DIM]], axis=0)
            qt_ref[lo:lo + HEAD_DIM, rows] = ((blk * cost + rot * sint) * scale).astype(BF16)

        cos = cos_ref[rows, :]
        sin = sin_ref[rows, :]
        for c in range(KV_WIDTH // LANES):
            blk = kk[:, c * LANES:(c + 1) * LANES]
            rot = jnp.where(first_half,
                            pltpu.roll(blk, LANES - HALF_DIM, axis=1),
                            pltpu.roll(blk, HALF_DIM, axis=1))
            k_ref[rows, c * LANES:(c + 1) * LANES] = (blk * cos + rot * sin).astype(BF16)

        vt_ref[:, rows] = vt.astype(BF16)


def _in_proj(x, g, wf, wqt, wk, wvt, cost, sint, cos, sin, tm):
    B, S, _ = x.shape
    return pl.pallas_call(
        _in_proj_kernel,
        grid=(B, S // tm),
        in_specs=[
            pl.BlockSpec((None, tm, D_MODEL), lambda b, s: (b, s, 0)),
            _resident((1, D_MODEL)),
            _resident((D_MODEL, F_WIDTH)),
            _resident((A_WIDTH, D_MODEL)),
            _resident((D_MODEL, KV_WIDTH)),
            _resident((KV_WIDTH, D_MODEL)),
            pl.BlockSpec((HEAD_DIM, tm), lambda b, s: (0, s)),
            pl.BlockSpec((HEAD_DIM, tm), lambda b, s: (0, s)),
            pl.BlockSpec((tm, LANES), lambda b, s: (s, 0)),
            pl.BlockSpec((tm, LANES), lambda b, s: (s, 0)),
        ],
        out_specs=[
            pl.BlockSpec((None, tm, F_WIDTH), lambda b, s: (b, s, 0)),
            pl.BlockSpec((None, None, A_WIDTH, tm), lambda b, s: (b, s, 0, 0)),
            pl.BlockSpec((None, tm, KV_WIDTH), lambda b, s: (b, s, 0)),
            pl.BlockSpec((None, None, KV_WIDTH, tm), lambda b, s: (b, s, 0, 0)),
        ],
        out_shape=[
            jax.ShapeDtypeStruct((B, S, F_WIDTH), BF16),
            jax.ShapeDtypeStruct((B, S // tm, A_WIDTH, tm), BF16),
            jax.ShapeDtypeStruct((B, S, KV_WIDTH), BF16),
            jax.ShapeDtypeStruct((B, S // tm, KV_WIDTH, tm), BF16),
        ],
        compiler_params=_params("parallel", "parallel"),
        name="in_proj",
    )(x, g, wf, wqt, wk, wvt, cost, sint, cos, sin)


REV = 256
ROW_PACK = 16
DFT_ROWS = 512


def _first_row(x):
    return jnp.where(lax.broadcasted_iota(jnp.int32, x.shape, 0) == 0, x, 0.0)


def _patch_first_row(tile, row):
    head = tile[:ROW_PACK] + _first_row(row)
    return jnp.concatenate([head, tile[ROW_PACK:]], axis=0)


def _seq_dft_kernel(z_ref, c_ref, s_ref, j_ref, mix_ref, o_ref, zp_ref, zm_ref, u_ref, *, seq):
    half = seq // 2
    n_rev = half // REV
    n_grp = z_ref.shape[1] // F_CH
    jmat = j_ref[...]

    for t in range(n_rev):
        lo = t * REV
        src = seq - lo - REV
        r = jnp.dot(jmat, z_ref[src:src + REV, :], preferred_element_type=F32)
        if t > 0:
            r = _patch_first_row(r, z_ref[seq - lo:seq - lo + ROW_PACK, :].astype(F32))
        z_lo = z_ref[lo:lo + REV, :].astype(F32)
        zp_ref[lo:lo + REV, :] = (z_lo + r).astype(BF16)
        zm_ref[lo:lo + REV, :] = (z_lo - r).astype(BF16)

    z_half = z_ref[half:half + ROW_PACK, :].astype(F32)[0:1, :] * (seq ** -0.5)
    row_par = lax.broadcasted_iota(jnp.int32, (DFT_ROWS, 1), 0) % 2
    sgn_z_half = jnp.where(row_par == 0, z_half, -z_half)

    zp = zp_ref[...]
    zm = zm_ref[...]
    n_row = half // DFT_ROWS

    def spectra(i):
        lo = i * DFT_ROWS
        extra = ROW_PACK if i == n_row - 1 else 0
        p = jnp.dot(c_ref[lo:lo + DFT_ROWS + extra, :], zp, preferred_element_type=F32)
        q = jnp.dot(s_ref[lo:lo + DFT_ROWS, :], zm, preferred_element_type=F32)
        if extra:
            p = jnp.concatenate([p[:DFT_ROWS] + sgn_z_half, p[DFT_ROWS:] + z_half], axis=0)
            q = jnp.concatenate([q, jnp.zeros((extra, q.shape[1]), F32)], axis=0)
        else:
            p = p + sgn_z_half
        return p.astype(BF16), q.astype(BF16)

    nyq = []

    def mix(i, pq):
        p, q = pq
        lo = i * DFT_ROWS
        for g in range(n_grp):
            cols = slice(g * F_CH, (g + 1) * F_CH)
            lu = jnp.dot(jnp.concatenate([p[:, cols], q[:, cols]], axis=1), mix_ref[g],
                         preferred_element_type=F32)
            o_ref[lo:lo + DFT_ROWS, cols] = lu[:DFT_ROWS, :F_CH].astype(BF16)
            u_ref[lo:lo + DFT_ROWS, cols] = lu[:DFT_ROWS, F_CH:].astype(BF16)
            if lu.shape[0] > DFT_ROWS:
                nyq.append(lu[DFT_ROWS:, :F_CH])

    pending = spectra(0)
    for i in range(n_row):
        cur = pending
        if i + 1 < n_row:
            pending = spectra(i + 1)
        mix(i, cur)
    nyq_row = jnp.concatenate(nyq, axis=1)

    for t in range(n_rev):
        src = half - (t + 1) * REV
        r = jnp.dot(jmat, u_ref[src:src + REV, :], preferred_element_type=F32)
        if t == 0:
            r = _patch_first_row(r, nyq_row)
        else:
            r = _patch_first_row(r, u_ref[src + REV:src + REV + ROW_PACK, :].astype(F32))
        o_ref[half + t * REV:half + (t + 1) * REV, :] = r.astype(BF16)


def _seq_dft(zf, cmat, smat, jmat, mix, tn):
    B, S, _ = zf.shape
    half = S // 2
    assert half % DFT_ROWS == 0 and half % 2 == 0 and tn % F_CH == 0
    return pl.pallas_call(
        functools.partial(_seq_dft_kernel, seq=S),
        grid=(B, F_WIDTH // tn),
        in_specs=[
            pl.BlockSpec((None, S, tn), lambda b, c: (b, 0, c)),
            pl.BlockSpec((half + ROW_PACK, half), lambda b, c: (0, 0), pipeline_mode=pl.Buffered(1)),
            _resident((half, half)),
            _resident((REV, REV)),
            pl.BlockSpec((tn // F_CH, 2 * F_CH, 2 * F_CH), lambda b, c: (c, 0, 0)),
        ],
        out_specs=pl.BlockSpec((None, S, tn), lambda b, c: (b, 0, c)),
        out_shape=jax.ShapeDtypeStruct((B, S, F_WIDTH), BF16),
        scratch_shapes=[pltpu.VMEM((half, tn), BF16),
                        pltpu.VMEM((half, tn), BF16),
                        pltpu.VMEM((half, tn), BF16)],
        compiler_params=_params("parallel", "parallel"),
        name="seq_dft",
    )(zf, cmat, smat, jmat, mix)


def _window_attn_kernel(sink_ref, qt_ref, k_ref, kp_ref, kn_ref, vt_ref, vtp_ref, vtn_ref, g_ref,
                        o_ref, kfull, vtfull, *, tq, n_blocks):
    s_idx = pl.program_id(1)
    r_blocks = tq // BLOCK
    span = 3 * BLOCK

    kfull[0:BLOCK, :] = kp_ref[...]
    kfull[BLOCK:BLOCK + tq, :] = k_ref[...]
    kfull[BLOCK + tq:, :] = kn_ref[...]
    vtfull[:, 0:BLOCK] = vtp_ref[...]
    vtfull[:, BLOCK:BLOCK + tq] = vt_ref[...]
    vtfull[:, BLOCK + tq:] = vtn_ref[...]

    key_i = lax.broadcasted_iota(jnp.int32, (BLOCK, BLOCK), 0)
    qry_i = lax.broadcasted_iota(jnp.int32, (BLOCK, BLOCK), 1)
    assert WINDOW == BLOCK

    def scores(n, h):
        r0 = h * HEAD_DIM
        q4 = jnp.concatenate(
            [qt_ref[(h * GQA_GROUP + g) * HEAD_DIM:(h * GQA_GROUP + g + 1) * HEAD_DIM,
                    n * BLOCK:(n + 1) * BLOCK] for g in range(GQA_GROUP)], axis=1)
        parts = [q4]
        if r0 > 0:
            parts.insert(0, jnp.zeros((r0, GQA_GROUP * BLOCK), BF16))
        if r0 + HEAD_DIM < KV_WIDTH:
            parts.append(jnp.zeros((KV_WIDTH - r0 - HEAD_DIM, GQA_GROUP * BLOCK), BF16))
        qpad = jnp.concatenate(parts, axis=0)
        kwin = kfull[n * BLOCK:n * BLOCK + span, :]
        return jnp.dot(kwin, qpad, preferred_element_type=F32)

    def softmax(n, h, st_all):
        nglob = s_idx * r_blocks + n
        valid_prev = (key_i >= qry_i) & (nglob > 0)
        valid_next = (key_i <= qry_i) & (nglob < n_blocks - 1)
        pn, tail = [], []
        for g in range(GQA_GROUP):
            cols = slice(g * BLOCK, (g + 1) * BLOCK)
            s_prev = jnp.where(valid_prev, st_all[0:BLOCK, cols], NEG_INF)
            s_self = st_all[BLOCK:2 * BLOCK, cols]
            s_next = jnp.where(valid_next, st_all[2 * BLOCK:, cols], NEG_INF)
            sink = jnp.full((1, BLOCK), sink_ref[h * GQA_GROUP + g] * LOG2_E, F32)
            m = jnp.max(jnp.maximum(jnp.maximum(s_prev, s_self), s_next), axis=0, keepdims=True)
            m = jnp.maximum(m, sink)
            tail.append(jnp.exp2(sink - m))
            pn.append(jnp.concatenate([jnp.exp2(s_prev - m), jnp.exp2(s_self - m),
                                       jnp.exp2(s_next - m)], axis=0).astype(BF16))
        return jnp.concatenate(pn, axis=1), jnp.concatenate(tail, axis=1)

    def weighted_values(n, h, p, tail):
        r0 = h * HEAD_DIM
        vwin = vtfull[r0:r0 + HEAD_DIM, n * BLOCK:n * BLOCK + span]
        vext = jnp.concatenate([vwin, jnp.ones((ROW_PACK, span), BF16)], axis=0)
        ot = jnp.dot(vext, p, preferred_element_type=F32)
        den = ot[HEAD_DIM:HEAD_DIM + 1] + tail
        ot = ot[:HEAD_DIM] * (1.0 / den)
        return [ot[:, g * BLOCK:(g + 1) * BLOCK] for g in range(GQA_GROUP)]

    def finish_block(n, heads):
        y = jnp.concatenate(heads, axis=0)
        r = lax.rsqrt(jnp.mean(y * y, axis=0, keepdims=True) + EPS)
        o_ref[n * BLOCK:(n + 1) * BLOCK, :] = ((y * r).T * g_ref[...]).astype(BF16)

    pairs = [(n, h) for n in range(r_blocks) for h in range(N_KV_HEADS)]
    heads = []

    def retire(n, h, p, tail):
        heads.extend(weighted_values(n, h, p, tail))
        if h == N_KV_HEADS - 1:
            finish_block(n, list(heads))
            heads.clear()

    st_next = scores(*pairs[0])
    held = None
    for i, (n, h) in enumerate(pairs):
        st_cur = st_next
        if i + 1 < len(pairs):
            st_next = scores(*pairs[i + 1])
        if held is not None:
            retire(*held)
        held = (n, h) + softmax(n, h, st_cur)
    retire(*held)


def _window_attn(sink, qt, k, vt, g, tq):
    B, n_tiles, _, _ = qt.shape
    assert qt.shape[3] == tq and vt.shape[3] == tq
    S = n_tiles * tq
    n_blocks = S // BLOCK
    r_blocks = tq // BLOCK
    prev_blk = lambda s: jnp.maximum(s * r_blocks - 1, 0)
    next_blk = lambda s: jnp.minimum((s + 1) * r_blocks, n_blocks - 1)
    prev_tile = lambda s: jnp.maximum(s - 1, 0)
    next_tile = lambda s: jnp.minimum(s + 1, n_tiles - 1)
    return pl.pallas_call(
        functools.partial(_window_attn_kernel, tq=tq, n_blocks=n_blocks),
        grid=(B, n_tiles),
        in_specs=[
            pl.BlockSpec(memory_space=pltpu.SMEM),
            pl.BlockSpec((None, None, A_WIDTH, tq), lambda b, s: (b, s, 0, 0)),
            pl.BlockSpec((None, tq, KV_WIDTH), lambda b, s: (b, s, 0)),
            pl.BlockSpec((None, BLOCK, KV_WIDTH), lambda b, s: (b, prev_blk(s), 0)),
            pl.BlockSpec((None, BLOCK, KV_WIDTH), lambda b, s: (b, next_blk(s), 0)),
            pl.BlockSpec((None, None, KV_WIDTH, tq), lambda b, s: (b, s, 0, 0)),
            pl.BlockSpec((None, None, KV_WIDTH, BLOCK), lambda b, s: (b, prev_tile(s), 0, r_blocks - 1)),
            pl.BlockSpec((None, None, KV_WIDTH, BLOCK), lambda b, s: (b, next_tile(s), 0, 0)),
            pl.BlockSpec((1, A_WIDTH), lambda b, s: (0, 0)),
        ],
        out_specs=pl.BlockSpec((None, tq, A_WIDTH), lambda b, s: (b, s, 0)),
        out_shape=jax.ShapeDtypeStruct((B, S, A_WIDTH), BF16),
        scratch_shapes=[
            pltpu.VMEM((tq + 2 * BLOCK, KV_WIDTH), BF16),
            pltpu.VMEM((KV_WIDTH, tq + 2 * BLOCK), BF16),
        ],
        compiler_params=_params("parallel", "parallel"),
        name="window_attn",
    )(sink, qt, k, k, k, vt, vt, vt, g)


OUT_SUB = 2


def _out_proj_kernel(of_ref, oa_ref, x_ref, wof_ref, woa_ref, gof_ref, g_ref, x1_ref, h2_ref):
    sub = x_ref.shape[0] // OUT_SUB
    for i in range(OUT_SUB):
        rows = slice(i * sub, (i + 1) * sub)
        of = of_ref[rows, :].astype(F32)
        rf = lax.rsqrt(jnp.mean(of * of, axis=-1, keepdims=True) + EPS)
        of_n = ((of * rf) * gof_ref[...]).astype(BF16)
        y = jnp.dot(of_n, wof_ref[...], preferred_element_type=F32)
        y = y + jnp.dot(oa_ref[rows, :], woa_ref[...], preferred_element_type=F32)
        x1 = x_ref[rows, :] + y
        x1_ref[rows, :] = x1
        r = lax.rsqrt(jnp.mean(x1 * x1, axis=-1, keepdims=True) + EPS)
        h2_ref[rows, :] = ((x1 * r) * g_ref[...]).astype(BF16)


def _out_proj(of, oa_n, x, wof, woa, gof, g, tm):
    T = x.shape[0]
    return pl.pallas_call(
        _out_proj_kernel,
        grid=(T // tm,),
        in_specs=[
            pl.BlockSpec((tm, F_WIDTH), lambda i: (i, 0)),
            pl.BlockSpec((tm, A_WIDTH), lambda i: (i, 0)),
            pl.BlockSpec((tm, D_MODEL), lambda i: (i, 0)),
            _resident((F_WIDTH, D_MODEL)),
            _resident((A_WIDTH, D_MODEL)),
            _resident((1, F_WIDTH)),
            _resident((1, D_MODEL)),
        ],
        out_specs=[
            pl.BlockSpec((tm, D_MODEL), lambda i: (i, 0)),
            pl.BlockSpec((tm, D_MODEL), lambda i: (i, 0)),
        ],
        out_shape=[
            jax.ShapeDtypeStruct((T, D_MODEL), F32),
            jax.ShapeDtypeStruct((T, D_MODEL), BF16),
        ],
        compiler_params=_params("parallel"),
        name="out_proj",
    )(of, oa_n, x, wof, woa, gof, g)


MLP_CHUNKS = 2
LAST_SUB = 2


def _mlp_kernel(h2_ref, x1_ref, wu_ref, wd_ref, g_ref, o_ref):
    j = pl.program_id(1)
    last = pl.num_programs(1) - 1
    tm = h2_ref.shape[0]
    tf = wu_ref.shape[1]
    cuts = [(c * tf // MLP_CHUNKS, (c + 1) * tf // MLP_CHUNKS) for c in range(MLP_CHUNKS)]

    def contribution(rows, c):
        lo, hi = cuts[c]
        u = jnp.maximum(jnp.dot(h2_ref[rows, :], wu_ref[:, lo:hi], preferred_element_type=F32), 0.0)
        return jnp.dot((u * u).astype(BF16), wd_ref[lo:hi, :], preferred_element_type=F32)

    @pl.when(j == 0)
    def _():
        rows = slice(0, tm)
        o_ref[...] = x1_ref[...] + contribution(rows, 0)
        for c in range(1, MLP_CHUNKS):
            o_ref[...] += contribution(rows, c)

    @pl.when((j > 0) & (j < last))
    def _():
        rows = slice(0, tm)
        for c in range(MLP_CHUNKS):
            o_ref[...] += contribution(rows, c)

    @pl.when(j == last)
    def _():
        sub = tm // LAST_SUB
        for i in range(LAST_SUB):
            rows = slice(i * sub, (i + 1) * sub)
            y = o_ref[rows, :]
            for c in range(MLP_CHUNKS):
                y = y + contribution(rows, c)
            r = lax.rsqrt(jnp.mean(y * y, axis=-1, keepdims=True) + EPS)
            o_ref[rows, :] = (y * r) * g_ref[...]


def _mlp(h2, x1, wu, wd, g, tm, tf):
    T = h2.shape[0]
    assert D_FF // tf >= 2, "the first and last d_ff steps must be different steps"
    return pl.pallas_call(
        _mlp_kernel,
        grid=(T // tm, D_FF // tf),
        in_specs=[
            pl.BlockSpec((tm, D_MODEL), lambda i, j: (i, 0)),
            pl.BlockSpec((tm, D_MODEL), lambda i, j: (i, 0)),
            pl.BlockSpec((D_MODEL, tf), lambda i, j: (0, j)),
            pl.BlockSpec((tf, D_MODEL), lambda i, j: (j, 0)),
            _resident((1, D_MODEL)),
        ],
        out_specs=pl.BlockSpec((tm, D_MODEL), lambda i, j: (i, 0)),
        out_shape=jax.ShapeDtypeStruct((T, D_MODEL), F32),
        compiler_params=_params("parallel", "arbitrary"),
        name="mlp",
    )(h2, x1, wu, wd, g)


def _rope_tables(S):
    inv_freq = 1.0 / (ROPE_THETA ** (jnp.arange(0, HEAD_DIM, 2, dtype=F32) / HEAD_DIM))
    pos = jnp.arange(S, dtype=F32)
    ang = pos[:, None] * inv_freq[None, :]
    ang = jnp.concatenate([ang, ang], axis=-1)
    cos, sin = jnp.cos(ang), jnp.sin(ang)
    sign = jnp.where(jnp.arange(HEAD_DIM) < HALF_DIM, -1.0, 1.0).astype(F32)
    sin = sin * sign[None, :]
    reps = LANES // HEAD_DIM
    return cos.T, sin.T, jnp.tile(cos, (1, reps)), jnp.tile(sin, (1, reps))


def _seq_dft_matrices(S):
    half = S // 2
    lo_n = 64
    hi_n = half // lo_n
    k = jnp.arange(half, dtype=jnp.int32)
    w = 2.0 * math.pi / S
    a_hi = ((jnp.arange(hi_n + 1, dtype=jnp.int32)[:, None] * lo_n * k[None, :]) % S).astype(F32) * w
    a_lo = ((jnp.arange(lo_n, dtype=jnp.int32)[:, None] * k[None, :]) % S).astype(F32) * w
    ch, sh = jnp.cos(a_hi)[:, None, :], jnp.sin(a_hi)[:, None, :]
    cl, sl = jnp.cos(a_lo)[None, :, :], jnp.sin(a_lo)[None, :, :]
    scale = S ** -0.5
    cos = ((ch * cl - sh * sl) * scale).reshape(half + lo_n, half).astype(BF16)
    sin = ((sh[:hi_n] * cl + ch[:hi_n] * sl) * scale).reshape(half, half).astype(BF16)
    i = jnp.arange(REV, dtype=jnp.int32)
    jmat = ((i[:, None] + i[None, :]) == REV).astype(BF16)
    return cos, sin, jmat


def _trunk(x, w, tm_in, tq, dft_tn, tm_out, tm_mlp, tf):
    B, S, _ = x.shape
    cost, sint, cos, sin = _rope_tables(S)
    zf, qt, k, vt = _in_proj(x, w["g_mix"], w["wf"], w["wqt"], w["wk"], w["wvt"],
                             cost, sint, cos, sin, tm_in)
    of = _seq_dft(zf, *_seq_dft_matrices(S), w["mix"], dft_tn)
    oa_n = _window_attn(w["sink"], qt, k, vt, w["g_oa"], tq)
    T = B * S
    x2 = x.reshape(T, D_MODEL)
    x1, h2 = _out_proj(of.reshape(T, F_WIDTH), oa_n.reshape(T, A_WIDTH), x2,
                       w["wof"], w["woa"], w["g_of"], w["g_mlp"], tm_out)
    y = _mlp(h2, x1, w["wu"], w["wd"], w["g_final"], tm_mlp, tf)
    return y.reshape(B, S, D_MODEL)


def _prepare_weights(ln_mix_g, w_in, w_fourier, attn_sink, out_norm_fourier_g, out_norm_attn_g,
                     w_out, ln_mlp_g, w_up, w_down, ln_final_g):
    o1, o2, o3 = F_WIDTH, F_WIDTH + A_WIDTH, F_WIDTH + A_WIDTH + KV_WIDTH
    w_out_b = w_out.astype(BF16)
    return {
        "g_mix": ln_mix_g.reshape(1, D_MODEL),
        "wf": w_in[:, :o1].astype(BF16),
        "wqt": w_in[:, o1:o2].T.astype(BF16),
        "wk": w_in[:, o2:o3].astype(BF16),
        "wvt": w_in[:, o3:].T.astype(BF16),
        "mix": _fourier_weights(w_fourier),
        "sink": attn_sink.astype(F32),
        "g_of": out_norm_fourier_g.reshape(1, F_WIDTH),
        "g_oa": out_norm_attn_g.reshape(1, A_WIDTH),
        "wof": w_out_b[:F_WIDTH],
        "woa": w_out_b[F_WIDTH:],
        "g_mlp": ln_mlp_g.reshape(1, D_MODEL),
        "wu": w_up.astype(BF16),
        "wd": w_down.astype(BF16),
        "g_final": ln_final_g.reshape(1, D_MODEL),
    }


def kernel(x_prompt, x_sample, ln_mix_g, w_in, w_fourier, attn_sink, out_norm_fourier_g,
           out_norm_attn_g, w_out, ln_mlp_g, w_up, w_down, ln_final_g):
    assert ln_mix_g.shape[0] == 1, "single-layer block"
    w = _prepare_weights(ln_mix_g[0], w_in[0], w_fourier[0], attn_sink[0], out_norm_fourier_g[0],
                         out_norm_attn_g[0], w_out[0], ln_mlp_g[0], w_up[0], w_down[0], ln_final_g)
    tiles = dict(tm_in=1024, tq=1024, dft_tn=512, tm_out=512, tm_mlp=512, tf=2048)
    return (_trunk(x_prompt, w, **tiles), _trunk(x_sample, w, **tiles))
```

```python
import functools
import math

import jax
import jax.numpy as jnp
from jax import lax
from jax.experimental import pallas as pl
from jax.experimental.pallas import tpu as pltpu

D_MODEL = 2048
F_WIDTH = 1024
F_GROUPS = 8
F_CH = 128
A_WIDTH = 1024
HEAD_DIM = 64
HALF_DIM = HEAD_DIM // 2
N_Q_HEADS = 16
N_KV_HEADS = 4
GQA_GROUP = 4
KV_WIDTH = 256
WINDOW = 128
BLOCK = 128
ROPE_THETA = 10000.0
D_FF = 4 * D_MODEL
EPS = 1e-6
NEG_INF = -1e30
LOG2_E = math.log2(math.e)

LANES = 128
ROW_PACK = 16
VMEM_LIMIT = 60 * 1024 * 1024

BF16 = jnp.bfloat16
F32 = jnp.float32

_NT = (((1,), (1,)), ((), ()))


def _params(*sem, flags=None):
    return pltpu.CompilerParams(dimension_semantics=sem, vmem_limit_bytes=VMEM_LIMIT, flags=flags)


def _resident(shape):
    nd = len(shape)
    return pl.BlockSpec(shape, lambda *_: (0,) * nd, pipeline_mode=pl.Buffered(1))


def _fourier_weights_kernel(cs_ref, w_ref, o_ref):
    for g in range(F_GROUPS):
        w = w_ref[g]
        c = jnp.dot(cs_ref[0], w, preferred_element_type=F32, precision=lax.Precision.HIGHEST)
        s = jnp.dot(cs_ref[1], w, preferred_element_type=F32, precision=lax.Precision.HIGHEST)
        o_ref[g, :F_CH, :F_CH] = c.astype(BF16)
        o_ref[g, :F_CH, F_CH:] = c.astype(BF16)
        o_ref[g, F_CH:, :F_CH] = (-s).astype(BF16)
        o_ref[g, F_CH:, F_CH:] = s.astype(BF16)


def _fourier_weights(w_fourier):
    n = jnp.arange(F_CH, dtype=jnp.int32)
    ang = ((n[:, None] * n[None, :]) % F_CH).astype(F32) * (2.0 * math.pi / F_CH)
    cs = jnp.stack([jnp.cos(ang), jnp.sin(ang)]) * (F_CH ** -0.5)
    return pl.pallas_call(
        _fourier_weights_kernel,
        out_shape=jax.ShapeDtypeStruct((F_GROUPS, 2 * F_CH, 2 * F_CH), BF16),
        name="fourier_weights",
    )(cs, w_fourier)


IN_SUB = 4


def _in_proj_kernel(x_ref, g_ref, wf_ref, wqt_ref, wk_ref, wvt_ref,
                    cost_ref, sint_ref, cos_ref, sin_ref, *rest, cast_mlp_weights):
    if cast_mlp_weights:
        wu32_ref, wd32_ref, zf_ref, qt_ref, k_ref, vt_ref, wu16_ref, wd16_ref = rest
        wu16_ref[...] = wu32_ref[...].astype(BF16)
        wd16_ref[...] = wd32_ref[...].astype(BF16)
    else:
        zf_ref, qt_ref, k_ref, vt_ref = rest
    tm = x_ref.shape[0]
    sub = tm // IN_SUB
    scale = HEAD_DIM ** -0.5 * LOG2_E
    first_half = (lax.broadcasted_iota(jnp.int32, (sub, LANES), 1) % HEAD_DIM) < HALF_DIM
    for i in range(IN_SUB):
        rows = slice(i * sub, (i + 1) * sub)
        x = x_ref[rows, :]
        r = lax.rsqrt(jnp.mean(x * x, axis=-1, keepdims=True) + EPS)
        h = ((x * r) * g_ref[...]).astype(BF16)

        zf = jnp.dot(h, wf_ref[...], preferred_element_type=F32)
        qt = lax.dot_general(wqt_ref[...], h, _NT, preferred_element_type=F32)
        kk = jnp.dot(h, wk_ref[...], preferred_element_type=F32)
        vt = lax.dot_general(wvt_ref[...], h, _NT, preferred_element_type=F32)

        zf_ref[rows, :] = zf.astype(BF16)

        cost = cost_ref[:, rows]
        sint = sint_ref[:, rows]
        for hd in range(N_Q_HEADS):
            lo = hd * HEAD_DIM
            blk = qt[lo:lo + HEAD_DIM]
            rot = jnp.concatenate([blk[HALF_DIM:], blk[:HALF_DIM]], axis=0)
            qt_ref[lo:lo + HEAD_DIM, rows] = ((blk * cost + rot * sint) * scale).astype(BF16)

        cos = cos_ref[rows, :]
        sin = sin_ref[rows, :]
        for c in range(KV_WIDTH // LANES):
            blk = kk[:, c * LANES:(c + 1) * LANES]
            rot = jnp.where(first_half,
                            pltpu.roll(blk, LANES - HALF_DIM, axis=1),
                            pltpu.roll(blk, HALF_DIM, axis=1))
            k_ref[rows, c * LANES:(c + 1) * LANES] = (blk * cos + rot * sin).astype(BF16)

        vt_ref[:, rows] = vt.astype(BF16)


def _in_proj(x, g, wf, wqt, wk, wvt, cost, sint, cos, sin, tm, mlp_weights=None):
    B, S, _ = x.shape
    n_s = S // tm
    in_specs = [
        pl.BlockSpec((None, tm, D_MODEL), lambda b, s: (b, s, 0)),
        _resident((1, D_MODEL)),
        _resident((D_MODEL, F_WIDTH)),
        _resident((A_WIDTH, D_MODEL)),
        _resident((D_MODEL, KV_WIDTH)),
        _resident((KV_WIDTH, D_MODEL)),
        pl.BlockSpec((HEAD_DIM, tm), lambda b, s: (0, s)),
        pl.BlockSpec((HEAD_DIM, tm), lambda b, s: (0, s)),
        pl.BlockSpec((tm, LANES), lambda b, s: (s, 0)),
        pl.BlockSpec((tm, LANES), lambda b, s: (s, 0)),
    ]
    out_specs = [
        pl.BlockSpec((None, tm, F_WIDTH), lambda b, s: (b, s, 0)),
        pl.BlockSpec((None, A_WIDTH, tm), lambda b, s: (b, 0, s)),
        pl.BlockSpec((None, tm, KV_WIDTH), lambda b, s: (b, s, 0)),
        pl.BlockSpec((None, KV_WIDTH, tm), lambda b, s: (b, 0, s)),
    ]
    out_shape = [
        jax.ShapeDtypeStruct((B, S, F_WIDTH), BF16),
        jax.ShapeDtypeStruct((B, A_WIDTH, S), BF16),
        jax.ShapeDtypeStruct((B, S, KV_WIDTH), BF16),
        jax.ShapeDtypeStruct((B, KV_WIDTH, S), BF16),
    ]
    args = [x, g, wf, wqt, wk, wvt, cost, sint, cos, sin]
    if mlp_weights is not None:
        n_steps = B * n_s
        for wgt in mlp_weights:
            rows, cols = wgt.shape
            assert rows % (n_steps * ROW_PACK) == 0
            spec = pl.BlockSpec((rows // n_steps, cols), lambda b, s: (b * n_s + s, 0))
            in_specs.append(spec)
            out_specs.append(spec)
            out_shape.append(jax.ShapeDtypeStruct(wgt.shape, BF16))
            args.append(wgt)
    return pl.pallas_call(
        functools.partial(_in_proj_kernel, cast_mlp_weights=mlp_weights is not None),
        grid=(B, n_s),
        in_specs=in_specs,
        out_specs=out_specs,
        out_shape=out_shape,
        compiler_params=_params("parallel", "parallel"),
        name="in_proj",
    )(*args)


REV = 256
DFT_ROWS = 512


def _first_row(x):
    return jnp.where(lax.broadcasted_iota(jnp.int32, x.shape, 0) == 0, x, 0.0)


def _patch_first_row(tile, row):
    head = tile[:ROW_PACK] + _first_row(row)
    return jnp.concatenate([head, tile[ROW_PACK:]], axis=0)


def _seq_dft_kernel(z_ref, c_ref, s_ref, j_ref, mix_ref, o_ref, zp_ref, zm_ref, u_ref, *, seq):
    half = seq // 2
    n_rev = half // REV
    n_grp = z_ref.shape[1] // F_CH
    jmat = j_ref[...]

    for t in range(n_rev):
        lo = t * REV
        src = seq - lo - REV
        r = jnp.dot(jmat, z_ref[src:src + REV, :], preferred_element_type=F32)
        if t > 0:
            r = _patch_first_row(r, z_ref[seq - lo:seq - lo + ROW_PACK, :].astype(F32))
        z_lo = z_ref[lo:lo + REV, :].astype(F32)
        zp_ref[lo:lo + REV, :] = (z_lo + r).astype(BF16)
        zm_ref[lo:lo + REV, :] = (z_lo - r).astype(BF16)

    z_half = z_ref[half:half + ROW_PACK, :].astype(F32)[0:1, :] * (seq ** -0.5)
    row_par = lax.broadcasted_iota(jnp.int32, (DFT_ROWS, 1), 0) % 2
    sgn_z_half = jnp.where(row_par == 0, z_half, -z_half)

    zp = zp_ref[...]
    zm = zm_ref[...]
    n_row = half // DFT_ROWS

    def spectra(i):
        lo = i * DFT_ROWS
        extra = ROW_PACK if i == n_row - 1 else 0
        p = jnp.dot(c_ref[lo:lo + DFT_ROWS + extra, :], zp, preferred_element_type=F32)
        q = jnp.dot(s_ref[lo:lo + DFT_ROWS, :], zm, preferred_element_type=F32)
        if extra:
            p = jnp.concatenate([p[:DFT_ROWS] + sgn_z_half, p[DFT_ROWS:] + z_half], axis=0)
            q = jnp.concatenate([q, jnp.zeros((extra, q.shape[1]), F32)], axis=0)
        else:
            p = p + sgn_z_half
        return p.astype(BF16), q.astype(BF16)

    nyq = []

    def mix(i, pq):
        p, q = pq
        lo = i * DFT_ROWS
        for g in range(n_grp):
            cols = slice(g * F_CH, (g + 1) * F_CH)
            lu = jnp.dot(jnp.concatenate([p[:, cols], q[:, cols]], axis=1), mix_ref[g],
                         preferred_element_type=F32)
            o_ref[lo:lo + DFT_ROWS, cols] = lu[:DFT_ROWS, :F_CH].astype(BF16)
            u_ref[lo:lo + DFT_ROWS, cols] = lu[:DFT_ROWS, F_CH:].astype(BF16)
            if lu.shape[0] > DFT_ROWS:
                nyq.append(lu[DFT_ROWS:, :F_CH])

    pending = spectra(0)
    for i in range(n_row):
        cur = pending
        if i + 1 < n_row:
            pending = spectra(i + 1)
        mix(i, cur)
    nyq_row = jnp.concatenate(nyq, axis=1)

    for t in range(n_rev):
        src = half - (t + 1) * REV
        r = jnp.dot(jmat, u_ref[src:src + REV, :], preferred_element_type=F32)
        if t == 0:
            r = _patch_first_row(r, nyq_row)
        else:
            r = _patch_first_row(r, u_ref[src + REV:src + REV + ROW_PACK, :].astype(F32))
        o_ref[half + t * REV:half + (t + 1) * REV, :] = r.astype(BF16)


def _seq_dft(zf, cmat, smat, jmat, mix, tn):
    B, S, _ = zf.shape
    half = S // 2
    assert half % DFT_ROWS == 0 and half % 2 == 0 and tn % F_CH == 0
    return pl.pallas_call(
        functools.partial(_seq_dft_kernel, seq=S),
        grid=(B, F_WIDTH // tn),
        in_specs=[
            pl.BlockSpec((None, S, tn), lambda b, c: (b, 0, c)),
            pl.BlockSpec((half + ROW_PACK, half), lambda b, c: (0, 0), pipeline_mode=pl.Buffered(1)),
            _resident((half, half)),
            _resident((REV, REV)),
            pl.BlockSpec((tn // F_CH, 2 * F_CH, 2 * F_CH), lambda b, c: (c, 0, 0)),
        ],
        out_specs=pl.BlockSpec((None, S, tn), lambda b, c: (b, 0, c)),
        out_shape=jax.ShapeDtypeStruct((B, S, F_WIDTH), BF16),
        scratch_shapes=[pltpu.VMEM((half, tn), BF16),
                        pltpu.VMEM((half, tn), BF16),
                        pltpu.VMEM((half, tn), BF16)],
        compiler_params=_params("parallel", "parallel"),
        name="seq_dft",
    )(zf, cmat, smat, jmat, mix)


def _window_attn_kernel(sink_ref, qt_ref, k_ref, kp_ref, kn_ref, vt_ref, vtp_ref, vtn_ref, g_ref,
                        o_ref, kfull, vtfull, *, tq, n_blocks):
    s_idx = pl.program_id(1)
    r_blocks = tq // BLOCK
    span = 3 * BLOCK

    kfull[0:BLOCK, :] = kp_ref[...]
    kfull[BLOCK:BLOCK + tq, :] = k_ref[...]
    kfull[BLOCK + tq:, :] = kn_ref[...]
    vtfull[:, 0:BLOCK] = vtp_ref[...]
    vtfull[:, BLOCK:BLOCK + tq] = vt_ref[...]
    vtfull[:, BLOCK + tq:] = vtn_ref[...]

    key_i = lax.broadcasted_iota(jnp.int32, (BLOCK, BLOCK), 0)
    qry_i = lax.broadcasted_iota(jnp.int32, (BLOCK, BLOCK), 1)
    assert WINDOW == BLOCK

    def scores(n, h):
        r0 = h * HEAD_DIM
        q4 = jnp.concatenate(
            [qt_ref[(h * GQA_GROUP + g) * HEAD_DIM:(h * GQA_GROUP + g + 1) * HEAD_DIM,
                    n * BLOCK:(n + 1) * BLOCK] for g in range(GQA_GROUP)], axis=1)
        parts = [q4]
        if r0 > 0:
            parts.insert(0, jnp.zeros((r0, GQA_GROUP * BLOCK), BF16))
        if r0 + HEAD_DIM < KV_WIDTH:
            parts.append(jnp.zeros((KV_WIDTH - r0 - HEAD_DIM, GQA_GROUP * BLOCK), BF16))
        qpad = jnp.concatenate(parts, axis=0)
        kwin = kfull[n * BLOCK:n * BLOCK + span, :]
        return jnp.dot(kwin, qpad, preferred_element_type=F32)

    def softmax(n, h, st_all):
        nglob = s_idx * r_blocks + n
        valid_prev = (key_i >= qry_i) & (nglob > 0)
        valid_next = (key_i <= qry_i) & (nglob < n_blocks - 1)
        pn, tail = [], []
        for g in range(GQA_GROUP):
            cols = slice(g * BLOCK, (g + 1) * BLOCK)
            s_prev = jnp.where(valid_prev, st_all[0:BLOCK, cols], NEG_INF)
            s_self = st_all[BLOCK:2 * BLOCK, cols]
            s_next = jnp.where(valid_next, st_all[2 * BLOCK:, cols], NEG_INF)
            sink = jnp.full((1, BLOCK), sink_ref[h * GQA_GROUP + g] * LOG2_E, F32)
            m = jnp.max(jnp.maximum(jnp.maximum(s_prev, s_self), s_next), axis=0, keepdims=True)
            m = jnp.maximum(m, sink)
            tail.append(jnp.exp2(sink - m))
            pn.append(jnp.concatenate([jnp.exp2(s_prev - m), jnp.exp2(s_self - m),
                                       jnp.exp2(s_next - m)], axis=0).astype(BF16))
        return jnp.concatenate(pn, axis=1), jnp.concatenate(tail, axis=1)

    def weighted_values(n, h, p, tail):
        r0 = h * HEAD_DIM
        vwin = vtfull[r0:r0 + HEAD_DIM, n * BLOCK:n * BLOCK + span]
        vext = jnp.concatenate([vwin, jnp.ones((ROW_PACK, span), BF16)], axis=0)
        ot = jnp.dot(vext, p, preferred_element_type=F32)
        den = ot[HEAD_DIM:HEAD_DIM + 1] + tail
        ot = ot[:HEAD_DIM] * (1.0 / den)
        return [ot[:, g * BLOCK:(g + 1) * BLOCK] for g in range(GQA_GROUP)]

    def finish_block(n, heads):
        y = jnp.concatenate(heads, axis=0)
        r = lax.rsqrt(jnp.mean(y * y, axis=0, keepdims=True) + EPS)
        o_ref[n * BLOCK:(n + 1) * BLOCK, :] = ((y * r).T * g_ref[...]).astype(BF16)

    pairs = [(n, h) for n in range(r_blocks) for h in range(N_KV_HEADS)]
    heads = []

    def retire(n, h, p, tail):
        heads.extend(weighted_values(n, h, p, tail))
        if h == N_KV_HEADS - 1:
            finish_block(n, list(heads))
            heads.clear()

    st_next = scores(*pairs[0])
    held = None
    for i, (n, h) in enumerate(pairs):
        st_cur = st_next
        if i + 1 < len(pairs):
            st_next = scores(*pairs[i + 1])
        if held is not None:
            retire(*held)
        held = (n, h) + softmax(n, h, st_cur)
    retire(*held)


def _window_attn(sink, qt, k, vt, g, tq):
    B, _, S = qt.shape
    n_blocks = S // BLOCK
    r_blocks = tq // BLOCK
    prev_blk = lambda s: jnp.maximum(s * r_blocks - 1, 0)
    next_blk = lambda s: jnp.minimum((s + 1) * r_blocks, n_blocks - 1)
    return pl.pallas_call(
        functools.partial(_window_attn_kernel, tq=tq, n_blocks=n_blocks),
        grid=(B, S // tq),
        in_specs=[
            pl.BlockSpec(memory_space=pltpu.SMEM),
            pl.BlockSpec((None, A_WIDTH, tq), lambda b, s: (b, 0, s)),
            pl.BlockSpec((None, tq, KV_WIDTH), lambda b, s: (b, s, 0)),
            pl.BlockSpec((None, BLOCK, KV_WIDTH), lambda b, s: (b, prev_blk(s), 0)),
            pl.BlockSpec((None, BLOCK, KV_WIDTH), lambda b, s: (b, next_blk(s), 0)),
            pl.BlockSpec((None, KV_WIDTH, tq), lambda b, s: (b, 0, s)),
            pl.BlockSpec((None, KV_WIDTH, BLOCK), lambda b, s: (b, 0, prev_blk(s))),
            pl.BlockSpec((None, KV_WIDTH, BLOCK), lambda b, s: (b, 0, next_blk(s))),
            pl.BlockSpec((1, A_WIDTH), lambda b, s: (0, 0)),
        ],
        out_specs=pl.BlockSpec((None, tq, A_WIDTH), lambda b, s: (b, s, 0)),
        out_shape=jax.ShapeDtypeStruct((B, S, A_WIDTH), BF16),
        scratch_shapes=[
            pltpu.VMEM((tq + 2 * BLOCK, KV_WIDTH), BF16),
            pltpu.VMEM((KV_WIDTH, tq + 2 * BLOCK), BF16),
        ],
        compiler_params=_params("parallel", "parallel"),
        name="window_attn",
    )(sink, qt, k, k, k, vt, vt, vt, g)


OUT_SUB = 2


def _out_proj_kernel(of_ref, oa_ref, x_ref, wof_ref, woa_ref, gof_ref, g_ref, x1_ref, h2_ref):
    sub = x_ref.shape[0] // OUT_SUB
    for i in range(OUT_SUB):
        rows = slice(i * sub, (i + 1) * sub)
        of = of_ref[rows, :].astype(F32)
        rf = lax.rsqrt(jnp.mean(of * of, axis=-1, keepdims=True) + EPS)
        of_n = ((of * rf) * gof_ref[...]).astype(BF16)
        y = jnp.dot(of_n, wof_ref[...], preferred_element_type=F32)
        y = y + jnp.dot(oa_ref[rows, :], woa_ref[...], preferred_element_type=F32)
        x1 = x_ref[rows, :] + y
        x1_ref[rows, :] = x1
        r = lax.rsqrt(jnp.mean(x1 * x1, axis=-1, keepdims=True) + EPS)
        h2_ref[rows, :] = ((x1 * r) * g_ref[...]).astype(BF16)


def _out_proj(of, oa_n, x, wof, woa, gof, g, tm):
    T = x.shape[0]
    return pl.pallas_call(
        _out_proj_kernel,
        grid=(T // tm,),
        in_specs=[
            pl.BlockSpec((tm, F_WIDTH), lambda i: (i, 0)),
            pl.BlockSpec((tm, A_WIDTH), lambda i: (i, 0)),
            pl.BlockSpec((tm, D_MODEL), lambda i: (i, 0)),
            _resident((F_WIDTH, D_MODEL)),
            _resident((A_WIDTH, D_MODEL)),
            _resident((1, F_WIDTH)),
            _resident((1, D_MODEL)),
        ],
        out_specs=[
            pl.BlockSpec((tm, D_MODEL), lambda i: (i, 0)),
            pl.BlockSpec((tm, D_MODEL), lambda i: (i, 0)),
        ],
        out_shape=[
            jax.ShapeDtypeStruct((T, D_MODEL), F32),
            jax.ShapeDtypeStruct((T, D_MODEL), BF16),
        ],
        compiler_params=_params("parallel"),
        name="out_proj",
    )(of, oa_n, x, wof, woa, gof, g)


MLP_CHUNKS = 2
LAST_SUB = 2


def _mlp_kernel(h2_ref, x1_ref, wu_ref, wd_ref, g_ref, o_ref):
    j = pl.program_id(1)
    last = pl.num_programs(1) - 1
    tm = h2_ref.shape[0]
    tf = wu_ref.shape[1]
    cuts = [(c * tf // MLP_CHUNKS, (c + 1) * tf // MLP_CHUNKS) for c in range(MLP_CHUNKS)]

    def contribution(rows, c):
        lo, hi = cuts[c]
        u = jnp.maximum(jnp.dot(h2_ref[rows, :], wu_ref[:, lo:hi], preferred_element_type=F32), 0.0)
        return jnp.dot((u * u).astype(BF16), wd_ref[lo:hi, :], preferred_element_type=F32)

    @pl.when(j == 0)
    def _():
        rows = slice(0, tm)
        o_ref[...] = x1_ref[...] + contribution(rows, 0)
        for c in range(1, MLP_CHUNKS):
            o_ref[...] += contribution(rows, c)

    @pl.when((j > 0) & (j < last))
    def _():
        rows = slice(0, tm)
        for c in range(MLP_CHUNKS):
            o_ref[...] += contribution(rows, c)

    @pl.when(j == last)
    def _():
        sub = tm // LAST_SUB
        for i in range(LAST_SUB):
            rows = slice(i * sub, (i + 1) * sub)
            y = o_ref[rows, :]
            for c in range(MLP_CHUNKS):
                y = y + contribution(rows, c)
            r = lax.rsqrt(jnp.mean(y * y, axis=-1, keepdims=True) + EPS)
            o_ref[rows, :] = (y * r) * g_ref[...]


def _mlp(h2, x1, wu, wd, g, tm, tf):
    T = h2.shape[0]
    assert D_FF // tf >= 2, "the first and last d_ff steps must be different steps"
    return pl.pallas_call(
        _mlp_kernel,
        grid=(T // tm, D_FF // tf),
        in_specs=[
            pl.BlockSpec((tm, D_MODEL), lambda i, j: (i, 0)),
            pl.BlockSpec((tm, D_MODEL), lambda i, j: (i, 0)),
            pl.BlockSpec((D_MODEL, tf), lambda i, j: (0, j)),
            pl.BlockSpec((tf, D_MODEL), lambda i, j: (j, 0)),
            _resident((1, D_MODEL)),
        ],
        out_specs=pl.BlockSpec((tm, D_MODEL), lambda i, j: (i, 0)),
        out_shape=jax.ShapeDtypeStruct((T, D_MODEL), F32),
        compiler_params=_params("parallel", "arbitrary"),
        name="mlp",
    )(h2, x1, wu, wd, g)


def _rope_tables(S):
    inv_freq = 1.0 / (ROPE_THETA ** (jnp.arange(0, HEAD_DIM, 2, dtype=F32) / HEAD_DIM))
    pos = jnp.arange(S, dtype=F32)
    ang = pos[:, None] * inv_freq[None, :]
    ang = jnp.concatenate([ang, ang], axis=-1)
    cos, sin = jnp.cos(ang), jnp.sin(ang)
    sign = jnp.where(jnp.arange(HEAD_DIM) < HALF_DIM, -1.0, 1.0).astype(F32)
    sin = sin * sign[None, :]
    reps = LANES // HEAD_DIM
    return cos.T, sin.T, jnp.tile(cos, (1, reps)), jnp.tile(sin, (1, reps))


def _seq_dft_matrices(S):
    half = S // 2
    lo_n = 64
    hi_n = half // lo_n
    k = jnp.arange(half, dtype=jnp.int32)
    w = 2.0 * math.pi / S
    a_hi = ((jnp.arange(hi_n + 1, dtype=jnp.int32)[:, None] * lo_n * k[None, :]) % S).astype(F32) * w
    a_lo = ((jnp.arange(lo_n, dtype=jnp.int32)[:, None] * k[None, :]) % S).astype(F32) * w
    ch, sh = jnp.cos(a_hi)[:, None, :], jnp.sin(a_hi)[:, None, :]
    cl, sl = jnp.cos(a_lo)[None, :, :], jnp.sin(a_lo)[None, :, :]
    scale = S ** -0.5
    cos = ((ch * cl - sh * sl) * scale).reshape(half + lo_n, half).astype(BF16)
    sin = ((sh[:hi_n] * cl + ch[:hi_n] * sl) * scale).reshape(half, half).astype(BF16)
    i = jnp.arange(REV, dtype=jnp.int32)
    jmat = ((i[:, None] + i[None, :]) == REV).astype(BF16)
    return cos, sin, jmat


def _trunk(x, w, tm_in, tq, dft_tn, tm_out, tm_mlp, tf):
    B, S, _ = x.shape
    cost, sint, cos, sin = _rope_tables(S)
    pending = None if "wu" in w else (w["wu_f32"], w["wd_f32"])
    zf, qt, k, vt, *cast = _in_proj(x, w["g_mix"], w["wf"], w["wqt"], w["wk"], w["wvt"],
                                    cost, sint, cos, sin, tm_in, pending)
    if pending is not None:
        w["wu"], w["wd"] = cast
    of = _seq_dft(zf, *_seq_dft_matrices(S), w["mix"], dft_tn)
    oa_n = _window_attn(w["sink"], qt, k, vt, w["g_oa"], tq)
    T = B * S
    x2 = x.reshape(T, D_MODEL)
    x1, h2 = _out_proj(of.reshape(T, F_WIDTH), oa_n.reshape(T, A_WIDTH), x2,
                       w["wof"], w["woa"], w["g_of"], w["g_mlp"], tm_out)
    y = _mlp(h2, x1, w["wu"], w["wd"], w["g_final"], tm_mlp, tf)
    return y.reshape(B, S, D_MODEL)


def _prepare_weights(ln_mix_g, w_in, w_fourier, attn_sink, out_norm_fourier_g, out_norm_attn_g,
                     w_out, ln_mlp_g, w_up, w_down, ln_final_g):
    o1, o2, o3 = F_WIDTH, F_WIDTH + A_WIDTH, F_WIDTH + A_WIDTH + KV_WIDTH
    w_out_b = w_out.astype(BF16)
    return {
        "g_mix": ln_mix_g.reshape(1, D_MODEL),
        "wf": w_in[:, :o1].astype(BF16),
        "wqt": w_in[:, o1:o2].T.astype(BF16),
        "wk": w_in[:, o2:o3].astype(BF16),
        "wvt": w_in[:, o3:].T.astype(BF16),
        "mix": _fourier_weights(w_fourier),
        "sink": attn_sink.astype(F32),
        "g_of": out_norm_fourier_g.reshape(1, F_WIDTH),
        "g_oa": out_norm_attn_g.reshape(1, A_WIDTH),
        "wof": w_out_b[:F_WIDTH],
        "woa": w_out_b[F_WIDTH:],
        "g_mlp": ln_mlp_g.reshape(1, D_MODEL),
        "wu_f32": w_up,
        "wd_f32": w_down,
        "g_final": ln_final_g.reshape(1, D_MODEL),
    }


def kernel(x_prompt, x_sample, ln_mix_g, w_in, w_fourier, attn_sink, out_norm_fourier_g,
           out_norm_attn_g, w_out, ln_mlp_g, w_up, w_down, ln_final_g):
    assert ln_mix_g.shape[0] == 1, "single-layer block"
    w = _prepare_weights(ln_mix_g[0], w_in[0], w_fourier[0], attn_sink[0], out_norm_fourier_g[0],
                         out_norm_attn_g[0], w_out[0], ln_mlp_g[0], w_up[0], w_down[0], ln_final_g)
    tiles = dict(tm_in=1024, tq=1024, dft_tn=512, tm_out=512, tm_mlp=512, tf=2048)
    return (_trunk(x_prompt, w, **tiles), _trunk(x_sample, w, **tiles))
```

```python
import functools
import math

import jax
import jax.numpy as jnp
from jax import lax
from jax.experimental import pallas as pl
from jax.experimental.pallas import tpu as pltpu

D_MODEL = 2048
F_WIDTH = 1024
F_GROUPS = 8
F_CH = 128
A_WIDTH = 1024
HEAD_DIM = 64
HALF_DIM = HEAD_DIM // 2
N_Q_HEADS = 16
N_KV_HEADS = 4
GQA_GROUP = 4
KV_WIDTH = 256
WINDOW = 128
BLOCK = 128
ROPE_THETA = 10000.0
D_FF = 4 * D_MODEL
EPS = 1e-6
NEG_INF = -1e30
LOG2_E = math.log2(math.e)

LANES = 128
ROW_PACK = 16
VMEM_LIMIT = 60 * 1024 * 1024

BF16 = jnp.bfloat16
F32 = jnp.float32

_NT = (((1,), (1,)), ((), ()))


def _params(*sem, flags=None):
    return pltpu.CompilerParams(dimension_semantics=sem, vmem_limit_bytes=VMEM_LIMIT, flags=flags)


def _resident(shape):
    nd = len(shape)
    return pl.BlockSpec(shape, lambda *_: (0,) * nd, pipeline_mode=pl.Buffered(1))


def _fourier_weights_kernel(cs_ref, w_ref, o_ref):
    for g in range(F_GROUPS):
        w = w_ref[g]
        c = jnp.dot(cs_ref[0], w, preferred_element_type=F32, precision=lax.Precision.HIGHEST)
        s = jnp.dot(cs_ref[1], w, preferred_element_type=F32, precision=lax.Precision.HIGHEST)
        o_ref[g, :F_CH, :F_CH] = c.astype(BF16)
        o_ref[g, :F_CH, F_CH:] = c.astype(BF16)
        o_ref[g, F_CH:, :F_CH] = (-s).astype(BF16)
        o_ref[g, F_CH:, F_CH:] = s.astype(BF16)


def _fourier_weights(w_fourier):
    n = jnp.arange(F_CH, dtype=jnp.int32)
    ang = ((n[:, None] * n[None, :]) % F_CH).astype(F32) * (2.0 * math.pi / F_CH)
    cs = jnp.stack([jnp.cos(ang), jnp.sin(ang)]) * (F_CH ** -0.5)
    return pl.pallas_call(
        _fourier_weights_kernel,
        out_shape=jax.ShapeDtypeStruct((F_GROUPS, 2 * F_CH, 2 * F_CH), BF16),
        name="fourier_weights",
    )(cs, w_fourier)


IN_W_ROWS = 256


def _in_weights_kernel(w_ref, wf_ref, wqt_ref, wk_ref, wvt_ref):
    o1, o2, o3 = F_WIDTH, F_WIDTH + A_WIDTH, F_WIDTH + A_WIDTH + KV_WIDTH
    wf_ref[...] = w_ref[:, :o1].astype(BF16)
    wqt_ref[...] = w_ref[:, o1:o2].T.astype(BF16)
    wk_ref[...] = w_ref[:, o2:o3].astype(BF16)
    wvt_ref[...] = w_ref[:, o3:].T.astype(BF16)


def _in_weights(w_in):
    d, n = w_in.shape
    return pl.pallas_call(
        _in_weights_kernel,
        grid=(d // IN_W_ROWS,),
        in_specs=[pl.BlockSpec((IN_W_ROWS, n), lambda i: (i, 0))],
        out_specs=[
            pl.BlockSpec((IN_W_ROWS, F_WIDTH), lambda i: (i, 0)),
            pl.BlockSpec((A_WIDTH, IN_W_ROWS), lambda i: (0, i)),
            pl.BlockSpec((IN_W_ROWS, KV_WIDTH), lambda i: (i, 0)),
            pl.BlockSpec((KV_WIDTH, IN_W_ROWS), lambda i: (0, i)),
        ],
        out_shape=[
            jax.ShapeDtypeStruct((d, F_WIDTH), BF16),
            jax.ShapeDtypeStruct((A_WIDTH, d), BF16),
            jax.ShapeDtypeStruct((d, KV_WIDTH), BF16),
            jax.ShapeDtypeStruct((KV_WIDTH, d), BF16),
        ],
        compiler_params=_params("parallel"),
        name="in_weights",
    )(w_in)


IN_SUB = 4


def _in_proj_kernel(x_ref, g_ref, wf_ref, wqt_ref, wk_ref, wvt_ref,
                    cost_ref, sint_ref, cos_ref, sin_ref, *rest, n_cast):
    zf_ref, qt_ref, k_ref, vt_ref = rest[n_cast:n_cast + 4]
    for src_ref, dst_ref in zip(rest[:n_cast], rest[n_cast + 4:]):
        dst_ref[...] = src_ref[...].astype(BF16)

    tm = x_ref.shape[0]
    sub = tm // IN_SUB
    scale = HEAD_DIM ** -0.5 * LOG2_E
    first_half = (lax.broadcasted_iota(jnp.int32, (sub, LANES), 1) % HEAD_DIM) < HALF_DIM
    for i in range(IN_SUB):
        rows = slice(i * sub, (i + 1) * sub)
        x = x_ref[rows, :]
        r = lax.rsqrt(jnp.mean(x * x, axis=-1, keepdims=True) + EPS)
        h = ((x * r) * g_ref[...]).astype(BF16)

        zf = jnp.dot(h, wf_ref[...], preferred_element_type=F32)
        qt = lax.dot_general(wqt_ref[...], h, _NT, preferred_element_type=F32)
        kk = jnp.dot(h, wk_ref[...], preferred_element_type=F32)
        vt = lax.dot_general(wvt_ref[...], h, _NT, preferred_element_type=F32)

        zf_ref[rows, :] = zf.astype(BF16)

        cost = cost_ref[:, rows]
        sint = sint_ref[:, rows]
        for hd in range(N_Q_HEADS):
            lo = hd * HEAD_DIM
            blk = qt[lo:lo + HEAD_DIM]
            rot = jnp.concatenate([blk[HALF_DIM:], blk[:HALF_DIM]], axis=0)
            qt_ref[lo:lo + HEAD_DIM, rows] = ((blk * cost + rot * sint) * scale).astype(BF16)

        cos = cos_ref[rows, :]
        sin = sin_ref[rows, :]
        for c in range(KV_WIDTH // LANES):
            blk = kk[:, c * LANES:(c + 1) * LANES]
            rot = jnp.where(first_half,
                            pltpu.roll(blk, LANES - HALF_DIM, axis=1),
                            pltpu.roll(blk, HALF_DIM, axis=1))
            k_ref[rows, c * LANES:(c + 1) * LANES] = (blk * cos + rot * sin).astype(BF16)

        vt_ref[:, rows] = vt.astype(BF16)


def _in_proj(x, g, wf, wqt, wk, wvt, cost, sint, cos, sin, tm, cast_weights=()):
    B, S, _ = x.shape
    n_s = S // tm
    in_specs = [
        pl.BlockSpec((None, tm, D_MODEL), lambda b, s: (b, s, 0)),
        _resident((1, D_MODEL)),
        _resident((D_MODEL, F_WIDTH)),
        _resident((A_WIDTH, D_MODEL)),
        _resident((D_MODEL, KV_WIDTH)),
        _resident((KV_WIDTH, D_MODEL)),
        pl.BlockSpec((HEAD_DIM, tm), lambda b, s: (0, s)),
        pl.BlockSpec((HEAD_DIM, tm), lambda b, s: (0, s)),
        pl.BlockSpec((tm, LANES), lambda b, s: (s, 0)),
        pl.BlockSpec((tm, LANES), lambda b, s: (s, 0)),
    ]
    out_specs = [
        pl.BlockSpec((None, tm, F_WIDTH), lambda b, s: (b, s, 0)),
        pl.BlockSpec((None, A_WIDTH, tm), lambda b, s: (b, 0, s)),
        pl.BlockSpec((None, tm, KV_WIDTH), lambda b, s: (b, s, 0)),
        pl.BlockSpec((None, KV_WIDTH, tm), lambda b, s: (b, 0, s)),
    ]
    out_shape = [
        jax.ShapeDtypeStruct((B, S, F_WIDTH), BF16),
        jax.ShapeDtypeStruct((B, A_WIDTH, S), BF16),
        jax.ShapeDtypeStruct((B, S, KV_WIDTH), BF16),
        jax.ShapeDtypeStruct((B, KV_WIDTH, S), BF16),
    ]
    args = [x, g, wf, wqt, wk, wvt, cost, sint, cos, sin]
    if cast_weights:
        n_steps = B * n_s
        for wgt in cast_weights:
            rows, cols = wgt.shape
            assert rows % (n_steps * ROW_PACK) == 0
            spec = pl.BlockSpec((rows // n_steps, cols), lambda b, s: (b * n_s + s, 0))
            in_specs.append(spec)
            out_specs.append(spec)
            out_shape.append(jax.ShapeDtypeStruct(wgt.shape, BF16))
            args.append(wgt)
    return pl.pallas_call(
        functools.partial(_in_proj_kernel, n_cast=len(cast_weights)),
        grid=(B, n_s),
        in_specs=in_specs,
        out_specs=out_specs,
        out_shape=out_shape,
        compiler_params=_params("parallel", "parallel"),
        name="in_proj",
    )(*args)


REV = 256
DFT_ROWS = 512


def _first_row(x):
    return jnp.where(lax.broadcasted_iota(jnp.int32, x.shape, 0) == 0, x, 0.0)


def _patch_first_row(tile, row):
    head = tile[:ROW_PACK] + _first_row(row)
    return jnp.concatenate([head, tile[ROW_PACK:]], axis=0)


def _seq_dft_kernel(z_ref, c_ref, s_ref, j_ref, mix_ref, o_ref, zp_ref, zm_ref, u_ref, *, seq):
    half = seq // 2
    n_rev = half // REV
    n_grp = z_ref.shape[1] // F_CH
    jmat = j_ref[...]

    for t in range(n_rev):
        lo = t * REV
        src = seq - lo - REV
        r = jnp.dot(jmat, z_ref[src:src + REV, :], preferred_element_type=F32)
        if t > 0:
            r = _patch_first_row(r, z_ref[seq - lo:seq - lo + ROW_PACK, :].astype(F32))
        z_lo = z_ref[lo:lo + REV, :].astype(F32)
        zp_ref[lo:lo + REV, :] = (z_lo + r).astype(BF16)
        zm_ref[lo:lo + REV, :] = (z_lo - r).astype(BF16)

    z_half = z_ref[half:half + ROW_PACK, :].astype(F32)[0:1, :] * (seq ** -0.5)
    row_par = lax.broadcasted_iota(jnp.int32, (DFT_ROWS, 1), 0) % 2
    sgn_z_half = jnp.where(row_par == 0, z_half, -z_half)

    zp = zp_ref[...]
    zm = zm_ref[...]
    n_row = half // DFT_ROWS

    def spectra(i):
        lo = i * DFT_ROWS
        extra = ROW_PACK if i == n_row - 1 else 0
        p = jnp.dot(c_ref[lo:lo + DFT_ROWS + extra, :], zp, preferred_element_type=F32)
        q = jnp.dot(s_ref[lo:lo + DFT_ROWS, :], zm, preferred_element_type=F32)
        if extra:
            p = jnp.concatenate([p[:DFT_ROWS] + sgn_z_half, p[DFT_ROWS:] + z_half], axis=0)
            q = jnp.concatenate([q, jnp.zeros((extra, q.shape[1]), F32)], axis=0)
        else:
            p = p + sgn_z_half
        return p.astype(BF16), q.astype(BF16)

    nyq = []

    def mix(i, pq):
        p, q = pq
        lo = i * DFT_ROWS
        for g in range(n_grp):
            cols = slice(g * F_CH, (g + 1) * F_CH)
            lu = jnp.dot(jnp.concatenate([p[:, cols], q[:, cols]], axis=1), mix_ref[g],
                         preferred_element_type=F32)
            o_ref[lo:lo + DFT_ROWS, cols] = lu[:DFT_ROWS, :F_CH].astype(BF16)
            u_ref[lo:lo + DFT_ROWS, cols] = lu[:DFT_ROWS, F_CH:].astype(BF16)
            if lu.shape[0] > DFT_ROWS:
                nyq.append(lu[DFT_ROWS:, :F_CH])

    pending = spectra(0)
    for i in range(n_row):
        cur = pending
        if i + 1 < n_row:
            pending = spectra(i + 1)
        mix(i, cur)
    nyq_row = jnp.concatenate(nyq, axis=1)

    for t in range(n_rev):
        src = half - (t + 1) * REV
        r = jnp.dot(jmat, u_ref[src:src + REV, :], preferred_element_type=F32)
        if t == 0:
            r = _patch_first_row(r, nyq_row)
        else:
            r = _patch_first_row(r, u_ref[src + REV:src + REV + ROW_PACK, :].astype(F32))
        o_ref[half + t * REV:half + (t + 1) * REV, :] = r.astype(BF16)


def _seq_dft(zf, cmat, smat, jmat, mix, tn):
    B, S, _ = zf.shape
    half = S // 2
    assert half % DFT_ROWS == 0 and half % 2 == 0 and tn % F_CH == 0
    return pl.pallas_call(
        functools.partial(_seq_dft_kernel, seq=S),
        grid=(B, F_WIDTH // tn),
        in_specs=[
            pl.BlockSpec((None, S, tn), lambda b, c: (b, 0, c)),
            pl.BlockSpec((half + ROW_PACK, half), lambda b, c: (0, 0), pipeline_mode=pl.Buffered(1)),
            _resident((half, half)),
            _resident((REV, REV)),
            pl.BlockSpec((tn // F_CH, 2 * F_CH, 2 * F_CH), lambda b, c: (c, 0, 0)),
        ],
        out_specs=pl.BlockSpec((None, S, tn), lambda b, c: (b, 0, c)),
        out_shape=jax.ShapeDtypeStruct((B, S, F_WIDTH), BF16),
        scratch_shapes=[pltpu.VMEM((half, tn), BF16),
                        pltpu.VMEM((half, tn), BF16),
                        pltpu.VMEM((half, tn), BF16)],
        compiler_params=_params("parallel", "parallel"),
        name="seq_dft",
    )(zf, cmat, smat, jmat, mix)


def _window_attn_kernel(sink_ref, qt_ref, k_ref, kp_ref, kn_ref, vt_ref, vtp_ref, vtn_ref, g_ref,
                        o_ref, kfull, vtfull, *, tq, n_blocks):
    s_idx = pl.program_id(1)
    r_blocks = tq // BLOCK
    span = 3 * BLOCK

    kfull[0:BLOCK, :] = kp_ref[...]
    kfull[BLOCK:BLOCK + tq, :] = k_ref[...]
    kfull[BLOCK + tq:, :] = kn_ref[...]
    vtfull[:, 0:BLOCK] = vtp_ref[...]
    vtfull[:, BLOCK:BLOCK + tq] = vt_ref[...]
    vtfull[:, BLOCK + tq:] = vtn_ref[...]

    key_i = lax.broadcasted_iota(jnp.int32, (BLOCK, BLOCK), 0)
    qry_i = lax.broadcasted_iota(jnp.int32, (BLOCK, BLOCK), 1)
    assert WINDOW == BLOCK

    def scores(n, h):
        r0 = h * HEAD_DIM
        q4 = jnp.concatenate(
            [qt_ref[(h * GQA_GROUP + g) * HEAD_DIM:(h * GQA_GROUP + g + 1) * HEAD_DIM,
                    n * BLOCK:(n + 1) * BLOCK] for g in range(GQA_GROUP)], axis=1)
        parts = [q4]
        if r0 > 0:
            parts.insert(0, jnp.zeros((r0, GQA_GROUP * BLOCK), BF16))
        if r0 + HEAD_DIM < KV_WIDTH:
            parts.append(jnp.zeros((KV_WIDTH - r0 - HEAD_DIM, GQA_GROUP * BLOCK), BF16))
        qpad = jnp.concatenate(parts, axis=0)
        kwin = kfull[n * BLOCK:n * BLOCK + span, :]
        return jnp.dot(kwin, qpad, preferred_element_type=F32)

    def softmax(n, h, st_all):
        nglob = s_idx * r_blocks + n
        valid_prev = (key_i >= qry_i) & (nglob > 0)
        valid_next = (key_i <= qry_i) & (nglob < n_blocks - 1)
        pn, tail = [], []
        for g in range(GQA_GROUP):
            cols = slice(g * BLOCK, (g + 1) * BLOCK)
            s_prev = jnp.where(valid_prev, st_all[0:BLOCK, cols], NEG_INF)
            s_self = st_all[BLOCK:2 * BLOCK, cols]
            s_next = jnp.where(valid_next, st_all[2 * BLOCK:, cols], NEG_INF)
            sink = jnp.full((1, BLOCK), sink_ref[h * GQA_GROUP + g] * LOG2_E, F32)
            m = jnp.max(jnp.maximum(jnp.maximum(s_prev, s_self), s_next), axis=0, keepdims=True)
            m = jnp.maximum(m, sink)
            tail.append(jnp.exp2(sink - m))
            pn.append(jnp.concatenate([jnp.exp2(s_prev - m), jnp.exp2(s_self - m),
                                       jnp.exp2(s_next - m)], axis=0).astype(BF16))
        return jnp.concatenate(pn, axis=1), jnp.concatenate(tail, axis=1)

    def weighted_values(n, h, p, tail):
        r0 = h * HEAD_DIM
        vwin = vtfull[r0:r0 + HEAD_DIM, n * BLOCK:n * BLOCK + span]
        vext = jnp.concatenate([vwin, jnp.ones((ROW_PACK, span), BF16)], axis=0)
        ot = jnp.dot(vext, p, preferred_element_type=F32)
        den = ot[HEAD_DIM:HEAD_DIM + 1] + tail
        ot = ot[:HEAD_DIM] * (1.0 / den)
        return [ot[:, g * BLOCK:(g + 1) * BLOCK] for g in range(GQA_GROUP)]

    def finish_block(n, heads):
        y = jnp.concatenate(heads, axis=0)
        r = lax.rsqrt(jnp.mean(y * y, axis=0, keepdims=True) + EPS)
        o_ref[n * BLOCK:(n + 1) * BLOCK, :] = ((y * r).T * g_ref[...]).astype(BF16)

    pairs = [(n, h) for n in range(r_blocks) for h in range(N_KV_HEADS)]
    heads = []

    def retire(n, h, p, tail):
        heads.extend(weighted_values(n, h, p, tail))
        if h == N_KV_HEADS - 1:
            finish_block(n, list(heads))
            heads.clear()

    st_next = scores(*pairs[0])
    held = None
    for i, (n, h) in enumerate(pairs):
        st_cur = st_next
        if i + 1 < len(pairs):
            st_next = scores(*pairs[i + 1])
        if held is not None:
            retire(*held)
        held = (n, h) + softmax(n, h, st_cur)
    retire(*held)


def _window_attn(sink, qt, k, vt, g, tq):
    B, _, S = qt.shape
    n_blocks = S // BLOCK
    r_blocks = tq // BLOCK
    prev_blk = lambda s: jnp.maximum(s * r_blocks - 1, 0)
    next_blk = lambda s: jnp.minimum((s + 1) * r_blocks, n_blocks - 1)
    return pl.pallas_call(
        functools.partial(_window_attn_kernel, tq=tq, n_blocks=n_blocks),
        grid=(B, S // tq),
        in_specs=[
            pl.BlockSpec(memory_space=pltpu.SMEM),
            pl.BlockSpec((None, A_WIDTH, tq), lambda b, s: (b, 0, s)),
            pl.BlockSpec((None, tq, KV_WIDTH), lambda b, s: (b, s, 0)),
            pl.BlockSpec((None, BLOCK, KV_WIDTH), lambda b, s: (b, prev_blk(s), 0)),
            pl.BlockSpec((None, BLOCK, KV_WIDTH), lambda b, s: (b, next_blk(s), 0)),
            pl.BlockSpec((None, KV_WIDTH, tq), lambda b, s: (b, 0, s)),
            pl.BlockSpec((None, KV_WIDTH, BLOCK), lambda b, s: (b, 0, prev_blk(s))),
            pl.BlockSpec((None, KV_WIDTH, BLOCK), lambda b, s: (b, 0, next_blk(s))),
            pl.BlockSpec((1, A_WIDTH), lambda b, s: (0, 0)),
        ],
        out_specs=pl.BlockSpec((None, tq, A_WIDTH), lambda b, s: (b, s, 0)),
        out_shape=jax.ShapeDtypeStruct((B, S, A_WIDTH), BF16),
        scratch_shapes=[
            pltpu.VMEM((tq + 2 * BLOCK, KV_WIDTH), BF16),
            pltpu.VMEM((KV_WIDTH, tq + 2 * BLOCK), BF16),
        ],
        compiler_params=_params("parallel", "parallel"),
        name="window_attn",
    )(sink, qt, k, k, k, vt, vt, vt, g)


OUT_SUB = 2


def _out_proj_kernel(of_ref, oa_ref, x_ref, wout_ref, gof_ref, g_ref, x1_ref, h2_ref):
    sub = x_ref.shape[0] // OUT_SUB
    for i in range(OUT_SUB):
        rows = slice(i * sub, (i + 1) * sub)
        of = of_ref[rows, :].astype(F32)
        rf = lax.rsqrt(jnp.mean(of * of, axis=-1, keepdims=True) + EPS)
        of_n = ((of * rf) * gof_ref[...]).astype(BF16)
        y = jnp.dot(of_n, wout_ref[:F_WIDTH, :], preferred_element_type=F32)
        y = y + jnp.dot(oa_ref[rows, :], wout_ref[F_WIDTH:, :], preferred_element_type=F32)
        x1 = x_ref[rows, :] + y
        x1_ref[rows, :] = x1
        r = lax.rsqrt(jnp.mean(x1 * x1, axis=-1, keepdims=True) + EPS)
        h2_ref[rows, :] = ((x1 * r) * g_ref[...]).astype(BF16)


def _out_proj(of, oa_n, x, wout, gof, g, tm):
    T = x.shape[0]
    return pl.pallas_call(
        _out_proj_kernel,
        grid=(T // tm,),
        in_specs=[
            pl.BlockSpec((tm, F_WIDTH), lambda i: (i, 0)),
            pl.BlockSpec((tm, A_WIDTH), lambda i: (i, 0)),
            pl.BlockSpec((tm, D_MODEL), lambda i: (i, 0)),
            _resident((F_WIDTH + A_WIDTH, D_MODEL)),
            _resident((1, F_WIDTH)),
            _resident((1, D_MODEL)),
        ],
        out_specs=[
            pl.BlockSpec((tm, D_MODEL), lambda i: (i, 0)),
            pl.BlockSpec((tm, D_MODEL), lambda i: (i, 0)),
        ],
        out_shape=[
            jax.ShapeDtypeStruct((T, D_MODEL), F32),
            jax.ShapeDtypeStruct((T, D_MODEL), BF16),
        ],
        compiler_params=_params("parallel"),
        name="out_proj",
    )(of, oa_n, x, wout, gof, g)


MLP_CHUNKS = 2
LAST_SUB = 2


def _mlp_kernel(h2_ref, x1_ref, wu_ref, wd_ref, g_ref, o_ref):
    j = pl.program_id(1)
    last = pl.num_programs(1) - 1
    tm = h2_ref.shape[0]
    tf = wu_ref.shape[1]
    cuts = [(c * tf // MLP_CHUNKS, (c + 1) * tf // MLP_CHUNKS) for c in range(MLP_CHUNKS)]

    def contribution(rows, c):
        lo, hi = cuts[c]
        u = jnp.maximum(jnp.dot(h2_ref[rows, :], wu_ref[:, lo:hi], preferred_element_type=F32), 0.0)
        return jnp.dot((u * u).astype(BF16), wd_ref[lo:hi, :], preferred_element_type=F32)

    @pl.when(j == 0)
    def _():
        rows = slice(0, tm)
        o_ref[...] = x1_ref[...] + contribution(rows, 0)
        for c in range(1, MLP_CHUNKS):
            o_ref[...] += contribution(rows, c)

    @pl.when((j > 0) & (j < last))
    def _():
        rows = slice(0, tm)
        for c in range(MLP_CHUNKS):
            o_ref[...] += contribution(rows, c)

    @pl.when(j == last)
    def _():
        sub = tm // LAST_SUB
        for i in range(LAST_SUB):
            rows = slice(i * sub, (i + 1) * sub)
            y = o_ref[rows, :]
            for c in range(MLP_CHUNKS):
                y = y + contribution(rows, c)
            r = lax.rsqrt(jnp.mean(y * y, axis=-1, keepdims=True) + EPS)
            o_ref[rows, :] = (y * r) * g_ref[...]


def _mlp(h2, x1, wu, wd, g, tm, tf):
    T = h2.shape[0]
    assert D_FF // tf >= 2, "the first and last d_ff steps must be different steps"
    return pl.pallas_call(
        _mlp_kernel,
        grid=(T // tm, D_FF // tf),
        in_specs=[
            pl.BlockSpec((tm, D_MODEL), lambda i, j: (i, 0)),
            pl.BlockSpec((tm, D_MODEL), lambda i, j: (i, 0)),
            pl.BlockSpec((D_MODEL, tf), lambda i, j: (0, j)),
            pl.BlockSpec((tf, D_MODEL), lambda i, j: (j, 0)),
            _resident((1, D_MODEL)),
        ],
        out_specs=pl.BlockSpec((tm, D_MODEL), lambda i, j: (i, 0)),
        out_shape=jax.ShapeDtypeStruct((T, D_MODEL), F32),
        compiler_params=_params("parallel", "arbitrary"),
        name="mlp",
    )(h2, x1, wu, wd, g)


def _rope_tables(S):
    inv_freq = 1.0 / (ROPE_THETA ** (jnp.arange(0, HEAD_DIM, 2, dtype=F32) / HEAD_DIM))
    pos = jnp.arange(S, dtype=F32)
    ang = pos[:, None] * inv_freq[None, :]
    ang = jnp.concatenate([ang, ang], axis=-1)
    cos, sin = jnp.cos(ang), jnp.sin(ang)
    sign = jnp.where(jnp.arange(HEAD_DIM) < HALF_DIM, -1.0, 1.0).astype(F32)
    sin = sin * sign[None, :]
    reps = LANES // HEAD_DIM
    return cos.T, sin.T, jnp.tile(cos, (1, reps)), jnp.tile(sin, (1, reps))


def _seq_dft_matrices(S):
    half = S // 2
    lo_n = 64
    hi_n = half // lo_n
    k = jnp.arange(half, dtype=jnp.int32)
    w = 2.0 * math.pi / S
    a_hi = ((jnp.arange(hi_n + 1, dtype=jnp.int32)[:, None] * lo_n * k[None, :]) % S).astype(F32) * w
    a_lo = ((jnp.arange(lo_n, dtype=jnp.int32)[:, None] * k[None, :]) % S).astype(F32) * w
    ch, sh = jnp.cos(a_hi)[:, None, :], jnp.sin(a_hi)[:, None, :]
    cl, sl = jnp.cos(a_lo)[None, :, :], jnp.sin(a_lo)[None, :, :]
    scale = S ** -0.5
    cos = ((ch * cl - sh * sl) * scale).reshape(half + lo_n, half).astype(BF16)
    sin = ((sh[:hi_n] * cl + ch[:hi_n] * sl) * scale).reshape(half, half).astype(BF16)
    i = jnp.arange(REV, dtype=jnp.int32)
    jmat = ((i[:, None] + i[None, :]) == REV).astype(BF16)
    return cos, sin, jmat


def _trunk(x, w, tm_in, tq, dft_tn, tm_out, tm_mlp, tf):
    B, S, _ = x.shape
    cost, sint, cos, sin = _rope_tables(S)
    pending = () if "wu" in w else (w["wu_f32"], w["wd_f32"], w["wout_f32"])
    zf, qt, k, vt, *cast = _in_proj(x, w["g_mix"], w["wf"], w["wqt"], w["wk"], w["wvt"],
                                    cost, sint, cos, sin, tm_in, pending)
    if pending:
        w["wu"], w["wd"], w["wout"] = cast
    of = _seq_dft(zf, *_seq_dft_matrices(S), w["mix"], dft_tn)
    oa_n = _window_attn(w["sink"], qt, k, vt, w["g_oa"], tq)
    T = B * S
    x2 = x.reshape(T, D_MODEL)
    x1, h2 = _out_proj(of.reshape(T, F_WIDTH), oa_n.reshape(T, A_WIDTH), x2,
                       w["wout"], w["g_of"], w["g_mlp"], tm_out)
    y = _mlp(h2, x1, w["wu"], w["wd"], w["g_final"], tm_mlp, tf)
    return y.reshape(B, S, D_MODEL)


def _prepare_weights(ln_mix_g, w_in, w_fourier, attn_sink, out_norm_fourier_g, out_norm_attn_g,
                     w_out, ln_mlp_g, w_up, w_down, ln_final_g):
    wf, wqt, wk, wvt = _in_weights(w_in)
    return {
        "g_mix": ln_mix_g.reshape(1, D_MODEL),
        "wf": wf,
        "wqt": wqt,
        "wk": wk,
        "wvt": wvt,
        "mix": _fourier_weights(w_fourier),
        "sink": attn_sink.astype(F32),
        "g_of": out_norm_fourier_g.reshape(1, F_WIDTH),
        "g_oa": out_norm_attn_g.reshape(1, A_WIDTH),
        "wout_f32": w_out,
        "g_mlp": ln_mlp_g.reshape(1, D_MODEL),
        "wu_f32": w_up,
        "wd_f32": w_down,
        "g_final": ln_final_g.reshape(1, D_MODEL),
    }


def kernel(x_prompt, x_sample, ln_mix_g, w_in, w_fourier, attn_sink, out_norm_fourier_g,
           out_norm_attn_g, w_out, ln_mlp_g, w_up, w_down, ln_final_g):
    assert ln_mix_g.shape[0] == 1, "single-layer block"
    w = _prepare_weights(ln_mix_g[0], w_in[0], w_fourier[0], attn_sink[0], out_norm_fourier_g[0],
                         out_norm_attn_g[0], w_out[0], ln_mlp_g[0], w_up[0], w_down[0], ln_final_g)
    tiles = dict(tm_in=1024, tq=1024, dft_tn=512, tm_out=512, tm_mlp=512, tf=2048)
    return (_trunk(x_prompt, w, **tiles), _trunk(x_sample, w, **tiles))
```

```python
import functools
import math

import jax
import jax.numpy as jnp
from jax import lax
from jax.experimental import pallas as pl
from jax.experimental.pallas import tpu as pltpu

D_MODEL = 2048
F_WIDTH = 1024
F_GROUPS = 8
F_CH = 128
A_WIDTH = 1024
HEAD_DIM = 64
HALF_DIM = HEAD_DIM // 2
N_Q_HEADS = 16
N_KV_HEADS = 4
GQA_GROUP = 4
KV_WIDTH = 256
WINDOW = 128
BLOCK = 128
ROPE_THETA = 10000.0
D_FF = 4 * D_MODEL
EPS = 1e-6
NEG_INF = -1e30
LOG2_E = math.log2(math.e)

LANES = 128
ROW_PACK = 16
VMEM_LIMIT = 60 * 1024 * 1024

BF16 = jnp.bfloat16
F32 = jnp.float32

_NT = (((1,), (1,)), ((), ()))


def _params(*sem, flags=None):
    return pltpu.CompilerParams(dimension_semantics=sem, vmem_limit_bytes=VMEM_LIMIT, flags=flags)


def _resident(shape):
    nd = len(shape)
    return pl.BlockSpec(shape, lambda *_: (0,) * nd, pipeline_mode=pl.Buffered(1))


def _fourier_weights_kernel(cs_ref, w_ref, o_ref):
    for g in range(F_GROUPS):
        w = w_ref[g]
        c = jnp.dot(cs_ref[0], w, preferred_element_type=F32, precision=lax.Precision.HIGHEST)
        s = jnp.dot(cs_ref[1], w, preferred_element_type=F32, precision=lax.Precision.HIGHEST)
        o_ref[g, :F_CH, :F_CH] = c.astype(BF16)
        o_ref[g, :F_CH, F_CH:] = c.astype(BF16)
        o_ref[g, F_CH:, :F_CH] = (-s).astype(BF16)
        o_ref[g, F_CH:, F_CH:] = s.astype(BF16)


def _fourier_weights(w_fourier):
    n = jnp.arange(F_CH, dtype=jnp.int32)
    ang = ((n[:, None] * n[None, :]) % F_CH).astype(F32) * (2.0 * math.pi / F_CH)
    cs = jnp.stack([jnp.cos(ang), jnp.sin(ang)]) * (F_CH ** -0.5)
    return pl.pallas_call(
        _fourier_weights_kernel,
        out_shape=jax.ShapeDtypeStruct((F_GROUPS, 2 * F_CH, 2 * F_CH), BF16),
        name="fourier_weights",
    )(cs, w_fourier)


IN_W_ROWS = 256


def _in_weights_kernel(w_ref, wf_ref, wqt_ref, wk_ref, wvt_ref):
    o1, o2, o3 = F_WIDTH, F_WIDTH + A_WIDTH, F_WIDTH + A_WIDTH + KV_WIDTH
    wf_ref[...] = w_ref[:, :o1].astype(BF16)
    wqt_ref[...] = w_ref[:, o1:o2].T.astype(BF16)
    wk_ref[...] = w_ref[:, o2:o3].astype(BF16)
    wvt_ref[...] = w_ref[:, o3:].T.astype(BF16)


def _in_weights(w_in):
    d, n = w_in.shape
    return pl.pallas_call(
        _in_weights_kernel,
        grid=(d // IN_W_ROWS,),
        in_specs=[pl.BlockSpec((IN_W_ROWS, n), lambda i: (i, 0))],
        out_specs=[
            pl.BlockSpec((IN_W_ROWS, F_WIDTH), lambda i: (i, 0)),
            pl.BlockSpec((A_WIDTH, IN_W_ROWS), lambda i: (0, i)),
            pl.BlockSpec((IN_W_ROWS, KV_WIDTH), lambda i: (i, 0)),
            pl.BlockSpec((KV_WIDTH, IN_W_ROWS), lambda i: (0, i)),
        ],
        out_shape=[
            jax.ShapeDtypeStruct((d, F_WIDTH), BF16),
            jax.ShapeDtypeStruct((A_WIDTH, d), BF16),
            jax.ShapeDtypeStruct((d, KV_WIDTH), BF16),
            jax.ShapeDtypeStruct((KV_WIDTH, d), BF16),
        ],
        compiler_params=_params("parallel"),
        name="in_weights",
    )(w_in)


IN_SUB = 4


def _in_proj_kernel(x_ref, g_ref, wf_ref, wqt_ref, wk_ref, wvt_ref,
                    cost_ref, sint_ref, cos_ref, sin_ref, *rest, n_cast):
    zf_ref, qt_ref, k_ref, vt_ref = rest[n_cast:n_cast + 4]
    for src_ref, dst_ref in zip(rest[:n_cast], rest[n_cast + 4:]):
        dst_ref[...] = src_ref[...].astype(BF16)

    tm = x_ref.shape[0]
    sub = tm // IN_SUB
    scale = HEAD_DIM ** -0.5 * LOG2_E
    first_half = (lax.broadcasted_iota(jnp.int32, (sub, LANES), 1) % HEAD_DIM) < HALF_DIM
    for i in range(IN_SUB):
        rows = slice(i * sub, (i + 1) * sub)
        x = x_ref[rows, :]
        r = lax.rsqrt(jnp.mean(x * x, axis=-1, keepdims=True) + EPS)
        h = ((x * r) * g_ref[...]).astype(BF16)

        zf = jnp.dot(h, wf_ref[...], preferred_element_type=F32)
        qt = lax.dot_general(wqt_ref[...], h, _NT, preferred_element_type=F32)
        kk = jnp.dot(h, wk_ref[...], preferred_element_type=F32)
        vt = lax.dot_general(wvt_ref[...], h, _NT, preferred_element_type=F32)

        zf_ref[rows, :] = zf.astype(BF16)

        cost = cost_ref[:, rows]
        sint = sint_ref[:, rows]
        for hd in range(N_Q_HEADS):
            lo = hd * HEAD_DIM
            blk = qt[lo:lo + HEAD_DIM]
            rot = jnp.concatenate([blk[HALF_DIM:], blk[:HALF_DIM]], axis=0)
            qt_ref[lo:lo + HEAD_DIM, rows] = ((blk * cost + rot * sint) * scale).astype(BF16)

        cos = cos_ref[rows, :]
        sin = sin_ref[rows, :]
        for c in range(KV_WIDTH // LANES):
            blk = kk[:, c * LANES:(c + 1) * LANES]
            rot = jnp.where(first_half,
                            pltpu.roll(blk, LANES - HALF_DIM, axis=1),
                            pltpu.roll(blk, HALF_DIM, axis=1))
            k_ref[rows, c * LANES:(c + 1) * LANES] = (blk * cos + rot * sin).astype(BF16)

        vt_ref[:, rows] = vt.astype(BF16)


def _in_proj(x, g, wf, wqt, wk, wvt, cost, sint, cos, sin, tm, cast_weights=()):
    B, S, _ = x.shape
    n_s = S // tm
    in_specs = [
        pl.BlockSpec((None, tm, D_MODEL), lambda b, s: (b, s, 0)),
        _resident((1, D_MODEL)),
        _resident((D_MODEL, F_WIDTH)),
        _resident((A_WIDTH, D_MODEL)),
        _resident((D_MODEL, KV_WIDTH)),
        _resident((KV_WIDTH, D_MODEL)),
        pl.BlockSpec((HEAD_DIM, tm), lambda b, s: (0, s)),
        pl.BlockSpec((HEAD_DIM, tm), lambda b, s: (0, s)),
        pl.BlockSpec((tm, LANES), lambda b, s: (s, 0)),
        pl.BlockSpec((tm, LANES), lambda b, s: (s, 0)),
    ]
    out_specs = [
        pl.BlockSpec((None, tm, F_WIDTH), lambda b, s: (b, s, 0)),
        pl.BlockSpec((None, A_WIDTH, tm), lambda b, s: (b, 0, s)),
        pl.BlockSpec((None, tm, KV_WIDTH), lambda b, s: (b, s, 0)),
        pl.BlockSpec((None, KV_WIDTH, tm), lambda b, s: (b, 0, s)),
    ]
    out_shape = [
        jax.ShapeDtypeStruct((B, S, F_WIDTH), BF16),
        jax.ShapeDtypeStruct((B, A_WIDTH, S), BF16),
        jax.ShapeDtypeStruct((B, S, KV_WIDTH), BF16),
        jax.ShapeDtypeStruct((B, KV_WIDTH, S), BF16),
    ]
    args = [x, g, wf, wqt, wk, wvt, cost, sint, cos, sin]
    if cast_weights:
        n_steps = B * n_s
        for wgt in cast_weights:
            rows, cols = wgt.shape
            assert rows % (n_steps * ROW_PACK) == 0
            spec = pl.BlockSpec((rows // n_steps, cols), lambda b, s: (b * n_s + s, 0))
            in_specs.append(spec)
            out_specs.append(spec)
            out_shape.append(jax.ShapeDtypeStruct(wgt.shape, BF16))
            args.append(wgt)
    return pl.pallas_call(
        functools.partial(_in_proj_kernel, n_cast=len(cast_weights)),
        grid=(B, n_s),
        in_specs=in_specs,
        out_specs=out_specs,
        out_shape=out_shape,
        compiler_params=_params("parallel", "parallel"),
        name="in_proj",
    )(*args)


REV = 256
DFT_ROWS = 512


def _first_row(x):
    return jnp.where(lax.broadcasted_iota(jnp.int32, x.shape, 0) == 0, x, 0.0)


def _patch_first_row(tile, row):
    head = tile[:ROW_PACK] + _first_row(row)
    return jnp.concatenate([head, tile[ROW_PACK:]], axis=0)


def _seq_dft_kernel(z_ref, c_ref, s_ref, j_ref, mix_ref, o_ref, zp_ref, zm_ref, u_ref, *, seq):
    half = seq // 2
    n_rev = half // REV
    n_grp = z_ref.shape[1] // F_CH
    jmat = j_ref[...]

    for t in range(n_rev):
        lo = t * REV
        src = seq - lo - REV
        r = jnp.dot(jmat, z_ref[src:src + REV, :], preferred_element_type=F32)
        if t > 0:
            r = _patch_first_row(r, z_ref[seq - lo:seq - lo + ROW_PACK, :].astype(F32))
        z_lo = z_ref[lo:lo + REV, :].astype(F32)
        zp_ref[lo:lo + REV, :] = (z_lo + r).astype(BF16)
        zm_ref[lo:lo + REV, :] = (z_lo - r).astype(BF16)

    z_half = z_ref[half:half + ROW_PACK, :].astype(F32)[0:1, :] * (seq ** -0.5)
    row_par = lax.broadcasted_iota(jnp.int32, (DFT_ROWS, 1), 0) % 2
    sgn_z_half = jnp.where(row_par == 0, z_half, -z_half)

    zp = zp_ref[...]
    zm = zm_ref[...]
    n_row = half // DFT_ROWS

    def spectra(i):
        lo = i * DFT_ROWS
        extra = ROW_PACK if i == n_row - 1 else 0
        p = jnp.dot(c_ref[lo:lo + DFT_ROWS + extra, :], zp, preferred_element_type=F32)
        q = jnp.dot(s_ref[lo:lo + DFT_ROWS, :], zm, preferred_element_type=F32)
        if extra:
            p = jnp.concatenate([p[:DFT_ROWS] + sgn_z_half, p[DFT_ROWS:] + z_half], axis=0)
            q = jnp.concatenate([q, jnp.zeros((extra, q.shape[1]), F32)], axis=0)
        else:
            p = p + sgn_z_half
        return p.astype(BF16), q.astype(BF16)

    nyq = []

    def mix(i, pq):
        p, q = pq
        lo = i * DFT_ROWS
        for g in range(n_grp):
            cols = slice(g * F_CH, (g + 1) * F_CH)
            lu = jnp.dot(jnp.concatenate([p[:, cols], q[:, cols]], axis=1), mix_ref[g],
                         preferred_element_type=F32)
            o_ref[lo:lo + DFT_ROWS, cols] = lu[:DFT_ROWS, :F_CH].astype(BF16)
            u_ref[lo:lo + DFT_ROWS, cols] = lu[:DFT_ROWS, F_CH:].astype(BF16)
            if lu.shape[0] > DFT_ROWS:
                nyq.append(lu[DFT_ROWS:, :F_CH])

    pending = spectra(0)
    for i in range(n_row):
        cur = pending
        if i + 1 < n_row:
            pending = spectra(i + 1)
        mix(i, cur)
    nyq_row = jnp.concatenate(nyq, axis=1)

    for t in range(n_rev):
        src = half - (t + 1) * REV
        r = jnp.dot(jmat, u_ref[src:src + REV, :], preferred_element_type=F32)
        if t == 0:
            r = _patch_first_row(r, nyq_row)
        else:
            r = _patch_first_row(r, u_ref[src + REV:src + REV + ROW_PACK, :].astype(F32))
        o_ref[half + t * REV:half + (t + 1) * REV, :] = r.astype(BF16)


def _seq_dft(zf, cmat, smat, jmat, mix, tn):
    B, S, _ = zf.shape
    half = S // 2
    assert half % DFT_ROWS == 0 and half % 2 == 0 and tn % F_CH == 0
    return pl.pallas_call(
        functools.partial(_seq_dft_kernel, seq=S),
        grid=(B, F_WIDTH // tn),
        in_specs=[
            pl.BlockSpec((None, S, tn), lambda b, c: (b, 0, c)),
            pl.BlockSpec((half + ROW_PACK, half), lambda b, c: (0, 0), pipeline_mode=pl.Buffered(1)),
            _resident((half, half)),
            _resident((REV, REV)),
            pl.BlockSpec((tn // F_CH, 2 * F_CH, 2 * F_CH), lambda b, c: (c, 0, 0)),
        ],
        out_specs=pl.BlockSpec((None, S, tn), lambda b, c: (b, 0, c)),
        out_shape=jax.ShapeDtypeStruct((B, S, F_WIDTH), BF16),
        scratch_shapes=[pltpu.VMEM((half, tn), BF16),
                        pltpu.VMEM((half, tn), BF16),
                        pltpu.VMEM((half, tn), BF16)],
        compiler_params=_params("parallel", "parallel"),
        name="seq_dft",
    )(zf, cmat, smat, jmat, mix)


def _window_attn_kernel(sink_ref, qt_ref, k_ref, kp_ref, kn_ref, vt_ref, vtp_ref, vtn_ref, g_ref,
                        o_ref, kfull, vtfull, *, tq, n_blocks):
    s_idx = pl.program_id(1)
    r_blocks = tq // BLOCK
    span = 3 * BLOCK

    kfull[0:BLOCK, :] = kp_ref[...]
    kfull[BLOCK:BLOCK + tq, :] = k_ref[...]
    kfull[BLOCK + tq:, :] = kn_ref[...]
    vtfull[:, 0:BLOCK] = vtp_ref[...]
    vtfull[:, BLOCK:BLOCK + tq] = vt_ref[...]
    vtfull[:, BLOCK + tq:] = vtn_ref[...]

    key_i = lax.broadcasted_iota(jnp.int32, (BLOCK, BLOCK), 0)
    qry_i = lax.broadcasted_iota(jnp.int32, (BLOCK, BLOCK), 1)
    assert WINDOW == BLOCK

    def scores(n, h):
        r0 = h * HEAD_DIM
        q4 = jnp.concatenate(
            [qt_ref[(h * GQA_GROUP + g) * HEAD_DIM:(h * GQA_GROUP + g + 1) * HEAD_DIM,
                    n * BLOCK:(n + 1) * BLOCK] for g in range(GQA_GROUP)], axis=1)
        parts = [q4]
        if r0 > 0:
            parts.insert(0, jnp.zeros((r0, GQA_GROUP * BLOCK), BF16))
        if r0 + HEAD_DIM < KV_WIDTH:
            parts.append(jnp.zeros((KV_WIDTH - r0 - HEAD_DIM, GQA_GROUP * BLOCK), BF16))
        qpad = jnp.concatenate(parts, axis=0)
        kwin = kfull[n * BLOCK:n * BLOCK + span, :]
        return jnp.dot(kwin, qpad, preferred_element_type=F32)

    def softmax(n, h, st_all):
        nglob = s_idx * r_blocks + n
        valid_prev = (key_i >= qry_i) & (nglob > 0)
        valid_next = (key_i <= qry_i) & (nglob < n_blocks - 1)
        pn, tail = [], []
        for g in range(GQA_GROUP):
            cols = slice(g * BLOCK, (g + 1) * BLOCK)
            s_prev = jnp.where(valid_prev, st_all[0:BLOCK, cols], NEG_INF)
            s_self = st_all[BLOCK:2 * BLOCK, cols]
            s_next = jnp.where(valid_next, st_all[2 * BLOCK:, cols], NEG_INF)
            sink = jnp.full((1, BLOCK), sink_ref[h * GQA_GROUP + g] * LOG2_E, F32)
            m = jnp.max(jnp.maximum(jnp.maximum(s_prev, s_self), s_next), axis=0, keepdims=True)
            m = jnp.maximum(m, sink)
            tail.append(jnp.exp2(sink - m))
            pn.append(jnp.concatenate([jnp.exp2(s_prev - m), jnp.exp2(s_self - m),
                                       jnp.exp2(s_next - m)], axis=0).astype(BF16))
        return jnp.concatenate(pn, axis=1), jnp.concatenate(tail, axis=1)

    def weighted_values(n, h, p, tail):
        r0 = h * HEAD_DIM
        vwin = vtfull[r0:r0 + HEAD_DIM, n * BLOCK:n * BLOCK + span]
        vext = jnp.concatenate([vwin, jnp.ones((ROW_PACK, span), BF16)], axis=0)
        ot = jnp.dot(vext, p, preferred_element_type=F32)
        den = ot[HEAD_DIM:HEAD_DIM + 1] + tail
        ot = ot[:HEAD_DIM] * (1.0 / den)
        return [ot[:, g * BLOCK:(g + 1) * BLOCK] for g in range(GQA_GROUP)]

    def finish_block(n, heads):
        y = jnp.concatenate(heads, axis=0)
        r = lax.rsqrt(jnp.mean(y * y, axis=0, keepdims=True) + EPS)
        o_ref[n * BLOCK:(n + 1) * BLOCK, :] = ((y * r).T * g_ref[...]).astype(BF16)

    pairs = [(n, h) for n in range(r_blocks) for h in range(N_KV_HEADS)]
    heads = []

    def retire(n, h, p, tail):
        heads.extend(weighted_values(n, h, p, tail))
        if h == N_KV_HEADS - 1:
            finish_block(n, list(heads))
            heads.clear()

    st_next = scores(*pairs[0])
    held = None
    for i, (n, h) in enumerate(pairs):
        st_cur = st_next
        if i + 1 < len(pairs):
            st_next = scores(*pairs[i + 1])
        if held is not None:
            retire(*held)
        held = (n, h) + softmax(n, h, st_cur)
    retire(*held)


def _window_attn(sink, qt, k, vt, g, tq):
    B, _, S = qt.shape
    n_blocks = S // BLOCK
    r_blocks = tq // BLOCK
    prev_blk = lambda s: jnp.maximum(s * r_blocks - 1, 0)
    next_blk = lambda s: jnp.minimum((s + 1) * r_blocks, n_blocks - 1)
    return pl.pallas_call(
        functools.partial(_window_attn_kernel, tq=tq, n_blocks=n_blocks),
        grid=(B, S // tq),
        in_specs=[
            pl.BlockSpec(memory_space=pltpu.SMEM),
            pl.BlockSpec((None, A_WIDTH, tq), lambda b, s: (b, 0, s)),
            pl.BlockSpec((None, tq, KV_WIDTH), lambda b, s: (b, s, 0)),
            pl.BlockSpec((None, BLOCK, KV_WIDTH), lambda b, s: (b, prev_blk(s), 0)),
            pl.BlockSpec((None, BLOCK, KV_WIDTH), lambda b, s: (b, next_blk(s), 0)),
            pl.BlockSpec((None, KV_WIDTH, tq), lambda b, s: (b, 0, s)),
            pl.BlockSpec((None, KV_WIDTH, BLOCK), lambda b, s: (b, 0, prev_blk(s))),
            pl.BlockSpec((None, KV_WIDTH, BLOCK), lambda b, s: (b, 0, next_blk(s))),
            pl.BlockSpec((1, A_WIDTH), lambda b, s: (0, 0)),
        ],
        out_specs=pl.BlockSpec((None, tq, A_WIDTH), lambda b, s: (b, s, 0)),
        out_shape=jax.ShapeDtypeStruct((B, S, A_WIDTH), BF16),
        scratch_shapes=[
            pltpu.VMEM((tq + 2 * BLOCK, KV_WIDTH), BF16),
            pltpu.VMEM((KV_WIDTH, tq + 2 * BLOCK), BF16),
        ],
        compiler_params=_params("parallel", "parallel"),
        name="window_attn",
    )(sink, qt, k, k, k, vt, vt, vt, g)


OUT_SUB = 2


def _out_proj_kernel(of_ref, oa_ref, x_ref, wout_ref, gof_ref, g_ref, x1_ref, h2_ref):
    sub = x_ref.shape[0] // OUT_SUB
    for i in range(OUT_SUB):
        rows = slice(i * sub, (i + 1) * sub)
        of = of_ref[rows, :].astype(F32)
        rf = lax.rsqrt(jnp.mean(of * of, axis=-1, keepdims=True) + EPS)
        of_n = ((of * rf) * gof_ref[...]).astype(BF16)
        y = jnp.dot(of_n, wout_ref[:F_WIDTH, :], preferred_element_type=F32)
        y = y + jnp.dot(oa_ref[rows, :], wout_ref[F_WIDTH:, :], preferred_element_type=F32)
        x1 = x_ref[rows, :] + y
        x1_ref[rows, :] = x1
        r = lax.rsqrt(jnp.mean(x1 * x1, axis=-1, keepdims=True) + EPS)
        h2_ref[rows, :] = ((x1 * r) * g_ref[...]).astype(BF16)


def _out_proj(of, oa_n, x, wout, gof, g, tm):
    T = x.shape[0]
    return pl.pallas_call(
        _out_proj_kernel,
        grid=(T // tm,),
        in_specs=[
            pl.BlockSpec((tm, F_WIDTH), lambda i: (i, 0)),
            pl.BlockSpec((tm, A_WIDTH), lambda i: (i, 0)),
            pl.BlockSpec((tm, D_MODEL), lambda i: (i, 0)),
            _resident((F_WIDTH + A_WIDTH, D_MODEL)),
            _resident((1, F_WIDTH)),
            _resident((1, D_MODEL)),
        ],
        out_specs=[
            pl.BlockSpec((tm, D_MODEL), lambda i: (i, 0)),
            pl.BlockSpec((tm, D_MODEL), lambda i: (i, 0)),
        ],
        out_shape=[
            jax.ShapeDtypeStruct((T, D_MODEL), F32),
            jax.ShapeDtypeStruct((T, D_MODEL), BF16),
        ],
        compiler_params=_params("parallel"),
        name="out_proj",
    )(of, oa_n, x, wout, gof, g)


MLP_CHUNKS = 2
LAST_SUB = 2


def _mlp_kernel(h2_ref, x1_ref, wu_ref, wd_ref, g_ref, o_ref):
    j = pl.program_id(1)
    last = pl.num_programs(1) - 1
    tm = h2_ref.shape[0]
    tf = wu_ref.shape[1]
    cuts = [(c * tf // MLP_CHUNKS, (c + 1) * tf // MLP_CHUNKS) for c in range(MLP_CHUNKS)]

    def contribution(rows, c):
        lo, hi = cuts[c]
        u = jnp.maximum(jnp.dot(h2_ref[rows, :], wu_ref[:, lo:hi], preferred_element_type=F32), 0.0)
        return jnp.dot((u * u).astype(BF16), wd_ref[lo:hi, :], preferred_element_type=F32)

    @pl.when(j == 0)
    def _():
        rows = slice(0, tm)
        o_ref[...] = x1_ref[...] + contribution(rows, 0)
        for c in range(1, MLP_CHUNKS):
            o_ref[...] += contribution(rows, c)

    @pl.when((j > 0) & (j < last))
    def _():
        rows = slice(0, tm)
        for c in range(MLP_CHUNKS):
            o_ref[...] += contribution(rows, c)

    @pl.when(j == last)
    def _():
        sub = tm // LAST_SUB
        for i in range(LAST_SUB):
            rows = slice(i * sub, (i + 1) * sub)
            y = o_ref[rows, :]
            for c in range(MLP_CHUNKS):
                y = y + contribution(rows, c)
            r = lax.rsqrt(jnp.mean(y * y, axis=-1, keepdims=True) + EPS)
            o_ref[rows, :] = (y * r) * g_ref[...]


def _mlp(h2, x1, wu, wd, g, tm, tf):
    T = h2.shape[0]
    assert D_FF // tf >= 2, "the first and last d_ff steps must be different steps"
    return pl.pallas_call(
        _mlp_kernel,
        grid=(T // tm, D_FF // tf),
        in_specs=[
            pl.BlockSpec((tm, D_MODEL), lambda i, j: (i, 0)),
            pl.BlockSpec((tm, D_MODEL), lambda i, j: (i, 0)),
            pl.BlockSpec((D_MODEL, tf), lambda i, j: (0, j)),
            pl.BlockSpec((tf, D_MODEL), lambda i, j: (j, 0)),
            _resident((1, D_MODEL)),
        ],
        out_specs=pl.BlockSpec((tm, D_MODEL), lambda i, j: (i, 0)),
        out_shape=jax.ShapeDtypeStruct((T, D_MODEL), F32),
        compiler_params=_params("parallel", "arbitrary"),
        name="mlp",
    )(h2, x1, wu, wd, g)


def _rope_tables(S):
    inv_freq = 1.0 / (ROPE_THETA ** (jnp.arange(0, HEAD_DIM, 2, dtype=F32) / HEAD_DIM))
    pos = jnp.arange(S, dtype=F32)
    ang = pos[:, None] * inv_freq[None, :]
    ang = jnp.concatenate([ang, ang], axis=-1)
    cos, sin = jnp.cos(ang), jnp.sin(ang)
    sign = jnp.where(jnp.arange(HEAD_DIM) < HALF_DIM, -1.0, 1.0).astype(F32)
    sin = sin * sign[None, :]
    reps = LANES // HEAD_DIM
    return cos.T, sin.T, jnp.tile(cos, (1, reps)), jnp.tile(sin, (1, reps))


def _seq_dft_matrices(S):
    half = S // 2
    lo_n = 64
    hi_n = half // lo_n
    k = jnp.arange(half, dtype=jnp.int32)
    w = 2.0 * math.pi / S
    a_hi = ((jnp.arange(hi_n + 1, dtype=jnp.int32)[:, None] * lo_n * k[None, :]) % S).astype(F32) * w
    a_lo = ((jnp.arange(lo_n, dtype=jnp.int32)[:, None] * k[None, :]) % S).astype(F32) * w
    ch, sh = jnp.cos(a_hi)[:, None, :], jnp.sin(a_hi)[:, None, :]
    cl, sl = jnp.cos(a_lo)[None, :, :], jnp.sin(a_lo)[None, :, :]
    scale = S ** -0.5
    cos = ((ch * cl - sh * sl) * scale).reshape(half + lo_n, half).astype(BF16)
    sin = ((sh[:hi_n] * cl + ch[:hi_n] * sl) * scale).reshape(half, half).astype(BF16)
    i = jnp.arange(REV, dtype=jnp.int32)
    jmat = ((i[:, None] + i[None, :]) == REV).astype(BF16)
    return cos, sin, jmat


def _trunk(x, w, tm_in, tq, dft_tn, tm_out, tm_mlp, tf):
    B, S, _ = x.shape
    cost, sint, cos, sin = _rope_tables(S)
    pending = () if "wu" in w else (w["wu_f32"], w["wd_f32"], w["wout_f32"])
    zf, qt, k, vt, *cast = _in_proj(x, w["g_mix"], w["wf"], w["wqt"], w["wk"], w["wvt"],
                                    cost, sint, cos, sin, tm_in, pending)
    if pending:
        w["wu"], w["wd"], w["wout"] = cast
    of = _seq_dft(zf, *_seq_dft_matrices(S), w["mix"], dft_tn)
    oa_n = _window_attn(w["sink"], qt, k, vt, w["g_oa"], tq)
    T = B * S
    x2 = x.reshape(T, D_MODEL)
    x1, h2 = _out_proj(of.reshape(T, F_WIDTH), oa_n.reshape(T, A_WIDTH), x2,
                       w["wout"], w["g_of"], w["g_mlp"], tm_out)
    y = _mlp(h2, x1, w["wu"], w["wd"], w["g_final"], tm_mlp, tf)
    return y.reshape(B, S, D_MODEL)


def _prepare_weights(ln_mix_g, w_in, w_fourier, attn_sink, out_norm_fourier_g, out_norm_attn_g,
                     w_out, ln_mlp_g, w_up, w_down, ln_final_g):
    wf, wqt, wk, wvt = _in_weights(w_in)
    return {
        "g_mix": ln_mix_g.reshape(1, D_MODEL),
        "wf": wf,
        "wqt": wqt,
        "wk": wk,
        "wvt": wvt,
        "mix": _fourier_weights(w_fourier),
        "sink": attn_sink.astype(F32),
        "g_of": out_norm_fourier_g.reshape(1, F_WIDTH),
        "g_oa": out_norm_attn_g.reshape(1, A_WIDTH),
        "wout_f32": w_out,
        "g_mlp": ln_mlp_g.reshape(1, D_MODEL),
        "wu_f32": w_up,
        "wd_f32": w_down,
        "g_final": ln_final_g.reshape(1, D_MODEL),
    }


def kernel(x_prompt, x_sample, ln_mix_g, w_in, w_fourier, attn_sink, out_norm_fourier_g,
           out_norm_attn_g, w_out, ln_mlp_g, w_up, w_down, ln_final_g):
    assert ln_mix_g.shape[0] == 1, "single-layer block"
    w = _prepare_weights(ln_mix_g[0], w_in[0], w_fourier[0], attn_sink[0], out_norm_fourier_g[0],
                         out_norm_attn_g[0], w_out[0], ln_mlp_g[0], w_up[0], w_down[0], ln_final_g)
    tiles = dict(tm_in=1024, tq=2048, dft_tn=512, tm_out=512, tm_mlp=512, tf=2048)
    return (_trunk(x_prompt, w, **tiles), _trunk(x_sample, w, **tiles))
```

```python
import functools
import math

import jax
import jax.numpy as jnp
from jax import lax
from jax.experimental import pallas as pl
from jax.experimental.pallas import tpu as pltpu

D_MODEL = 2048
F_WIDTH = 1024
F_GROUPS = 8
F_CH = 128
A_WIDTH = 1024
HEAD_DIM = 64
HALF_DIM = HEAD_DIM // 2
N_Q_HEADS = 16
N_KV_HEADS = 4
GQA_GROUP = 4
KV_WIDTH = 256
WINDOW = 128
BLOCK = 128
ROPE_THETA = 10000.0
D_FF = 4 * D_MODEL
EPS = 1e-6
NEG_INF = -1e30
LOG2_E = math.log2(math.e)

LANES = 128
ROW_PACK = 16
VMEM_LIMIT = 60 * 1024 * 1024

BF16 = jnp.bfloat16
F32 = jnp.float32

_NT = (((1,), (1,)), ((), ()))


def _params(*sem, flags=None):
    return pltpu.CompilerParams(dimension_semantics=sem, vmem_limit_bytes=VMEM_LIMIT, flags=flags)


def _resident(shape):
    nd = len(shape)
    return pl.BlockSpec(shape, lambda *_: (0,) * nd, pipeline_mode=pl.Buffered(1))


def _fourier_weights_kernel(cs_ref, w_ref, o_ref):
    for g in range(F_GROUPS):
        w = w_ref[g]
        c = jnp.dot(cs_ref[0], w, preferred_element_type=F32, precision=lax.Precision.HIGHEST)
        s = jnp.dot(cs_ref[1], w, preferred_element_type=F32, precision=lax.Precision.HIGHEST)
        o_ref[g, :F_CH, :F_CH] = c.astype(BF16)
        o_ref[g, :F_CH, F_CH:] = c.astype(BF16)
        o_ref[g, F_CH:, :F_CH] = (-s).astype(BF16)
        o_ref[g, F_CH:, F_CH:] = s.astype(BF16)


def _fourier_weights(w_fourier):
    n = jnp.arange(F_CH, dtype=jnp.int32)
    ang = ((n[:, None] * n[None, :]) % F_CH).astype(F32) * (2.0 * math.pi / F_CH)
    cs = jnp.stack([jnp.cos(ang), jnp.sin(ang)]) * (F_CH ** -0.5)
    return pl.pallas_call(
        _fourier_weights_kernel,
        out_shape=jax.ShapeDtypeStruct((F_GROUPS, 2 * F_CH, 2 * F_CH), BF16),
        name="fourier_weights",
    )(cs, w_fourier)


IN_W_ROWS = 256


def _in_weights_kernel(w_ref, wf_ref, wqt_ref, wk_ref, wvt_ref):
    o1, o2, o3 = F_WIDTH, F_WIDTH + A_WIDTH, F_WIDTH + A_WIDTH + KV_WIDTH
    wf_ref[...] = w_ref[:, :o1].astype(BF16)
    wqt_ref[...] = w_ref[:, o1:o2].T.astype(BF16)
    wk_ref[...] = w_ref[:, o2:o3].astype(BF16)
    wvt_ref[...] = w_ref[:, o3:].T.astype(BF16)


def _in_weights(w_in):
    d, n = w_in.shape
    return pl.pallas_call(
        _in_weights_kernel,
        grid=(d // IN_W_ROWS,),
        in_specs=[pl.BlockSpec((IN_W_ROWS, n), lambda i: (i, 0))],
        out_specs=[
            pl.BlockSpec((IN_W_ROWS, F_WIDTH), lambda i: (i, 0)),
            pl.BlockSpec((A_WIDTH, IN_W_ROWS), lambda i: (0, i)),
            pl.BlockSpec((IN_W_ROWS, KV_WIDTH), lambda i: (i, 0)),
            pl.BlockSpec((KV_WIDTH, IN_W_ROWS), lambda i: (0, i)),
        ],
        out_shape=[
            jax.ShapeDtypeStruct((d, F_WIDTH), BF16),
            jax.ShapeDtypeStruct((A_WIDTH, d), BF16),
            jax.ShapeDtypeStruct((d, KV_WIDTH), BF16),
            jax.ShapeDtypeStruct((KV_WIDTH, d), BF16),
        ],
        compiler_params=_params("parallel"),
        name="in_weights",
    )(w_in)


IN_SUB = 4


def _in_proj_kernel(x_ref, g_ref, wf_ref, wqt_ref, wk_ref, wvt_ref,
                    cost_ref, sint_ref, cos_ref, sin_ref, *rest, n_cast):
    zf_ref, qt_ref, k_ref, vt_ref = rest[n_cast:n_cast + 4]
    for src_ref, dst_ref in zip(rest[:n_cast], rest[n_cast + 4:]):
        dst_ref[...] = src_ref[...].astype(BF16)

    tm = x_ref.shape[0]
    sub = tm // IN_SUB
    scale = HEAD_DIM ** -0.5 * LOG2_E
    first_half = (lax.broadcasted_iota(jnp.int32, (sub, LANES), 1) % HEAD_DIM) < HALF_DIM
    for i in range(IN_SUB):
        rows = slice(i * sub, (i + 1) * sub)
        x = x_ref[rows, :]
        r = lax.rsqrt(jnp.mean(x * x, axis=-1, keepdims=True) + EPS)
        h = ((x * r) * g_ref[...]).astype(BF16)

        zf = jnp.dot(h, wf_ref[...], preferred_element_type=F32)
        qt = lax.dot_general(wqt_ref[...], h, _NT, preferred_element_type=F32)
        kk = jnp.dot(h, wk_ref[...], preferred_element_type=F32)
        vt = lax.dot_general(wvt_ref[...], h, _NT, preferred_element_type=F32)

        zf_ref[rows, :] = zf.astype(BF16)

        cost = cost_ref[:, rows]
        sint = sint_ref[:, rows]
        for hd in range(N_Q_HEADS):
            lo = hd * HEAD_DIM
            blk = qt[lo:lo + HEAD_DIM]
            rot = jnp.concatenate([blk[HALF_DIM:], blk[:HALF_DIM]], axis=0)
            qt_ref[lo:lo + HEAD_DIM, rows] = ((blk * cost + rot * sint) * scale).astype(BF16)

        cos = cos_ref[rows, :]
        sin = sin_ref[rows, :]
        for c in range(KV_WIDTH // LANES):
            blk = kk[:, c * LANES:(c + 1) * LANES]
            rot = jnp.where(first_half,
                            pltpu.roll(blk, LANES - HALF_DIM, axis=1),
                            pltpu.roll(blk, HALF_DIM, axis=1))
            k_ref[rows, c * LANES:(c + 1) * LANES] = (blk * cos + rot * sin).astype(BF16)

        vt_ref[:, rows] = vt.astype(BF16)


def _in_proj(x, g, wf, wqt, wk, wvt, cost, sint, cos, sin, tm, cast_weights=()):
    B, S, _ = x.shape
    n_s = S // tm
    in_specs = [
        pl.BlockSpec((None, tm, D_MODEL), lambda b, s: (b, s, 0)),
        _resident((1, D_MODEL)),
        _resident((D_MODEL, F_WIDTH)),
        _resident((A_WIDTH, D_MODEL)),
        _resident((D_MODEL, KV_WIDTH)),
        _resident((KV_WIDTH, D_MODEL)),
        pl.BlockSpec((HEAD_DIM, tm), lambda b, s: (0, s)),
        pl.BlockSpec((HEAD_DIM, tm), lambda b, s: (0, s)),
        pl.BlockSpec((tm, LANES), lambda b, s: (s, 0)),
        pl.BlockSpec((tm, LANES), lambda b, s: (s, 0)),
    ]
    out_specs = [
        pl.BlockSpec((None, tm, F_WIDTH), lambda b, s: (b, s, 0)),
        pl.BlockSpec((None, A_WIDTH, tm), lambda b, s: (b, 0, s)),
        pl.BlockSpec((None, tm, KV_WIDTH), lambda b, s: (b, s, 0)),
        pl.BlockSpec((None, KV_WIDTH, tm), lambda b, s: (b, 0, s)),
    ]
    out_shape = [
        jax.ShapeDtypeStruct((B, S, F_WIDTH), BF16),
        jax.ShapeDtypeStruct((B, A_WIDTH, S), BF16),
        jax.ShapeDtypeStruct((B, S, KV_WIDTH), BF16),
        jax.ShapeDtypeStruct((B, KV_WIDTH, S), BF16),
    ]
    args = [x, g, wf, wqt, wk, wvt, cost, sint, cos, sin]
    if cast_weights:
        n_steps = B * n_s
        for wgt in cast_weights:
            rows, cols = wgt.shape
            assert rows % (n_steps * ROW_PACK) == 0
            spec = pl.BlockSpec((rows // n_steps, cols), lambda b, s: (b * n_s + s, 0))
            in_specs.append(spec)
            out_specs.append(spec)
            out_shape.append(jax.ShapeDtypeStruct(wgt.shape, BF16))
            args.append(wgt)
    return pl.pallas_call(
        functools.partial(_in_proj_kernel, n_cast=len(cast_weights)),
        grid=(B, n_s),
        in_specs=in_specs,
        out_specs=out_specs,
        out_shape=out_shape,
        compiler_params=_params("parallel", "parallel"),
        name="in_proj",
    )(*args)


REV = 256
DFT_ROWS = 512


def _first_row(x):
    return jnp.where(lax.broadcasted_iota(jnp.int32, x.shape, 0) == 0, x, 0.0)


def _patch_first_row(tile, row):
    head = tile[:ROW_PACK] + _first_row(row)
    return jnp.concatenate([head, tile[ROW_PACK:]], axis=0)


def _seq_dft_kernel(z_ref, c_ref, s_ref, j_ref, mix_ref, o_ref, zp_ref, zm_ref, u_ref, *, seq):
    half = seq // 2
    n_rev = half // REV
    n_grp = z_ref.shape[1] // F_CH
    jmat = j_ref[...]

    for t in range(n_rev):
        lo = t * REV
        src = seq - lo - REV
        r = jnp.dot(jmat, z_ref[src:src + REV, :], preferred_element_type=F32)
        if t > 0:
            r = _patch_first_row(r, z_ref[seq - lo:seq - lo + ROW_PACK, :].astype(F32))
        z_lo = z_ref[lo:lo + REV, :].astype(F32)
        zp_ref[lo:lo + REV, :] = (z_lo + r).astype(BF16)
        zm_ref[lo:lo + REV, :] = (z_lo - r).astype(BF16)

    z_half = z_ref[half:half + ROW_PACK, :].astype(F32)[0:1, :] * (seq ** -0.5)
    row_par = lax.broadcasted_iota(jnp.int32, (DFT_ROWS, 1), 0) % 2
    sgn_z_half = jnp.where(row_par == 0, z_half, -z_half)

    zp = zp_ref[...]
    zm = zm_ref[...]
    n_row = half // DFT_ROWS

    def spectra(i):
        lo = i * DFT_ROWS
        extra = ROW_PACK if i == n_row - 1 else 0
        p = jnp.dot(c_ref[lo:lo + DFT_ROWS + extra, :], zp, preferred_element_type=F32)
        q = jnp.dot(s_ref[lo:lo + DFT_ROWS, :], zm, preferred_element_type=F32)
        if extra:
            p = jnp.concatenate([p[:DFT_ROWS] + sgn_z_half, p[DFT_ROWS:] + z_half], axis=0)
            q = jnp.concatenate([q, jnp.zeros((extra, q.shape[1]), F32)], axis=0)
        else:
            p = p + sgn_z_half
        return p.astype(BF16), q.astype(BF16)

    nyq = []

    def mix(i, pq):
        p, q = pq
        lo = i * DFT_ROWS
        for g in range(n_grp):
            cols = slice(g * F_CH, (g + 1) * F_CH)
            lu = jnp.dot(jnp.concatenate([p[:, cols], q[:, cols]], axis=1), mix_ref[g],
                         preferred_element_type=F32)
            o_ref[lo:lo + DFT_ROWS, cols] = lu[:DFT_ROWS, :F_CH].astype(BF16)
            u_ref[lo:lo + DFT_ROWS, cols] = lu[:DFT_ROWS, F_CH:].astype(BF16)
            if lu.shape[0] > DFT_ROWS:
                nyq.append(lu[DFT_ROWS:, :F_CH])

    pending = spectra(0)
    for i in range(n_row):
        cur = pending
        if i + 1 < n_row:
            pending = spectra(i + 1)
        mix(i, cur)
    nyq_row = jnp.concatenate(nyq, axis=1)

    for t in range(n_rev):
        src = half - (t + 1) * REV
        r = jnp.dot(jmat, u_ref[src:src + REV, :], preferred_element_type=F32)
        if t == 0:
            r = _patch_first_row(r, nyq_row)
        else:
            r = _patch_first_row(r, u_ref[src + REV:src + REV + ROW_PACK, :].astype(F32))
        o_ref[half + t * REV:half + (t + 1) * REV, :] = r.astype(BF16)


def _seq_dft(zf, cmat, smat, jmat, mix, tn):
    B, S, _ = zf.shape
    half = S // 2
    assert half % DFT_ROWS == 0 and half % 2 == 0 and tn % F_CH == 0
    return pl.pallas_call(
        functools.partial(_seq_dft_kernel, seq=S),
        grid=(B, F_WIDTH // tn),
        in_specs=[
            pl.BlockSpec((None, S, tn), lambda b, c: (b, 0, c)),
            pl.BlockSpec((half + ROW_PACK, half), lambda b, c: (0, 0), pipeline_mode=pl.Buffered(1)),
            _resident((half, half)),
            _resident((REV, REV)),
            pl.BlockSpec((tn // F_CH, 2 * F_CH, 2 * F_CH), lambda b, c: (c, 0, 0)),
        ],
        out_specs=pl.BlockSpec((None, S, tn), lambda b, c: (b, 0, c)),
        out_shape=jax.ShapeDtypeStruct((B, S, F_WIDTH), BF16),
        scratch_shapes=[pltpu.VMEM((half, tn), BF16),
                        pltpu.VMEM((half, tn), BF16),
                        pltpu.VMEM((half, tn), BF16)],
        compiler_params=_params("parallel", "parallel"),
        name="seq_dft",
    )(zf, cmat, smat, jmat, mix)


def _window_attn_kernel(sink_ref, qt_ref, k_ref, kp_ref, kn_ref, vt_ref, vtp_ref, vtn_ref, g_ref,
                        o_ref, kfull, vtfull, *, tq, n_blocks):
    s_idx = pl.program_id(1)
    r_blocks = tq // BLOCK
    span = 3 * BLOCK

    kfull[0:BLOCK, :] = kp_ref[...]
    kfull[BLOCK:BLOCK + tq, :] = k_ref[...]
    kfull[BLOCK + tq:, :] = kn_ref[...]
    vtfull[:, 0:BLOCK] = vtp_ref[...]
    vtfull[:, BLOCK:BLOCK + tq] = vt_ref[...]
    vtfull[:, BLOCK + tq:] = vtn_ref[...]

    key_i = lax.broadcasted_iota(jnp.int32, (BLOCK, BLOCK), 0)
    qry_i = lax.broadcasted_iota(jnp.int32, (BLOCK, BLOCK), 1)
    assert WINDOW == BLOCK

    def scores(n, h):
        r0 = h * HEAD_DIM
        q4 = jnp.concatenate(
            [qt_ref[(h * GQA_GROUP + g) * HEAD_DIM:(h * GQA_GROUP + g + 1) * HEAD_DIM,
                    n * BLOCK:(n + 1) * BLOCK] for g in range(GQA_GROUP)], axis=1)
        parts = [q4]
        if r0 > 0:
            parts.insert(0, jnp.zeros((r0, GQA_GROUP * BLOCK), BF16))
        if r0 + HEAD_DIM < KV_WIDTH:
            parts.append(jnp.zeros((KV_WIDTH - r0 - HEAD_DIM, GQA_GROUP * BLOCK), BF16))
        qpad = jnp.concatenate(parts, axis=0)
        kwin = kfull[n * BLOCK:n * BLOCK + span, :]
        return jnp.dot(kwin, qpad, preferred_element_type=F32)

    def softmax(n, h, st_all):
        nglob = s_idx * r_blocks + n
        valid_prev = (key_i >= qry_i) & (nglob > 0)
        valid_next = (key_i <= qry_i) & (nglob < n_blocks - 1)
        pn, tail = [], []
        for g in range(GQA_GROUP):
            cols = slice(g * BLOCK, (g + 1) * BLOCK)
            s_prev = jnp.where(valid_prev, st_all[0:BLOCK, cols], NEG_INF)
            s_self = st_all[BLOCK:2 * BLOCK, cols]
            s_next = jnp.where(valid_next, st_all[2 * BLOCK:, cols], NEG_INF)
            sink = jnp.full((1, BLOCK), sink_ref[h * GQA_GROUP + g] * LOG2_E, F32)
            m = jnp.max(jnp.maximum(jnp.maximum(s_prev, s_self), s_next), axis=0, keepdims=True)
            m = jnp.maximum(m, sink)
            tail.append(jnp.exp2(sink - m))
            pn.append(jnp.concatenate([jnp.exp2(s_prev - m), jnp.exp2(s_self - m),
                                       jnp.exp2(s_next - m)], axis=0).astype(BF16))
        return jnp.concatenate(pn, axis=1), jnp.concatenate(tail, axis=1)

    def weighted_values(n, h, p, tail):
        r0 = h * HEAD_DIM
        vwin = vtfull[r0:r0 + HEAD_DIM, n * BLOCK:n * BLOCK + span]
        vext = jnp.concatenate([vwin, jnp.ones((ROW_PACK, span), BF16)], axis=0)
        ot = jnp.dot(vext, p, preferred_element_type=F32)
        den = ot[HEAD_DIM:HEAD_DIM + 1] + tail
        ot = ot[:HEAD_DIM] * (1.0 / den)
        return [ot[:, g * BLOCK:(g + 1) * BLOCK] for g in range(GQA_GROUP)]

    def finish_block(n, heads):
        y = jnp.concatenate(heads, axis=0)
        r = lax.rsqrt(jnp.mean(y * y, axis=0, keepdims=True) + EPS)
        o_ref[n * BLOCK:(n + 1) * BLOCK, :] = ((y * r).T * g_ref[...]).astype(BF16)

    pairs = [(n, h) for n in range(r_blocks) for h in range(N_KV_HEADS)]
    heads = []

    def retire(n, h, p, tail):
        heads.extend(weighted_values(n, h, p, tail))
        if h == N_KV_HEADS - 1:
            finish_block(n, list(heads))
            heads.clear()

    st_next = scores(*pairs[0])
    held = None
    for i, (n, h) in enumerate(pairs):
        st_cur = st_next
        if i + 1 < len(pairs):
            st_next = scores(*pairs[i + 1])
        if held is not None:
            retire(*held)
        held = (n, h) + softmax(n, h, st_cur)
    retire(*held)


def _window_attn(sink, qt, k, vt, g, tq):
    B, _, S = qt.shape
    n_blocks = S // BLOCK
    r_blocks = tq // BLOCK
    prev_blk = lambda s: jnp.maximum(s * r_blocks - 1, 0)
    next_blk = lambda s: jnp.minimum((s + 1) * r_blocks, n_blocks - 1)
    return pl.pallas_call(
        functools.partial(_window_attn_kernel, tq=tq, n_blocks=n_blocks),
        grid=(B, S // tq),
        in_specs=[
            pl.BlockSpec(memory_space=pltpu.SMEM),
            pl.BlockSpec((None, A_WIDTH, tq), lambda b, s: (b, 0, s)),
            pl.BlockSpec((None, tq, KV_WIDTH), lambda b, s: (b, s, 0)),
            pl.BlockSpec((None, BLOCK, KV_WIDTH), lambda b, s: (b, prev_blk(s), 0)),
            pl.BlockSpec((None, BLOCK, KV_WIDTH), lambda b, s: (b, next_blk(s), 0)),
            pl.BlockSpec((None, KV_WIDTH, tq), lambda b, s: (b, 0, s)),
            pl.BlockSpec((None, KV_WIDTH, BLOCK), lambda b, s: (b, 0, prev_blk(s))),
            pl.BlockSpec((None, KV_WIDTH, BLOCK), lambda b, s: (b, 0, next_blk(s))),
            pl.BlockSpec((1, A_WIDTH), lambda b, s: (0, 0)),
        ],
        out_specs=pl.BlockSpec((None, tq, A_WIDTH), lambda b, s: (b, s, 0)),
        out_shape=jax.ShapeDtypeStruct((B, S, A_WIDTH), BF16),
        scratch_shapes=[
            pltpu.VMEM((tq + 2 * BLOCK, KV_WIDTH), BF16),
            pltpu.VMEM((KV_WIDTH, tq + 2 * BLOCK), BF16),
        ],
        compiler_params=_params("parallel", "parallel"),
        name="window_attn",
    )(sink, qt, k, k, k, vt, vt, vt, g)


OUT_SUB = 2


def _out_proj_kernel(of_ref, oa_ref, x_ref, wout_ref, gof_ref, g_ref, x1_ref, h2_ref):
    sub = x_ref.shape[0] // OUT_SUB
    for i in range(OUT_SUB):
        rows = slice(i * sub, (i + 1) * sub)
        of = of_ref[rows, :].astype(F32)
        rf = lax.rsqrt(jnp.mean(of * of, axis=-1, keepdims=True) + EPS)
        of_n = ((of * rf) * gof_ref[...]).astype(BF16)
        y = jnp.dot(of_n, wout_ref[:F_WIDTH, :], preferred_element_type=F32)
        y = y + jnp.dot(oa_ref[rows, :], wout_ref[F_WIDTH:, :], preferred_element_type=F32)
        x1 = x_ref[rows, :] + y
        x1_ref[rows, :] = x1
        r = lax.rsqrt(jnp.mean(x1 * x1, axis=-1, keepdims=True) + EPS)
        h2_ref[rows, :] = ((x1 * r) * g_ref[...]).astype(BF16)


def _out_proj(of, oa_n, x, wout, gof, g, tm):
    T = x.shape[0]
    return pl.pallas_call(
        _out_proj_kernel,
        grid=(T // tm,),
        in_specs=[
            pl.BlockSpec((tm, F_WIDTH), lambda i: (i, 0)),
            pl.BlockSpec((tm, A_WIDTH), lambda i: (i, 0)),
            pl.BlockSpec((tm, D_MODEL), lambda i: (i, 0)),
            _resident((F_WIDTH + A_WIDTH, D_MODEL)),
            _resident((1, F_WIDTH)),
            _resident((1, D_MODEL)),
        ],
        out_specs=[
            pl.BlockSpec((tm, D_MODEL), lambda i: (i, 0)),
            pl.BlockSpec((tm, D_MODEL), lambda i: (i, 0)),
        ],
        out_shape=[
            jax.ShapeDtypeStruct((T, D_MODEL), F32),
            jax.ShapeDtypeStruct((T, D_MODEL), BF16),
        ],
        compiler_params=_params("parallel"),
        name="out_proj",
    )(of, oa_n, x, wout, gof, g)


MLP_CHUNKS = 1
LAST_SUB = 2


def _mlp_kernel(h2_ref, x1_ref, wu_ref, wd_ref, g_ref, o_ref):
    j = pl.program_id(1)
    last = pl.num_programs(1) - 1
    tm = h2_ref.shape[0]
    tf = wu_ref.shape[1]
    cuts = [(c * tf // MLP_CHUNKS, (c + 1) * tf // MLP_CHUNKS) for c in range(MLP_CHUNKS)]

    def contribution(rows, c):
        lo, hi = cuts[c]
        u = jnp.maximum(jnp.dot(h2_ref[rows, :], wu_ref[:, lo:hi], preferred_element_type=F32), 0.0)
        return jnp.dot((u * u).astype(BF16), wd_ref[lo:hi, :], preferred_element_type=F32)

    @pl.when(j == 0)
    def _():
        rows = slice(0, tm)
        o_ref[...] = x1_ref[...] + contribution(rows, 0)
        for c in range(1, MLP_CHUNKS):
            o_ref[...] += contribution(rows, c)

    @pl.when((j > 0) & (j < last))
    def _():
        rows = slice(0, tm)
        for c in range(MLP_CHUNKS):
            o_ref[...] += contribution(rows, c)

    @pl.when(j == last)
    def _():
        sub = tm // LAST_SUB
        for i in range(LAST_SUB):
            rows = slice(i * sub, (i + 1) * sub)
            y = o_ref[rows, :]
            for c in range(MLP_CHUNKS):
                y = y + contribution(rows, c)
            r = lax.rsqrt(jnp.mean(y * y, axis=-1, keepdims=True) + EPS)
            o_ref[rows, :] = (y * r) * g_ref[...]


def _mlp(h2, x1, wu, wd, g, tm, tf):
    T = h2.shape[0]
    assert D_FF // tf >= 2, "the first and last d_ff steps must be different steps"
    return pl.pallas_call(
        _mlp_kernel,
        grid=(T // tm, D_FF // tf),
        in_specs=[
            pl.BlockSpec((tm, D_MODEL), lambda i, j: (i, 0)),
            pl.BlockSpec((tm, D_MODEL), lambda i, j: (i, 0)),
            pl.BlockSpec((D_MODEL, tf), lambda i, j: (0, j)),
            pl.BlockSpec((tf, D_MODEL), lambda i, j: (j, 0)),
            _resident((1, D_MODEL)),
        ],
        out_specs=pl.BlockSpec((tm, D_MODEL), lambda i, j: (i, 0)),
        out_shape=jax.ShapeDtypeStruct((T, D_MODEL), F32),
        compiler_params=_params("parallel", "arbitrary"),
        name="mlp",
    )(h2, x1, wu, wd, g)


def _rope_tables(S):
    inv_freq = 1.0 / (ROPE_THETA ** (jnp.arange(0, HEAD_DIM, 2, dtype=F32) / HEAD_DIM))
    pos = jnp.arange(S, dtype=F32)
    ang = pos[:, None] * inv_freq[None, :]
    ang = jnp.concatenate([ang, ang], axis=-1)
    cos, sin = jnp.cos(ang), jnp.sin(ang)
    sign = jnp.where(jnp.arange(HEAD_DIM) < HALF_DIM, -1.0, 1.0).astype(F32)
    sin = sin * sign[None, :]
    reps = LANES // HEAD_DIM
    return cos.T, sin.T, jnp.tile(cos, (1, reps)), jnp.tile(sin, (1, reps))


def _seq_dft_matrices(S):
    half = S // 2
    lo_n = 64
    hi_n = half // lo_n
    k = jnp.arange(half, dtype=jnp.int32)
    w = 2.0 * math.pi / S
    a_hi = ((jnp.arange(hi_n + 1, dtype=jnp.int32)[:, None] * lo_n * k[None, :]) % S).astype(F32) * w
    a_lo = ((jnp.arange(lo_n, dtype=jnp.int32)[:, None] * k[None, :]) % S).astype(F32) * w
    ch, sh = jnp.cos(a_hi)[:, None, :], jnp.sin(a_hi)[:, None, :]
    cl, sl = jnp.cos(a_lo)[None, :, :], jnp.sin(a_lo)[None, :, :]
    scale = S ** -0.5
    cos = ((ch * cl - sh * sl) * scale).reshape(half + lo_n, half).astype(BF16)
    sin = ((sh[:hi_n] * cl + ch[:hi_n] * sl) * scale).reshape(half, half).astype(BF16)
    i = jnp.arange(REV, dtype=jnp.int32)
    jmat = ((i[:, None] + i[None, :]) == REV).astype(BF16)
    return cos, sin, jmat


def _trunk(x, w, tm_in, tq, dft_tn, tm_out, tm_mlp, tf):
    B, S, _ = x.shape
    cost, sint, cos, sin = _rope_tables(S)
    pending = () if "wu" in w else (w["wu_f32"], w["wd_f32"], w["wout_f32"])
    zf, qt, k, vt, *cast = _in_proj(x, w["g_mix"], w["wf"], w["wqt"], w["wk"], w["wvt"],
                                    cost, sint, cos, sin, tm_in, pending)
    if pending:
        w["wu"], w["wd"], w["wout"] = cast
    of = _seq_dft(zf, *_seq_dft_matrices(S), w["mix"], dft_tn)
    oa_n = _window_attn(w["sink"], qt, k, vt, w["g_oa"], tq)
    T = B * S
    x2 = x.reshape(T, D_MODEL)
    x1, h2 = _out_proj(of.reshape(T, F_WIDTH), oa_n.reshape(T, A_WIDTH), x2,
                       w["wout"], w["g_of"], w["g_mlp"], tm_out)
    y = _mlp(h2, x1, w["wu"], w["wd"], w["g_final"], tm_mlp, tf)
    return y.reshape(B, S, D_MODEL)


def _prepare_weights(ln_mix_g, w_in, w_fourier, attn_sink, out_norm_fourier_g, out_norm_attn_g,
                     w_out, ln_mlp_g, w_up, w_down, ln_final_g):
    wf, wqt, wk, wvt = _in_weights(w_in)
    return {
        "g_mix": ln_mix_g.reshape(1, D_MODEL),
        "wf": wf,
        "wqt": wqt,
        "wk": wk,
        "wvt": wvt,
        "mix": _fourier_weights(w_fourier),
        "sink": attn_sink.astype(F32),
        "g_of": out_norm_fourier_g.reshape(1, F_WIDTH),
        "g_oa": out_norm_attn_g.reshape(1, A_WIDTH),
        "wout_f32": w_out,
        "g_mlp": ln_mlp_g.reshape(1, D_MODEL),
        "wu_f32": w_up,
        "wd_f32": w_down,
        "g_final": ln_final_g.reshape(1, D_MODEL),
    }


def kernel(x_prompt, x_sample, ln_mix_g, w_in, w_fourier, attn_sink, out_norm_fourier_g,
           out_norm_attn_g, w_out, ln_mlp_g, w_up, w_down, ln_final_g):
    assert ln_mix_g.shape[0] == 1, "single-layer block"
    w = _prepare_weights(ln_mix_g[0], w_in[0], w_fourier[0], attn_sink[0], out_norm_fourier_g[0],
                         out_norm_attn_g[0], w_out[0], ln_mlp_g[0], w_up[0], w_down[0], ln_final_g)
    tiles = dict(tm_in=1024, tq=2048, dft_tn=512, tm_out=512, tm_mlp=512, tf=2048)
    return (_trunk(x_prompt, w, **tiles), _trunk(x_sample, w, **tiles))
```
